```python
import math
import jax, jax.numpy as jnp
from jax import lax
import numpy as np

D_MODEL = 1024
BATCH = 16
SEQ = 2048
DEPTH = 2
DEC_BATCH = 8
DEC_SEQ = 64
PAST_LEN = 4096

CHUNK = 64
N_AB = (DEPTH + 1) // 2
N_S5 = DEPTH // 2
LRU_WIDTH = D_MODEL // 2
LRU_HEADS = 8
LRU_HEAD_DIM = LRU_WIDTH // LRU_HEADS
CONV_WIDTH = 4
RG_C = 8.0
HG_WIDTH = D_MODEL // 2
HG_HEADS = 4
HG_DK = HG_WIDTH // HG_HEADS
HG_DV = HG_WIDTH // HG_HEADS
HG_BLOCK = 16
RMS_EPS = 1e-6
S5_WIDTH = D_MODEL
S5_GROUP = 16
S5_GROUPS = S5_WIDTH // S5_GROUP
S5_STATE = 64
N_EXPERTS = 16
N_EXPERT_GROUPS = 4
EXPERTS_PER_GROUP = N_EXPERTS // N_EXPERT_GROUPS
TOP_K = 2
D_EXPERT = 512
ALPHA = (2.0 * DEPTH) ** 0.25
BETA = (8.0 * DEPTH) ** -0.25
LN_EPS = 1e-5
D_IN_AB = 2 * LRU_WIDTH + 4 * HG_WIDTH
AB_SPLITS = [LRU_WIDTH, 2 * LRU_WIDTH, 2 * LRU_WIDTH + HG_WIDTH,
             2 * LRU_WIDTH + 2 * HG_WIDTH, 2 * LRU_WIDTH + 3 * HG_WIDTH]

kernel_name = 'hybrid_rglru_hgrn2_s5_moe_stream_step'

F32 = jnp.float32


def _post_norm(x, f, g, b):
    s = ALPHA * x.astype(F32) + f.astype(F32)
    mu = jnp.mean(s, -1, keepdims=True)
    var = jnp.mean(jnp.square(s - mu), -1, keepdims=True)
    return ((s - mu) * lax.rsqrt(var + LN_EPS) * g.astype(F32) + b.astype(F32)).astype(x.dtype)


def _linear_scan(a, b, h0):
    b = b.at[:, 0].add(a[:, 0] * h0)
    def combine(l, r):
        a_l, b_l = l
        a_r, b_r = r
        return a_l * a_r, a_r * b_l + b_r
    _, h = lax.associative_scan(combine, (a, b), axis=1)
    return h


def _causal_dwconv(u, buf, w, bias):
    T = u.shape[1]
    up = jnp.concatenate([buf.astype(F32), u.astype(F32)], axis=1)
    y = bias.astype(F32) + sum(w[k].astype(F32) * up[:, k:k + T] for k in range(CONV_WIDTH))
    return y, up[:, T:]


def _rg_lru(u, h0, w_r, b_r, w_i, b_i, lam):
    Bn, T, _ = u.shape
    uh = u.reshape(Bn, T, LRU_HEADS, LRU_HEAD_DIM)
    r = jax.nn.sigmoid(jnp.einsum('bthi,hij->bthj', uh, w_r.astype(F32)).reshape(Bn, T, LRU_WIDTH) + b_r.astype(F32))
    i = jax.nn.sigmoid(jnp.einsum('bthi,hij->bthj', uh, w_i.astype(F32)).reshape(Bn, T, LRU_WIDTH) + b_i.astype(F32))
    log_a = -RG_C * r * jax.nn.softplus(-lam.astype(F32))
    a = jnp.exp(log_a)
    mult = jnp.sqrt(-jnp.expm1(2.0 * log_a))
    h = _linear_scan(a, mult * i * u, h0.astype(F32))
    return h, h[:, -1]


def _hgrn2(q, log_f, v, s0):
    Bn, T, H, _ = q.shape
    L = math.gcd(T, HG_BLOCK)
    n = T // L
    k = -jnp.expm1(log_f)
    blk = lambda t: t.reshape(Bn, n, L, H, t.shape[-1])
    q, log_f, k, v = blk(q), blk(log_f), blk(k), blk(v)
    G = jnp.cumsum(log_f, axis=2)
    q_dec = q * jnp.exp(G)
    k_inc = k * jnp.exp(-G)
    causal = jnp.tril(jnp.ones((L, L), bool))
    scores = jnp.where(causal, jnp.einsum('bnthk,bnshk->bnhts', q_dec, k_inc), 0.0)
    o_intra = jnp.einsum('bnhts,bnshv->bnthv', scores, v)
    G_last = G[:, :, -1]
    k_tail = k * jnp.exp(G_last[:, :, None] - G)
    def step(S, inp):
        qd, kt, vb, decay = inp
        o = jnp.einsum('bthk,bhkv->bthv', qd, S)
        S = decay[..., None] * S + jnp.einsum('bshk,bshv->bhkv', kt, vb)
        return S, o
    mv = lambda t: jnp.moveaxis(t, 1, 0)
    s_final, o_inter = lax.scan(step, s0.astype(F32), (mv(q_dec), mv(k_tail), mv(v), mv(jnp.exp(G_last))))
    o = o_intra + jnp.moveaxis(o_inter, 0, 1)
    return o.reshape(Bn, T, H, -1), s_final


def _ab_mixer(x, h0, conv0, s0, w_in, w_out, conv_w, conv_b, w_r, b_r, w_i, b_i, lam, lb, norm_g):
    Bn, T, _ = x.shape
    z = x @ w_in
    z_gate, z_x, z_q, z_f, z_i, z_g = jnp.split(z, AB_SPLITS, axis=-1)
    u, conv_new = _causal_dwconv(z_x, conv0, conv_w, conv_b)
    h, h_last = _rg_lru(u, h0, w_r, b_r, w_i, b_i, lam)
    out_a = h * jax.nn.gelu(z_gate.astype(F32))
    f = lb + (1.0 - lb) * jax.nn.sigmoid(z_f.astype(F32))
    q = z_q.astype(F32).reshape(Bn, T, HG_HEADS, HG_DK)
    log_f = jnp.log(f).reshape(Bn, T, HG_HEADS, HG_DK)
    v = z_i.astype(F32).reshape(Bn, T, HG_HEADS, HG_DV)
    o, s_new = _hgrn2(q, log_f, v, s0)
    o = o * lax.rsqrt(jnp.mean(jnp.square(o), -1, keepdims=True) + RMS_EPS) * norm_g.astype(F32).reshape(HG_HEADS, HG_DV)
    out_b = o.reshape(Bn, T, HG_WIDTH) * jax.nn.silu(z_g.astype(F32))
    y = jnp.concatenate([out_a, out_b], axis=-1).astype(x.dtype) @ w_out
    return y, h_last, conv_new, s_new


def _s5_scan(u, s0_re, s0_im, a_re, a_im, b_re, b_im, c_re, c_im, d, log_dt):
    Bn, T, _ = u.shape
    uf = u.astype(F32)
    A = a_re.astype(F32) + 1j * a_im.astype(F32)
    dt = jnp.exp(log_dt.astype(F32))[:, None]
    A_bar = jnp.exp(dt * A)
    B_bar = ((A_bar - 1.0) / A)[..., None] * (b_re.astype(F32) + 1j * b_im.astype(F32))
    Cm = c_re.astype(F32) + 1j * c_im.astype(F32)
    L = math.gcd(T, CHUNK)
    n = T // L
    u_blk = jnp.moveaxis(uf.reshape(Bn, n, L, S5_GROUPS, S5_GROUP), 1, 0)
    def step(s, ub):
        bu = jnp.einsum('gpc,blgc->blgp', B_bar, ub.astype(jnp.complex64))
        states = _linear_scan(jnp.broadcast_to(A_bar, bu.shape), bu, s)
        y = jnp.einsum('gcp,blgp->blgc', Cm, states).real
        return states[:, -1], y
    s0 = s0_re.astype(F32) + 1j * s0_im.astype(F32)
    s_last, y = lax.scan(step, s0, u_blk)
    y = jnp.moveaxis(y, 0, 1).reshape(Bn, T, S5_WIDTH) + d.astype(F32) * uf
    return y, s_last.real, s_last.imag


def _s5_mixer(x, s0_re, s0_im, w_in, w_out, a_re, a_im, b_re, b_im, c_re, c_im, d, log_dt):
    u = x @ w_in
    y, s_re, s_im = _s5_scan(u, s0_re, s0_im, a_re, a_im, b_re, b_im, c_re, c_im, d, log_dt)
    z = jax.nn.gelu(y).astype(x.dtype) @ w_out
    val, gate = jnp.split(z, 2, axis=-1)
    return val * jax.nn.sigmoid(gate), s_re, s_im


def _moe(x, router_w, router_b, w_gate, w_up, w_down):
    Bn, T, D = x.shape
    xt = x.reshape(-1, D)
    logits = xt.astype(F32) @ router_w.astype(F32) + router_b.astype(F32)
    probs = jax.nn.softmax(logits, axis=-1)
    pg = probs.reshape(-1, N_EXPERT_GROUPS, EXPERTS_PER_GROUP)
    group_score = lax.top_k(pg, TOP_K)[0].sum(-1)
    g_sel = jnp.argmax(group_score, axis=-1)
    p_in = jnp.take_along_axis(pg, g_sel[:, None, None], axis=1)[:, 0]
    top_p, top_i = lax.top_k(p_in, TOP_K)
    gates = top_p / jnp.sum(top_p, -1, keepdims=True)
    expert_ids = g_sel[:, None] * EXPERTS_PER_GROUP + top_i
    combine = jnp.einsum('nk,nke->ne', gates, jax.nn.one_hot(expert_ids, N_EXPERTS, dtype=F32))
    y = jnp.zeros(xt.shape, F32)
    for e in range(N_EXPERTS):
        h = jax.nn.silu(xt @ w_gate[e]) * (xt @ w_up[e])
        y = y + combine[:, e:e + 1] * (h @ w_down[e]).astype(F32)
    return y.reshape(Bn, T, D).astype(x.dtype)


def setup_inputs(seed: int = 0) -> dict:
    key = jax.random.key(seed)
    ks = iter(jax.random.split(key, 48))
    nrm = lambda shape, scale: scale * jax.random.normal(next(ks), shape, F32)
    u_lam = jax.random.uniform(next(ks), (N_AB, LRU_WIDTH), F32, minval=0.9, maxval=0.999)
    p_lam = u_lam ** (1.0 / RG_C)
    log_dt = jax.random.uniform(next(ks), (N_S5, S5_GROUPS), F32, minval=math.log(0.001), maxval=math.log(0.1))
    a_im = jnp.pi * jnp.broadcast_to(jnp.arange(S5_STATE, dtype=F32), (N_S5, S5_GROUPS, S5_STATE)) + nrm((N_S5, S5_GROUPS, S5_STATE), 0.01)
    return {
        'x_prompt': nrm((BATCH, SEQ, D_MODEL), 1.0),
        'x_sample': nrm((DEC_BATCH, DEC_SEQ, D_MODEL), 1.0),
        'state_lru_h': nrm((N_AB, DEC_BATCH, LRU_WIDTH), 0.5),
        'state_lru_conv': nrm((N_AB, DEC_BATCH, CONV_WIDTH - 1, LRU_WIDTH), 1.0),
        'state_hgrn2': nrm((N_AB, DEC_BATCH, HG_HEADS, HG_DK, HG_DV), 0.3),
        'state_s5_re': nrm((N_S5, DEC_BATCH, S5_GROUPS, S5_STATE), 0.3),
        'state_s5_im': nrm((N_S5, DEC_BATCH, S5_GROUPS, S5_STATE), 0.3),
        'ln_mix_g': 1.0 + nrm((DEPTH, D_MODEL), 0.02),
        'ln_mix_b': nrm((DEPTH, D_MODEL), 0.02),
        'ln_ffn_g': 1.0 + nrm((DEPTH, D_MODEL), 0.02),
        'ln_ffn_b': nrm((DEPTH, D_MODEL), 0.02),
        'w_in_ab': nrm((N_AB, D_MODEL, D_IN_AB), D_MODEL ** -0.5),
        'w_out_ab': nrm((N_AB, LRU_WIDTH + HG_WIDTH, D_MODEL), BETA * (LRU_WIDTH + HG_WIDTH) ** -0.5),
        'lru_conv_w': nrm((N_AB, CONV_WIDTH, LRU_WIDTH), CONV_WIDTH ** -0.5),
        'lru_conv_b': nrm((N_AB, LRU_WIDTH), 0.02),
        'lru_w_r': nrm((N_AB, LRU_HEADS, LRU_HEAD_DIM, LRU_HEAD_DIM), LRU_HEAD_DIM ** -0.5),
        'lru_b_r': nrm((N_AB, LRU_WIDTH), 0.02),
        'lru_w_i': nrm((N_AB, LRU_HEADS, LRU_HEAD_DIM, LRU_HEAD_DIM), LRU_HEAD_DIM ** -0.5),
        'lru_b_i': nrm((N_AB, LRU_WIDTH), 0.02),
        'lru_lam': jnp.log(p_lam) - jnp.log1p(-p_lam),
        'hgrn_lb_logits': nrm((N_AB + 1, HG_WIDTH), 0.5),
        'hgrn_norm_g': 1.0 + nrm((N_AB, HG_WIDTH), 0.02),
        'w_in_s5': nrm((N_S5, D_MODEL, S5_WIDTH), D_MODEL ** -0.5),
        'w_out_s5': nrm((N_S5, S5_WIDTH, 2 * D_MODEL), BETA * S5_WIDTH ** -0.5),
        's5_a_re': -0.5 + nrm((N_S5, S5_GROUPS, S5_STATE), 0.01),
        's5_a_im': a_im,
        's5_b_re': nrm((N_S5, S5_GROUPS, S5_STATE, S5_GROUP), (2 * S5_GROUP) ** -0.5),
        's5_b_im': nrm((N_S5, S5_GROUPS, S5_STATE, S5_GROUP), (2 * S5_GROUP) ** -0.5),
        's5_c_re': nrm((N_S5, S5_GROUPS, S5_GROUP, S5_STATE), S5_STATE ** -0.5),
        's5_c_im': nrm((N_S5, S5_GROUPS, S5_GROUP, S5_STATE), S5_STATE ** -0.5),
        's5_d': nrm((N_S5, S5_WIDTH), 0.5),
        's5_log_dt': log_dt,
        'router_w': nrm((D_MODEL, N_EXPERTS), D_MODEL ** -0.5),
        'router_b': nrm((N_EXPERTS,), 0.01),
        'moe_w_gate': nrm((DEPTH, N_EXPERTS, D_MODEL, D_EXPERT), D_MODEL ** -0.5),
        'moe_w_up': nrm((DEPTH, N_EXPERTS, D_MODEL, D_EXPERT), D_MODEL ** -0.5),
        'moe_w_down': nrm((DEPTH, N_EXPERTS, D_EXPERT, D_MODEL), BETA * D_EXPERT ** -0.5),
    }


def reference(x_prompt, x_sample, state_lru_h, state_lru_conv, state_hgrn2, state_s5_re, state_s5_im,
              ln_mix_g, ln_mix_b, ln_ffn_g, ln_ffn_b, w_in_ab, w_out_ab, lru_conv_w, lru_conv_b,
              lru_w_r, lru_b_r, lru_w_i, lru_b_i, lru_lam, hgrn_lb_logits, hgrn_norm_g,
              w_in_s5, w_out_s5, s5_a_re, s5_a_im, s5_b_re, s5_b_im, s5_c_re, s5_c_im, s5_d, s5_log_dt,
              router_w, router_b, moe_w_gate, moe_w_up, moe_w_down):
    bp = x_prompt.shape[0]
    lb_all = jnp.cumsum(jax.nn.softmax(hgrn_lb_logits.astype(F32), axis=0), axis=0)
    yp, ys = x_prompt, x_sample
    p_h, p_conv, p_hg, p_re, p_im = [], [], [], [], []
    s_h, s_conv, s_hg, s_re, s_im = [], [], [], [], []
    for l in range(DEPTH):
        j = l // 2
        if l % 2 == 0:
            w = (w_in_ab[j], w_out_ab[j], lru_conv_w[j], lru_conv_b[j], lru_w_r[j], lru_b_r[j],
                 lru_w_i[j], lru_b_i[j], lru_lam[j], lb_all[j], hgrn_norm_g[j])
            mp, h1, c1, g1 = _ab_mixer(yp, jnp.zeros((bp, LRU_WIDTH), F32),
                                       jnp.zeros((bp, CONV_WIDTH - 1, LRU_WIDTH), F32),
                                       jnp.zeros((bp, HG_HEADS, HG_DK, HG_DV), F32), *w)
            ms, h2, c2, g2 = _ab_mixer(ys, state_lru_h[j], state_lru_conv[j], state_hgrn2[j], *w)
            p_h.append(h1); p_conv.append(c1); p_hg.append(g1)
            s_h.append(h2); s_conv.append(c2); s_hg.append(g2)
        else:
            w = (w_in_s5[j], w_out_s5[j], s5_a_re[j], s5_a_im[j], s5_b_re[j], s5_b_im[j],
                 s5_c_re[j], s5_c_im[j], s5_d[j], s5_log_dt[j])
            zp = jnp.zeros((bp, S5_GROUPS, S5_STATE), F32)
            mp, r1, i1 = _s5_mixer(yp, zp, zp, *w)
            ms, r2, i2 = _s5_mixer(ys, state_s5_re[j], state_s5_im[j], *w)
            p_re.append(r1); p_im.append(i1)
            s_re.append(r2); s_im.append(i2)
        yp = _post_norm(yp, mp, ln_mix_g[l], ln_mix_b[l])
        ys = _post_norm(ys, ms, ln_mix_g[l], ln_mix_b[l])
        yp = _post_norm(yp, _moe(yp, router_w, router_b, moe_w_gate[l], moe_w_up[l], moe_w_down[l]), ln_ffn_g[l], ln_ffn_b[l])
        ys = _post_norm(ys, _moe(ys, router_w, router_b, moe_w_gate[l], moe_w_up[l], moe_w_down[l]), ln_ffn_g[l], ln_ffn_b[l])
    return (yp, ys,
            jnp.stack(p_h).astype(state_lru_h.dtype), jnp.stack(p_conv).astype(state_lru_conv.dtype),
            jnp.stack(p_hg).astype(state_hgrn2.dtype), jnp.stack(p_re).astype(state_s5_re.dtype),
            jnp.stack(p_im).astype(state_s5_im.dtype),
            jnp.stack(s_h).astype(state_lru_h.dtype), jnp.stack(s_conv).astype(state_lru_conv.dtype),
            jnp.stack(s_hg).astype(state_hgrn2.dtype), jnp.stack(s_re).astype(state_s5_re.dtype),
            jnp.stack(s_im).astype(state_s5_im.dtype))
```

```python
import functools
import math

import jax
import jax.numpy as jnp
from jax import lax
from jax.experimental import pallas as pl
from jax.experimental.pallas import tpu as pltpu

F32 = jnp.float32
BF16 = jnp.bfloat16
HIGHEST = lax.Precision.HIGHEST

DEPTH = 2
RG_C = 8.0
LRU_HEADS = 8
HG_HEADS = 4
HG_CHUNK = 16
RMS_EPS = 1e-6
S5_GROUP = 16
S5_CHUNK = 16
N_EXPERT_GROUPS = 4
ALPHA = (2.0 * DEPTH) ** 0.25
LN_EPS = 1e-5
VMEM_LIMIT_BYTES = 56 * 1024 * 1024


def _dot(a, b):
    return jnp.dot(a, b, preferred_element_type=F32)


def _dot_nt(a, b):
    return lax.dot_general(a, b, (((1,), (1,)), ((), ())), preferred_element_type=F32)


def _sigmoid(x):
    return 1.0 / (1.0 + jnp.exp(-x))


def _gelu_tanh(x):
    c = math.sqrt(2.0 / math.pi)
    return 0.5 * x * (1.0 + jnp.tanh(c * (x + 0.044715 * (x * x * x))))


def _layer_norm(s, g, b):
    mu = jnp.mean(s, axis=-1, keepdims=True)
    d = s - mu
    var = jnp.mean(d * d, axis=-1, keepdims=True)
    return d * lax.rsqrt(var + LN_EPS) * g + b


def _shift_rows(x, s, row):
    del row
    return pltpu.roll(x, s, axis=0)


def _ab_kernel(x_ref, h0_ref, conv0_ref, s0_ref, w_in_ref, w_out_ref, conv_w_ref, conv_b_ref,
               wr_ref, br_ref, wi_ref, bi_ref, lam_ref, lb_logits_ref, norm_g_ref, ln_g_ref, ln_b_ref,
               y_ref, h_out_ref, conv_out_ref, s_out_ref,
               zx_buf, h_carry, st_carry, st_all, *, tb, width):
    t = pl.program_id(1)
    nt = pl.num_programs(1)
    W = width
    L = HG_CHUNK
    nc = tb // L
    dk = W // HG_HEADS

    @pl.when(t == 0)
    def _init():
        zx_buf[0:8, :] = jnp.zeros((8, W), F32)
        zx_buf[5:8, :] = conv0_ref[...]
        h_carry[...] = jnp.broadcast_to(h0_ref[...], h_carry.shape)
        for hd in range(HG_HEADS):
            st_carry[hd] = s0_ref[hd].T

    x = x_ref[...]
    z = _dot(x.astype(BF16), w_in_ref[...])
    z_gate = z[:, 0 * W:1 * W]
    z_x = z[:, 1 * W:2 * W]
    z_q = z[:, 2 * W:3 * W]
    z_f = z[:, 3 * W:4 * W]
    z_i = z[:, 4 * W:5 * W]
    z_g = z[:, 5 * W:6 * W]

    zx_buf[8:8 + tb, :] = z_x
    cw = conv_w_ref[...]
    u = (conv_b_ref[...] + cw[3:4, :] * z_x
         + cw[2:3, :] * zx_buf[7:7 + tb, :]
         + cw[1:2, :] * zx_buf[6:6 + tb, :]
         + cw[0:1, :] * zx_buf[5:5 + tb, :])
    tail = zx_buf[tb:tb + 8, :]
    zx_buf[0:8, :] = tail

    ub = u.astype(BF16)
    r = _sigmoid(_dot(ub, wr_ref[...]) + br_ref[...])
    ig = _sigmoid(_dot(ub, wi_ref[...]) + bi_ref[...])
    nlam = -lam_ref[...]
    softplus_nlam = jnp.maximum(nlam, 0.0) + jnp.log1p(jnp.exp(-jnp.abs(nlam)))
    log_a = (-RG_C) * r * softplus_nlam
    a = jnp.exp(log_a)
    th = jnp.tanh(log_a)
    b = jnp.sqrt(-2.0 * th / (1.0 - th)) * ig * u

    row = lax.broadcasted_iota(jnp.int32, (tb, W), 0)
    s = 1
    while s < tb:
        m = row >= s
        a_sh = _shift_rows(a, s, row)
        b_sh = _shift_rows(b, s, row)
        b = jnp.where(m, a * b_sh + b, b)
        a = jnp.where(m, a * a_sh, a)
        s *= 2
    h = b + a * h_carry[0:1, :]
    h_carry[...] = jnp.broadcast_to(h[tb - 1:tb, :], h_carry.shape)
    out_a = h * _gelu_tanh(z_gate)

    lg = lb_logits_ref[...]
    e = jnp.exp(lg - jnp.max(lg, axis=0, keepdims=True))
    lb = e[0:1, :] / jnp.sum(e, axis=0, keepdims=True)
    f = lb + (1.0 - lb) * _sigmoid(z_f)
    log_f = jnp.log(f)
    kk = 1.0 - f
    rin = row % L
    G = log_f
    s = 1
    while s < L:
        G = G + jnp.where(rin >= s, _shift_rows(G, s, row), 0.0)
        s *= 2
    G3 = G.reshape(nc, L, W)
    G_last = G3[:, L - 1:L, :]
    q_dec = (z_q * jnp.exp(G)).reshape(nc, L, W).astype(BF16)
    k_inc = (kk * jnp.exp(-G)).reshape(nc, L, W).astype(BF16)
    k_tail = (kk.reshape(nc, L, W) * jnp.exp(G_last - G3)).astype(BF16)
    decay = jnp.exp(G_last)
    v3 = z_i.reshape(nc, L, W).astype(BF16)
    causal = (lax.broadcasted_iota(jnp.int32, (nc, L, L), 1)
              >= lax.broadcasted_iota(jnp.int32, (nc, L, L), 2))

    o_heads = []
    for hd in range(HG_HEADS):
        sl = slice(hd * dk, (hd + 1) * dk)
        qd, ki, kt, vv = q_dec[:, :, sl], k_inc[:, :, sl], k_tail[:, :, sl], v3[:, :, sl]
        scores = jnp.einsum('ntk,nsk->nts', qd, ki, preferred_element_type=F32)
        scores = jnp.where(causal, scores, 0.0).astype(BF16)
        o_intra = jnp.einsum('nts,nsv->ntv', scores, vv, preferred_element_type=F32)
        vt = jnp.swapaxes(vv.astype(F32), 1, 2).astype(BF16)
        upd = jnp.einsum('nvs,nsk->nvk', vt, kt, preferred_element_type=F32)
        st = st_carry[hd]
        for n in range(nc):
            st_all[hd, n] = st.astype(BF16)
            st = st * decay[n, :, sl] + upd[n]
        st_carry[hd] = st
        o_inter = jnp.einsum('ntk,nvk->ntv', qd, st_all[hd], preferred_element_type=F32)
        o = (o_intra + o_inter).reshape(tb, dk)
        o = o * lax.rsqrt(jnp.mean(o * o, axis=-1, keepdims=True) + RMS_EPS)
        o_heads.append(o)
    o_all = jnp.concatenate(o_heads, axis=-1) * norm_g_ref[...]
    out_b = o_all * (z_g * _sigmoid(z_g))

    mixed = jnp.concatenate([out_a, out_b], axis=-1).astype(BF16)
    y = _dot(mixed, w_out_ref[...])
    y_ref[...] = _layer_norm(ALPHA * x + y, ln_g_ref[...], ln_b_ref[...])

    @pl.when(t == nt - 1)
    def _fin():
        h_out_ref[...] = h[tb - 1:tb, :]
        conv_out_ref[...] = tail[5:8, :]
        for hd in range(HG_HEADS):
            s_out_ref[hd] = st_carry[hd].T


def _ab_mixer(x, h0, conv0, s0, wts, tb):
    B, T, D = x.shape
    W = h0.shape[-1]
    dk = W // HG_HEADS
    nt = T // tb
    nc = tb // HG_CHUNK
    (w_in, w_out, conv_w, conv_b, wr, br, wi, bi, lam, lb_logits, norm_g, ln_g, ln_b) = wts
    full = lambda a: pl.BlockSpec(a.shape, lambda b, t: (0,) * a.ndim)
    per_b = lambda shp: pl.BlockSpec((None,) + shp, lambda b, t: (b,) + (0,) * len(shp))
    out_shape = (jax.ShapeDtypeStruct((B, T, D), F32),
                 jax.ShapeDtypeStruct((B, 1, W), F32),
                 jax.ShapeDtypeStruct((B, 3, W), F32),
                 jax.ShapeDtypeStruct((B, HG_HEADS, dk, dk), F32))
    return pl.pallas_call(
        functools.partial(_ab_kernel, tb=tb, width=W),
        grid=(B, nt),
        in_specs=[pl.BlockSpec((None, tb, D), lambda b, t: (b, t, 0)),
                  per_b((1, W)), per_b((3, W)), per_b((HG_HEADS, dk, dk))]
                 + [full(a) for a in wts],
        out_specs=(pl.BlockSpec((None, tb, D), lambda b, t: (b, t, 0)),
                   per_b((1, W)), per_b((3, W)), per_b((HG_HEADS, dk, dk))),
        out_shape=out_shape,
        scratch_shapes=[pltpu.VMEM((tb + 8, W), F32),
                        pltpu.VMEM((8, W), F32),
                        pltpu.VMEM((HG_HEADS, dk, dk), F32),
                        pltpu.VMEM((HG_HEADS, nc, dk, dk), BF16)],
        compiler_params=pltpu.CompilerParams(
            dimension_semantics=("arbitrary", "arbitrary"),
            vmem_limit_bytes=VMEM_LIMIT_BYTES),
    )(x, h0, conv0, s0, *wts)


def _router(x, rw, rb, n_experts):
    per_group = n_experts // N_EXPERT_GROUPS
    logits = jnp.dot(x, rw, preferred_element_type=F32, precision=HIGHEST) + rb
    logits = logits - jnp.max(logits, axis=-1, keepdims=True)
    ex = jnp.exp(logits)
    p = ex / jnp.sum(ex, axis=-1, keepdims=True)
    lane_i = lax.broadcasted_iota(jnp.int32, p.shape, 1)
    grp = lane_i // per_group
    lane = lane_i.astype(F32)
    big = float(n_experts)

    def top2(pm):
        m1 = jnp.max(pm, axis=-1, keepdims=True)
        i1 = jnp.min(jnp.where(pm == m1, lane, big), axis=-1, keepdims=True)
        pm2 = jnp.where(lane == i1, -1.0, pm)
        m2 = jnp.max(pm2, axis=-1, keepdims=True)
        i2 = jnp.min(jnp.where(pm2 == m2, lane, big), axis=-1, keepdims=True)
        return m1, i1, m2, i2

    best = None
    g_sel = None
    for j in range(N_EXPERT_GROUPS):
        m1, _, m2, _ = top2(jnp.where(grp == j, p, -1.0))
        score = m1 + m2
        if best is None:
            best, g_sel = score, jnp.zeros_like(lane_i[:, :1])
        else:
            better = score > best
            best = jnp.where(better, score, best)
            g_sel = jnp.where(better, j, g_sel)
    m1, i1, m2, i2 = top2(jnp.where(grp == g_sel, p, -1.0))
    den = m1 + m2
    return jnp.where(lane == i1, m1 / den, 0.0) + jnp.where(lane == i2, m2 / den, 0.0)


def _moe_dense_kernel(x_ref, rw_ref, rb_ref, wg_ref, wu_ref, wd_ref, ln_g_ref, ln_b_ref,
                      y_ref, xb_scr, comb_scr, acc_scr, *, n_experts):
    e = pl.program_id(1)

    @pl.when(e == 0)
    def _init():
        x = x_ref[...]
        xb_scr[...] = x.astype(BF16)
        comb_scr[...] = _router(x, rw_ref[...], rb_ref[...], n_experts)
        acc_scr[...] = jnp.zeros_like(acc_scr)

    comb = comb_scr[...]
    lane = lax.broadcasted_iota(jnp.int32, comb.shape, 1)
    c_e = jnp.sum(jnp.where(lane == e, comb, 0.0), axis=-1, keepdims=True)
    xb = xb_scr[...]
    g = _dot(xb, wg_ref[...])
    up = _dot(xb, wu_ref[...])
    hmid = (g * _sigmoid(g) * up).astype(BF16)
    acc_scr[...] += c_e * _dot(hmid, wd_ref[...])

    @pl.when(e == n_experts - 1)
    def _fin():
        y_ref[...] = _layer_norm(ALPHA * x_ref[...] + acc_scr[...], ln_g_ref[...], ln_b_ref[...])


def _moe_dense(x2d, rw, rb, wg, wu, wd, ln_g, ln_b, tm):
    N, D = x2d.shape
    E, _, De = wg.shape
    full = lambda a: pl.BlockSpec(a.shape, lambda i, e: (0,) * a.ndim)
    return pl.pallas_call(
        functools.partial(_moe_dense_kernel, n_experts=E),
        grid=(N // tm, E),
        in_specs=[pl.BlockSpec((tm, D), lambda i, e: (i, 0)), full(rw), full(rb),
                  pl.BlockSpec((None, D, De), lambda i, e: (e, 0, 0)),
                  pl.BlockSpec((None, D, De), lambda i, e: (e, 0, 0)),
                  pl.BlockSpec((None, De, D), lambda i, e: (e, 0, 0)),
                  full(ln_g), full(ln_b)],
        out_specs=pl.BlockSpec((tm, D), lambda i, e: (i, 0)),
        out_shape=jax.ShapeDtypeStruct((N, D), F32),
        scratch_shapes=[pltpu.VMEM((tm, D), BF16), pltpu.VMEM((tm, E), F32), pltpu.VMEM((tm, D), F32)],
        compiler_params=pltpu.CompilerParams(
            dimension_semantics=("arbitrary", "arbitrary"),
            vmem_limit_bytes=VMEM_LIMIT_BYTES),
    )(x2d, rw, rb, wg, wu, wd, ln_g, ln_b)


def _s5_weights_kernel(lam_col_ref, lam_row_ref, bt_ref, ct_ref, d_ref,
                       toe_ref, pst_ref, qst_ref, a_pow_ref):
    L, Cg = S5_CHUNK, S5_GROUP
    LC = L * Cg
    P = lam_row_ref.shape[-1]

    def discretise(a_re, a_im, log_dt):
        dt = jnp.exp(log_dt)
        lr, li = dt * a_re, dt * a_im
        er = jnp.exp(lr)
        abr, abi = er * jnp.cos(li), er * jnp.sin(li)
        nr, ni = abr - 1.0, abi
        den = a_re * a_re + a_im * a_im
        cr = (nr * a_re + ni * a_im) / den
        ci = (ni * a_re - nr * a_im) / den
        return lr, li, cr, ci

    def a_power(lr, li, tau):
        er = jnp.exp(tau * lr)
        return er * jnp.cos(tau * li), er * jnp.sin(tau * li)

    lc = lam_col_ref[...]
    lr_c, li_c, _, _ = discretise(lc[:, 0:1], lc[:, 1:2], lc[:, 2:3])
    lw = lam_row_ref[...]
    lr_r, li_r, cr_r, ci_r = discretise(lw[0:1, :], lw[1:2, :], lw[2:3, :])

    btr, bti = bt_ref[0], bt_ref[1]
    bbr = cr_r * btr - ci_r * bti
    bbi = cr_r * bti + ci_r * btr

    ctr, cti = ct_ref[0], ct_ref[1]
    tau_l = (lax.broadcasted_iota(jnp.int32, (P, LC), 1) // Cg).astype(F32)

    pr, pi = a_power(lr_c, li_c, tau_l)
    car = ctr * pr - cti * pi
    cai = ctr * pi + cti * pr
    t0 = (jnp.dot(bbr[0:Cg, :], car, preferred_element_type=F32, precision=HIGHEST)
          - jnp.dot(bbi[0:Cg, :], cai, preferred_element_type=F32, precision=HIGHEST))
    r16 = lax.broadcasted_iota(jnp.int32, (Cg, LC), 0)
    l16 = lax.broadcasted_iota(jnp.int32, (Cg, LC), 1)
    t0 = t0 + jnp.where(r16 == l16, d_ref[...], 0.0)
    toe_ref[0:Cg, :] = t0
    for s in range(1, L):
        toe_ref[s * Cg:(s + 1) * Cg, :] = jnp.where(l16 >= s * Cg, pltpu.roll(t0, s * Cg, axis=1), 0.0)

    pr, pi = a_power(lr_c, li_c, tau_l + 1.0)
    qst_ref[0:P, :] = ctr * pr - cti * pi
    qst_ref[P:2 * P, :] = -(ctr * pi + cti * pr)

    tau_r = (L - 1 - lax.broadcasted_iota(jnp.int32, (LC, P), 0) // Cg).astype(F32)
    pr, pi = a_power(lr_r, li_r, tau_r)
    pst_ref[:, 0:P] = bbr * pr - bbi * pi
    pst_ref[:, P:2 * P] = bbr * pi + bbi * pr

    pr, pi = a_power(lr_r, li_r, float(L))
    a_pow_ref[0:1, :] = pr
    a_pow_ref[1:2, :] = pi


def _s5_weights(a_re, a_im, log_dt, b_re, b_im, c_re, c_im, d):
    G, P = a_re.shape
    Cg, L = S5_GROUP, S5_CHUNK
    LC = L * Cg
    lam = jnp.stack([a_re, a_im, jnp.broadcast_to(log_dt[:, None], (G, P)), jnp.zeros((G, P), F32)], axis=1)
    lam_row = lam
    lam_col = jnp.swapaxes(lam, 1, 2)
    bt = jnp.stack([jnp.swapaxes(b_re, 1, 2), jnp.swapaxes(b_im, 1, 2)], axis=1)
    bt = jnp.tile(bt, (1, 1, L, 1))
    ct = jnp.stack([jnp.swapaxes(c_re, 1, 2), jnp.swapaxes(c_im, 1, 2)], axis=1)
    ct = jnp.tile(ct, (1, 1, 1, L))
    d_col = d.reshape(G, Cg, 1)
    blk = lambda shp: pl.BlockSpec((None,) + shp, lambda g: (g,) + (0,) * len(shp))
    return pl.pallas_call(
        _s5_weights_kernel,
        grid=(G,),
        in_specs=[blk((P, 4)), blk((4, P)), blk((2, LC, P)), blk((2, P, LC)), blk((Cg, 1))],
        out_specs=(blk((LC, LC)), blk((LC, 2 * P)), blk((2 * P, LC)), blk((2, P))),
        out_shape=(jax.ShapeDtypeStruct((G, LC, LC), F32),
                   jax.ShapeDtypeStruct((G, LC, 2 * P), F32),
                   jax.ShapeDtypeStruct((G, 2 * P, LC), F32),
                   jax.ShapeDtypeStruct((G, 2, P), F32)),
        compiler_params=pltpu.CompilerParams(dimension_semantics=("arbitrary",)),
    )(lam_col, lam_row, bt, ct, d_col)


def _proj_kernel(x_ref, w_ref, o_ref):
    o_ref[...] = _dot(x_ref[...].astype(BF16), w_ref[...]).astype(o_ref.dtype)


def _proj(x2d, w, tm, out_dtype):
    N, D = x2d.shape
    Do = w.shape[1]
    return pl.pallas_call(
        _proj_kernel,
        grid=(N // tm,),
        in_specs=[pl.BlockSpec((tm, D), lambda i: (i, 0)), pl.BlockSpec(w.shape, lambda i: (0, 0))],
        out_specs=pl.BlockSpec((tm, Do), lambda i: (i, 0)),
        out_shape=jax.ShapeDtypeStruct((N, Do), out_dtype),
        compiler_params=pltpu.CompilerParams(dimension_semantics=("arbitrary",),
                                             vmem_limit_bytes=VMEM_LIMIT_BYTES),
    )(x2d, w)


def _s5_scan_kernel(x_ref, s0_ref, toe_ref, pst_ref, qst_ref, apow_ref, y_ref, sfin_ref, z_scr, sin_scr,
                    *, n_chunks, batch):
    P = apow_ref.shape[-1]
    xg = x_ref[...]
    z_scr[...] = _dot(xg, pst_ref[...].astype(BF16))
    ar = apow_ref[0:1, :]
    ai = apow_ref[1:2, :]

    def step(n, carry):
        sr, si = carry
        rows = pl.ds(pl.multiple_of(n * batch, batch), batch)
        sin_scr[rows, 0:P] = sr
        sin_scr[rows, P:2 * P] = si
        zc = z_scr[rows, :]
        return (ar * sr - ai * si + zc[:, 0:P], ar * si + ai * sr + zc[:, P:2 * P])

    sr, si = lax.fori_loop(0, n_chunks, step, (s0_ref[0], s0_ref[1]))
    sfin_ref[0] = sr
    sfin_ref[1] = si
    y = _dot(xg, toe_ref[...].astype(BF16)) + _dot(sin_scr[...].astype(BF16), qst_ref[...].astype(BF16))
    y_ref[...] = y.astype(y_ref.dtype)


def _s5_scan(xg, s0, toe, pst, qst, apow, n_chunks, batch):
    G, R, LC = xg.shape
    P = apow.shape[-1]
    blk = lambda shp: pl.BlockSpec((None,) + shp, lambda g: (g,) + (0,) * len(shp))
    return pl.pallas_call(
        functools.partial(_s5_scan_kernel, n_chunks=n_chunks, batch=batch),
        grid=(G,),
        in_specs=[blk((R, LC)), blk((2, batch, P)), blk((LC, LC)), blk((LC, 2 * P)), blk((2 * P, LC)),
                  blk((2, P))],
        out_specs=(blk((R, LC)), blk((2, batch, P))),
        out_shape=(jax.ShapeDtypeStruct((G, R, LC), BF16), jax.ShapeDtypeStruct((G, 2, batch, P), F32)),
        scratch_shapes=[pltpu.VMEM((R, 2 * P), F32), pltpu.VMEM((R, 2 * P), F32)],
        compiler_params=pltpu.CompilerParams(dimension_semantics=("arbitrary",),
                                             vmem_limit_bytes=VMEM_LIMIT_BYTES),
    )(xg, s0, toe, pst, qst, apow)


def _s5_out_kernel(x_ref, y_ref, w_ref, ln_g_ref, ln_b_ref, o_ref, *, d_model):
    act = _gelu_tanh(y_ref[...].astype(F32)).astype(BF16)
    zz = _dot(act, w_ref[...])
    mix = zz[:, :d_model] * _sigmoid(zz[:, d_model:])
    o_ref[...] = _layer_norm(ALPHA * x_ref[...] + mix, ln_g_ref[...], ln_b_ref[...])


def _s5_out(x2d, y2d, w_out, ln_g, ln_b, tm):
    N, D = x2d.shape
    full = lambda a: pl.BlockSpec(a.shape, lambda i: (0,) * a.ndim)
    return pl.pallas_call(
        functools.partial(_s5_out_kernel, d_model=D),
        grid=(N // tm,),
        in_specs=[pl.BlockSpec((tm, D), lambda i: (i, 0)), pl.BlockSpec((tm, D), lambda i: (i, 0)),
                  full(w_out), full(ln_g), full(ln_b)],
        out_specs=pl.BlockSpec((tm, D), lambda i: (i, 0)),
        out_shape=jax.ShapeDtypeStruct((N, D), F32),
        compiler_params=pltpu.CompilerParams(dimension_semantics=("arbitrary",),
                                             vmem_limit_bytes=VMEM_LIMIT_BYTES),
    )(x2d, y2d, w_out, ln_g, ln_b)


def _s5_mixer(x, s0_re, s0_im, w_in, w_out, s5w, ln_g, ln_b, tm):
    B, T, D = x.shape
    toe, pst, qst, apow = s5w
    G = toe.shape[0]
    L, Cg = S5_CHUNK, S5_GROUP
    n = T // L
    x2d = x.reshape(B * T, D)
    u = _proj(x2d, w_in, tm, BF16)
    xg = u.reshape(B, n, L, G, Cg).transpose(3, 1, 0, 2, 4).reshape(G, n * B, L * Cg)
    s0 = jnp.stack([s0_re, s0_im], axis=0).transpose(2, 0, 1, 3)
    yg, sfin = _s5_scan(xg, s0, toe, pst, qst, apow, n, B)
    y2d = yg.reshape(G, n, B, L, Cg).transpose(2, 1, 3, 0, 4).reshape(B * T, D)
    out = _s5_out(x2d, y2d, w_out, ln_g, ln_b, tm).reshape(B, T, D)
    sfin = sfin.transpose(1, 2, 0, 3)
    return out, sfin[0], sfin[1]


def _block_diag(w):
    H, d, _ = w.shape
    eye = jnp.eye(H, dtype=w.dtype)
    return (eye[:, None, :, None] * w[:, :, None, :]).reshape(H * d, H * d)


def kernel(x_prompt, x_sample, state_lru_h, state_lru_conv, state_hgrn2, state_s5_re, state_s5_im, ln_mix_g, ln_mix_b, ln_ffn_g, ln_ffn_b, w_in_ab, w_out_ab, lru_conv_w, lru_conv_b, lru_w_r, lru_b_r, lru_w_i, lru_b_i, lru_lam, hgrn_lb_logits, hgrn_norm_g, w_in_s5, w_out_s5, s5_a_re, s5_a_im, s5_b_re, s5_b_im, s5_c_re, s5_c_im, s5_d, s5_log_dt, router_w, router_b, moe_w_gate, moe_w_up, moe_w_down):
    Bp, Tp, D = x_prompt.shape
    Bs, Ts, _ = x_sample.shape
    W = state_lru_h.shape[-1]
    dk = W // HG_HEADS
    row = lambda v: v.reshape(1, -1)

    def moe(y, l, tm):
        shp = y.shape
        out = _moe_dense(y.reshape(-1, D), router_w, row(router_b), moe_w_gate[l].astype(BF16),
                         moe_w_up[l].astype(BF16), moe_w_down[l].astype(BF16),
                         row(ln_ffn_g[l]), row(ln_ffn_b[l]), tm)
        return out.reshape(shp)

    j = 0
    wts = (w_in_ab[j].astype(BF16), w_out_ab[j].astype(BF16), lru_conv_w[j], row(lru_conv_b[j]),
           _block_diag(lru_w_r[j]).astype(BF16), row(lru_b_r[j]),
           _block_diag(lru_w_i[j]).astype(BF16), row(lru_b_i[j]),
           row(lru_lam[j]), hgrn_lb_logits, row(hgrn_norm_g[j]), row(ln_mix_g[0]), row(ln_mix_b[0]))
    yp, p_h, p_conv, p_hg = _ab_mixer(
        x_prompt, jnp.zeros((Bp, 1, W), F32), jnp.zeros((Bp, 3, W), F32),
        jnp.zeros((Bp, HG_HEADS, dk, dk), F32), wts, 256)
    ys, s_h, s_conv, s_hg = _ab_mixer(
        x_sample, state_lru_h[j][:, None, :], state_lru_conv[j], state_hgrn2[j], wts, Ts)
    yp = moe(yp, 0, 512)
    ys = moe(ys, 0, 512)

    s5w = _s5_weights(s5_a_re[j], s5_a_im[j], s5_log_dt[j], s5_b_re[j], s5_b_im[j],
                      s5_c_re[j], s5_c_im[j], s5_d[j])
    G, P = s5_a_re[j].shape
    w_in5 = w_in_s5[j].astype(BF16)
    w_out5 = w_out_s5[j].astype(BF16)
    zp = jnp.zeros((Bp, G, P), F32)
    yp, p_re, p_im = _s5_mixer(yp, zp, zp, w_in5, w_out5, s5w, row(ln_mix_g[1]), row(ln_mix_b[1]), 512)
    ys, s_re, s_im = _s5_mixer(ys, state_s5_re[j], state_s5_im[j], w_in5, w_out5, s5w,
                               row(ln_mix_g[1]), row(ln_mix_b[1]), 512)
    yp = moe(yp, 1, 512)
    ys = moe(ys, 1, 512)

    return (yp, ys,
            p_h.reshape(1, Bp, W), p_conv[None], p_hg[None], p_re[None], p_im[None],
            s_h.reshape(1, Bs, W), s_conv[None], s_hg[None], s_re[None], s_im[None])
```

```python
import functools
import math

import jax
import jax.numpy as jnp
from jax import lax
from jax.experimental import pallas as pl
from jax.experimental.pallas import tpu as pltpu

F32 = jnp.float32
BF16 = jnp.bfloat16
HIGHEST = lax.Precision.HIGHEST

DEPTH = 2
RG_C = 8.0
LRU_HEADS = 8
HG_HEADS = 4
HG_CHUNK = 16
RMS_EPS = 1e-6
S5_GROUP = 16
S5_CHUNK = 16
N_EXPERT_GROUPS = 4
ALPHA = (2.0 * DEPTH) ** 0.25
LN_EPS = 1e-5
MOE_TILE = 256
MOE_CHUNK = 16
MOE_SLOTS = 768
MOE_BLOCK = 512
VMEM_LIMIT_BYTES = 56 * 1024 * 1024


def _dot(a, b):
    return jnp.dot(a, b, preferred_element_type=F32)


def _dot_nt(a, b):
    return lax.dot_general(a, b, (((1,), (1,)), ((), ())), preferred_element_type=F32)


def _sigmoid(x):
    return 1.0 / (1.0 + jnp.exp(-x))


def _gelu_tanh(x):
    c = math.sqrt(2.0 / math.pi)
    return 0.5 * x * (1.0 + jnp.tanh(c * (x + 0.044715 * (x * x * x))))


def _layer_norm(s, g, b):
    mu = jnp.mean(s, axis=-1, keepdims=True)
    d = s - mu
    var = jnp.mean(d * d, axis=-1, keepdims=True)
    return d * lax.rsqrt(var + LN_EPS) * g + b


def _shift_rows(x, s, row):
    del row
    return pltpu.roll(x, s, axis=0)


def _ab_kernel(x_ref, h0_ref, conv0_ref, s0_ref, w_in_ref, w_out_ref, conv_w_ref, conv_b_ref,
               wr_ref, br_ref, wi_ref, bi_ref, lam_ref, lb_logits_ref, norm_g_ref, ln_g_ref, ln_b_ref,
               y_ref, h_out_ref, conv_out_ref, s_out_ref,
               zx_buf, h_carry, st_carry, st_all, *, tb, width):
    t = pl.program_id(1)
    nt = pl.num_programs(1)
    W = width
    L = HG_CHUNK
    nc = tb // L
    dk = W // HG_HEADS

    @pl.when(t == 0)
    def _init():
        zx_buf[0:8, :] = jnp.zeros((8, W), F32)
        zx_buf[5:8, :] = conv0_ref[...]
        h_carry[...] = jnp.broadcast_to(h0_ref[...], h_carry.shape)
        for hd in range(HG_HEADS):
            st_carry[hd] = s0_ref[hd].T

    x = x_ref[...]
    z = _dot(x.astype(BF16), w_in_ref[...])
    z_gate = z[:, 0 * W:1 * W]
    z_x = z[:, 1 * W:2 * W]
    z_q = z[:, 2 * W:3 * W]
    z_f = z[:, 3 * W:4 * W]
    z_i = z[:, 4 * W:5 * W]
    z_g = z[:, 5 * W:6 * W]

    zx_buf[8:8 + tb, :] = z_x
    cw = conv_w_ref[...]
    u = (conv_b_ref[...] + cw[3:4, :] * z_x
         + cw[2:3, :] * zx_buf[7:7 + tb, :]
         + cw[1:2, :] * zx_buf[6:6 + tb, :]
         + cw[0:1, :] * zx_buf[5:5 + tb, :])
    tail = zx_buf[tb:tb + 8, :]
    zx_buf[0:8, :] = tail

    ub = u.astype(BF16)
    r = _sigmoid(_dot(ub, wr_ref[...]) + br_ref[...])
    ig = _sigmoid(_dot(ub, wi_ref[...]) + bi_ref[...])
    nlam = -lam_ref[...]
    softplus_nlam = jnp.maximum(nlam, 0.0) + jnp.log1p(jnp.exp(-jnp.abs(nlam)))
    log_a = (-RG_C) * r * softplus_nlam
    a = jnp.exp(log_a)
    th = jnp.tanh(log_a)
    b = jnp.sqrt(-2.0 * th / (1.0 - th)) * ig * u

    row = lax.broadcasted_iota(jnp.int32, (tb, W), 0)
    s = 1
    while s < tb:
        m = row >= s
        a_sh = _shift_rows(a, s, row)
        b_sh = _shift_rows(b, s, row)
        b = jnp.where(m, a * b_sh + b, b)
        a = jnp.where(m, a * a_sh, a)
        s *= 2
    h = b + a * h_carry[0:1, :]
    h_carry[...] = jnp.broadcast_to(h[tb - 1:tb, :], h_carry.shape)
    out_a = h * _gelu_tanh(z_gate)

    lg = lb_logits_ref[...]
    e = jnp.exp(lg - jnp.max(lg, axis=0, keepdims=True))
    lb = e[0:1, :] / jnp.sum(e, axis=0, keepdims=True)
    f = lb + (1.0 - lb) * _sigmoid(z_f)
    log_f = jnp.log(f)
    kk = 1.0 - f
    rin = row % L
    G = log_f
    s = 1
    while s < L:
        G = G + jnp.where(rin >= s, _shift_rows(G, s, row), 0.0)
        s *= 2
    G3 = G.reshape(nc, L, W)
    G_last = G3[:, L - 1:L, :]
    q_dec = (z_q * jnp.exp(G)).reshape(nc, L, W).astype(BF16)
    k_inc = (kk * jnp.exp(-G)).reshape(nc, L, W).astype(BF16)
    k_tail = (kk.reshape(nc, L, W) * jnp.exp(G_last - G3)).astype(BF16)
    decay = jnp.exp(G_last)
    v3 = z_i.reshape(nc, L, W).astype(BF16)
    causal = (lax.broadcasted_iota(jnp.int32, (nc, L, L), 1)
              >= lax.broadcasted_iota(jnp.int32, (nc, L, L), 2))

    o_heads = []
    for hd in range(HG_HEADS):
        sl = slice(hd * dk, (hd + 1) * dk)
        qd, ki, kt, vv = q_dec[:, :, sl], k_inc[:, :, sl], k_tail[:, :, sl], v3[:, :, sl]
        scores = jnp.einsum('ntk,nsk->nts', qd, ki, preferred_element_type=F32)
        scores = jnp.where(causal, scores, 0.0).astype(BF16)
        o_intra = jnp.einsum('nts,nsv->ntv', scores, vv, preferred_element_type=F32)
        vt = jnp.swapaxes(vv.astype(F32), 1, 2).astype(BF16)
        upd = jnp.einsum('nvs,nsk->nvk', vt, kt, preferred_element_type=F32)
        st = st_carry[hd]
        for n in range(nc):
            st_all[hd, n] = st.astype(BF16)
            st = st * decay[n, :, sl] + upd[n]
        st_carry[hd] = st
        o_inter = jnp.einsum('ntk,nvk->ntv', qd, st_all[hd], preferred_element_type=F32)
        o = (o_intra + o_inter).reshape(tb, dk)
        o = o * lax.rsqrt(jnp.mean(o * o, axis=-1, keepdims=True) + RMS_EPS)
        o_heads.append(o)
    o_all = jnp.concatenate(o_heads, axis=-1) * norm_g_ref[...]
    out_b = o_all * (z_g * _sigmoid(z_g))

    mixed = jnp.concatenate([out_a, out_b], axis=-1).astype(BF16)
    y = _dot(mixed, w_out_ref[...])
    y_ref[...] = _layer_norm(ALPHA * x + y, ln_g_ref[...], ln_b_ref[...])

    @pl.when(t == nt - 1)
    def _fin():
        h_out_ref[...] = h[tb - 1:tb, :]
        conv_out_ref[...] = tail[5:8, :]
        for hd in range(HG_HEADS):
            s_out_ref[hd] = st_carry[hd].T


def _ab_mixer(x, h0, conv0, s0, wts, tb):
    B, T, D = x.shape
    W = h0.shape[-1]
    dk = W // HG_HEADS
    nt = T // tb
    nc = tb // HG_CHUNK
    (w_in, w_out, conv_w, conv_b, wr, br, wi, bi, lam, lb_logits, norm_g, ln_g, ln_b) = wts
    full = lambda a: pl.BlockSpec(a.shape, lambda b, t: (0,) * a.ndim)
    per_b = lambda shp: pl.BlockSpec((None,) + shp, lambda b, t: (b,) + (0,) * len(shp))
    out_shape = (jax.ShapeDtypeStruct((B, T, D), F32),
                 jax.ShapeDtypeStruct((B, 1, W), F32),
                 jax.ShapeDtypeStruct((B, 3, W), F32),
                 jax.ShapeDtypeStruct((B, HG_HEADS, dk, dk), F32))
    return pl.pallas_call(
        functools.partial(_ab_kernel, tb=tb, width=W),
        grid=(B, nt),
        in_specs=[pl.BlockSpec((None, tb, D), lambda b, t: (b, t, 0)),
                  per_b((1, W)), per_b((3, W)), per_b((HG_HEADS, dk, dk))]
                 + [full(a) for a in wts],
        out_specs=(pl.BlockSpec((None, tb, D), lambda b, t: (b, t, 0)),
                   per_b((1, W)), per_b((3, W)), per_b((HG_HEADS, dk, dk))),
        out_shape=out_shape,
        scratch_shapes=[pltpu.VMEM((tb + 8, W), F32),
                        pltpu.VMEM((8, W), F32),
                        pltpu.VMEM((HG_HEADS, dk, dk), F32),
                        pltpu.VMEM((HG_HEADS, nc, dk, dk), BF16)],
        compiler_params=pltpu.CompilerParams(
            dimension_semantics=("arbitrary", "arbitrary"),
            vmem_limit_bytes=VMEM_LIMIT_BYTES),
    )(x, h0, conv0, s0, *wts)


def _router(x, rw, rb, n_experts):
    per_group = n_experts // N_EXPERT_GROUPS
    logits = jnp.dot(x, rw, preferred_element_type=F32, precision=HIGHEST) + rb
    logits = logits - jnp.max(logits, axis=-1, keepdims=True)
    ex = jnp.exp(logits)
    p = ex / jnp.sum(ex, axis=-1, keepdims=True)
    lane_i = lax.broadcasted_iota(jnp.int32, p.shape, 1)
    grp = lane_i // per_group
    lane = lane_i.astype(F32)
    big = float(n_experts)

    def top2(pm):
        m1 = jnp.max(pm, axis=-1, keepdims=True)
        i1 = jnp.min(jnp.where(pm == m1, lane, big), axis=-1, keepdims=True)
        pm2 = jnp.where(lane == i1, -1.0, pm)
        m2 = jnp.max(pm2, axis=-1, keepdims=True)
        i2 = jnp.min(jnp.where(pm2 == m2, lane, big), axis=-1, keepdims=True)
        return m1, i1, m2, i2

    best = None
    g_sel = None
    for j in range(N_EXPERT_GROUPS):
        m1, _, m2, _ = top2(jnp.where(grp == j, p, -1.0))
        score = m1 + m2
        if best is None:
            best, g_sel = score, jnp.zeros_like(lane_i[:, :1])
        else:
            better = score > best
            best = jnp.where(better, score, best)
            g_sel = jnp.where(better, j, g_sel)
    m1, i1, m2, i2 = top2(jnp.where(grp == g_sel, p, -1.0))
    den = m1 + m2
    return jnp.where(lane == i1, m1 / den, 0.0) + jnp.where(lane == i2, m2 / den, 0.0)


def _moe_dense_kernel(x_ref, rw_ref, rb_ref, wg_ref, wu_ref, wd_ref, ln_g_ref, ln_b_ref,
                      y_ref, xb_scr, comb_scr, acc_scr, *, n_experts):
    e = pl.program_id(1)

    @pl.when(e == 0)
    def _init():
        x = x_ref[...]
        xb_scr[...] = x.astype(BF16)
        comb_scr[...] = _router(x, rw_ref[...], rb_ref[...], n_experts)
        acc_scr[...] = jnp.zeros_like(acc_scr)

    comb = comb_scr[...]
    lane = lax.broadcasted_iota(jnp.int32, comb.shape, 1)
    c_e = jnp.sum(jnp.where(lane == e, comb, 0.0), axis=-1, keepdims=True)
    xb = xb_scr[...]
    g = _dot(xb, wg_ref[...])
    up = _dot(xb, wu_ref[...])
    hmid = (g * _sigmoid(g) * up).astype(BF16)
    acc_scr[...] += c_e * _dot(hmid, wd_ref[...])

    @pl.when(e == n_experts - 1)
    def _fin():
        y_ref[...] = _layer_norm(ALPHA * x_ref[...] + acc_scr[...], ln_g_ref[...], ln_b_ref[...])


def _moe_dense(x2d, rw, rb, wg, wu, wd, ln_g, ln_b, tm):
    N, D = x2d.shape
    E, _, De = wg.shape
    full = lambda a: pl.BlockSpec(a.shape, lambda i, e: (0,) * a.ndim)
    return pl.pallas_call(
        functools.partial(_moe_dense_kernel, n_experts=E),
        grid=(N // tm, E),
        in_specs=[pl.BlockSpec((tm, D), lambda i, e: (i, 0)), full(rw), full(rb),
                  pl.BlockSpec((None, D, De), lambda i, e: (e, 0, 0)),
                  pl.BlockSpec((None, D, De), lambda i, e: (e, 0, 0)),
                  pl.BlockSpec((None, De, D), lambda i, e: (e, 0, 0)),
                  full(ln_g), full(ln_b)],
        out_specs=pl.BlockSpec((tm, D), lambda i, e: (i, 0)),
        out_shape=jax.ShapeDtypeStruct((N, D), F32),
        scratch_shapes=[pltpu.VMEM((tm, D), BF16), pltpu.VMEM((tm, E), F32), pltpu.VMEM((tm, D), F32)],
        compiler_params=pltpu.CompilerParams(
            dimension_semantics=("arbitrary", "arbitrary"),
            vmem_limit_bytes=VMEM_LIMIT_BYTES),
    )(x2d, rw, rb, wg, wu, wd, ln_g, ln_b)


def _route_kernel(x_ref, rwt_ref, rbc_ref, meta_ref, pc_ref, *, n_experts):
    E = n_experts
    per_group = E // N_EXPERT_GROUPS
    x = x_ref[...]
    tm = x.shape[0]
    logits = lax.dot_general(rwt_ref[...], x, (((1,), (1,)), ((), ())),
                             precision=HIGHEST, preferred_element_type=F32) + rbc_ref[...]
    logits = logits - jnp.max(logits, axis=0, keepdims=True)
    ex = jnp.exp(logits)
    p = ex / jnp.sum(ex, axis=0, keepdims=True)
    eid_i = lax.broadcasted_iota(jnp.int32, (E, tm), 0)
    grp = eid_i // per_group
    eid = eid_i.astype(F32)
    big = float(E)

    def top2(pm):
        m1 = jnp.max(pm, axis=0, keepdims=True)
        i1 = jnp.min(jnp.where(pm == m1, eid, big), axis=0, keepdims=True)
        pm2 = jnp.where(eid == i1, -1.0, pm)
        m2 = jnp.max(pm2, axis=0, keepdims=True)
        i2 = jnp.min(jnp.where(pm2 == m2, eid, big), axis=0, keepdims=True)
        return m1, i1, m2, i2

    best = None
    g_sel = None
    for j in range(N_EXPERT_GROUPS):
        m1, _, m2, _ = top2(jnp.where(grp == j, p, -1.0))
        score = m1 + m2
        if best is None:
            best, g_sel = score, jnp.zeros_like(eid_i[:1, :])
        else:
            better = score > best
            best = jnp.where(better, score, best)
            g_sel = jnp.where(better, j, g_sel)
    m1, i1, m2, i2 = top2(jnp.where(grp == g_sel, p, -1.0))
    den = m1 + m2

    oh1 = jnp.where(eid == i1, 1.0, 0.0)
    oh2 = jnp.where(eid == i2, 1.0, 0.0)
    earlier = jnp.where(lax.broadcasted_iota(jnp.int32, (tm, tm), 0)
                        < lax.broadcasted_iota(jnp.int32, (tm, tm), 1), 1.0, 0.0).astype(BF16)
    r1 = _dot(oh1.astype(BF16), earlier)
    r2 = _dot(oh2.astype(BF16), earlier)
    cnt1 = jnp.sum(oh1, axis=1, keepdims=True)
    cnt2 = jnp.sum(oh2, axis=1, keepdims=True)
    pc = jnp.floor((cnt1 + cnt2 + (MOE_CHUNK - 1.0)) * (1.0 / MOE_CHUNK)) * MOE_CHUNK
    pc_b = jnp.broadcast_to(pc, (E, 128))
    below = jnp.where(lax.broadcasted_iota(jnp.int32, (E, E), 1)
                      < lax.broadcasted_iota(jnp.int32, (E, E), 0), 1.0, 0.0)
    run_off = jnp.dot(below, pc_b, precision=HIGHEST, preferred_element_type=F32)[:, 0:1]
    meta_ref[0:1, :] = jnp.sum(oh1 * (run_off + r1), axis=0, keepdims=True)
    meta_ref[1:2, :] = jnp.sum(oh2 * (run_off + cnt1 + r2), axis=0, keepdims=True)
    meta_ref[2:3, :] = m1 / den
    meta_ref[3:4, :] = m2 / den
    meta_ref[4:8, :] = jnp.zeros((4, tm), F32)
    pc_ref[...] = pc_b


def _route(x2d, rwt, rbc):
    N, D = x2d.shape
    E = rwt.shape[0]
    nt = N // MOE_TILE
    full = lambda a: pl.BlockSpec(a.shape, lambda i: (0,) * a.ndim)
    return pl.pallas_call(
        functools.partial(_route_kernel, n_experts=E),
        grid=(nt,),
        in_specs=[pl.BlockSpec((MOE_TILE, D), lambda i: (i, 0)), full(rwt), full(rbc)],
        out_specs=(pl.BlockSpec((None, 8, MOE_TILE), lambda i: (i, 0, 0)),
                   pl.BlockSpec((None, E, 128), lambda i: (i, 0, 0))),
        out_shape=(jax.ShapeDtypeStruct((nt, 8, MOE_TILE), F32), jax.ShapeDtypeStruct((nt, E, 128), F32)),
        compiler_params=pltpu.CompilerParams(dimension_semantics=("arbitrary",),
                                             vmem_limit_bytes=VMEM_LIMIT_BYTES),
    )(x2d, rwt, rbc)


def _moe_schedule(pc, n_blocks_max):
    nt, E = pc.shape
    cum = jnp.cumsum(pc, axis=1)
    run_off = cum - pc
    total = cum[:, -1]
    seg_len = jnp.sum(pc, axis=0)
    seg_pad = ((seg_len + MOE_BLOCK - 1) // MOE_BLOCK) * MOE_BLOCK
    seg_end = jnp.cumsum(seg_pad)
    goff = (seg_end - seg_pad)[None, :] + (jnp.cumsum(pc, axis=0) - pc)
    r = jnp.arange(MOE_SLOTS // MOE_CHUNK, dtype=jnp.int32) * MOE_CHUNK
    e_c = jnp.minimum(jnp.sum((cum[:, None, :] <= r[None, :, None]).astype(jnp.int32), axis=-1), E - 1)
    dst_row = (jnp.take_along_axis(goff, e_c, axis=1) + r[None, :] - jnp.take_along_axis(run_off, e_c, axis=1))
    dst_chunk = jnp.where(r[None, :] < total[:, None], dst_row // MOE_CHUNK, -1).astype(jnp.int32)
    blk_start = jnp.arange(n_blocks_max, dtype=jnp.int32) * MOE_BLOCK
    block_expert = jnp.minimum(jnp.sum((seg_end[None, :] <= blk_start[:, None]).astype(jnp.int32), axis=-1), E - 1)
    n_valid = (seg_end[-1] // MOE_BLOCK).reshape(1).astype(jnp.int32)
    n_chunks_all = n_blocks_max * (MOE_BLOCK // MOE_CHUNK)
    flat = dst_chunk.reshape(-1)
    covered = jnp.zeros((n_chunks_all + 1,), jnp.bool_).at[jnp.where(flat >= 0, flat, n_chunks_all)].set(True)
    n_fill_max = n_chunks_all - (nt * 2 * MOE_TILE) // MOE_CHUNK
    fill = jnp.nonzero(~covered[:n_chunks_all], size=n_fill_max, fill_value=0)[0].astype(jnp.int32)
    n_fill = (n_chunks_all - jnp.sum(covered[:n_chunks_all].astype(jnp.int32))).reshape(1).astype(jnp.int32)
    return flat, total.astype(jnp.int32), block_expert.astype(jnp.int32), n_valid, fill, n_fill


def _chunk_copy(hbm_ref, buf, sem, b, c, d, to_hbm):
    local = buf.at[b, pl.ds(c * MOE_CHUNK, MOE_CHUNK)]
    remote = hbm_ref.at[pl.ds(pl.multiple_of(d * MOE_CHUNK, MOE_CHUNK), MOE_CHUNK)]
    if to_hbm:
        return pltpu.make_async_copy(local, remote, sem.at[b])
    return pltpu.make_async_copy(remote, local, sem.at[b])


def _for_tile_chunks(dst_ref, tile, fn):
    n_chunks = MOE_SLOTS // MOE_CHUNK
    for c in range(n_chunks):
        d = dst_ref[tile * n_chunks + c]

        @pl.when(d >= 0)
        def _():
            fn(c, d)


def _dispatch_kernel(dst_ref, fill_ref, nfill_ref, x_ref, meta_ref, xs_hbm, buf, sem, zbuf, zsem):
    i = pl.program_id(0)
    n = pl.num_programs(0)
    b = i % 2
    tm = x_ref.shape[0]

    def wait_tile(tile, slot):
        _for_tile_chunks(dst_ref, tile, lambda c, d: _chunk_copy(xs_hbm, buf, sem, slot, c, d, True).wait())

    def zero_copy(k):
        d = fill_ref[k]
        return pltpu.make_async_copy(
            zbuf, xs_hbm.at[pl.ds(pl.multiple_of(d * MOE_CHUNK, MOE_CHUNK), MOE_CHUNK)], zsem.at[0])

    @pl.when(i == 0)
    def _():
        zbuf[...] = jnp.zeros_like(zbuf)
        lax.fori_loop(0, nfill_ref[0], lambda k, c: (zero_copy(k).start(), c)[1], 0)

    @pl.when(i >= 2)
    def _():
        wait_tile(i - 2, b)

    s1 = meta_ref[0:1, :].astype(jnp.int32)
    s2 = meta_ref[1:2, :].astype(jnp.int32)
    rows = lax.broadcasted_iota(jnp.int32, (MOE_SLOTS, tm), 0)
    dsp = jnp.where((rows == s1) | (rows == s2), 1.0, 0.0).astype(BF16)
    buf[b] = _dot(dsp, x_ref[...].astype(BF16)).astype(BF16)
    _for_tile_chunks(dst_ref, i, lambda c, d: _chunk_copy(xs_hbm, buf, sem, b, c, d, True).start())

    @pl.when(i == n - 1)
    def _():
        wait_tile(i, b)

    @pl.when((i == n - 1) & (i >= 1))
    def _():
        wait_tile(i - 1, 1 - b)

    @pl.when(i == n - 1)
    def _():
        lax.fori_loop(0, nfill_ref[0], lambda k, c: (zero_copy(k).wait(), c)[1], 0)


def _dispatch(x2d, meta, dst_chunk, fill, n_fill, n_rows):
    N, D = x2d.shape
    nt = N // MOE_TILE
    return pl.pallas_call(
        _dispatch_kernel,
        grid_spec=pltpu.PrefetchScalarGridSpec(
            num_scalar_prefetch=3, grid=(nt,),
            in_specs=[pl.BlockSpec((MOE_TILE, D), lambda i, *_: (i, 0)),
                      pl.BlockSpec((None, 8, MOE_TILE), lambda i, *_: (i, 0, 0))],
            out_specs=pl.BlockSpec(memory_space=pl.ANY),
            scratch_shapes=[pltpu.VMEM((2, MOE_SLOTS, D), BF16), pltpu.SemaphoreType.DMA((2,)),
                            pltpu.VMEM((MOE_CHUNK, D), BF16), pltpu.SemaphoreType.DMA((1,))]),
        out_shape=jax.ShapeDtypeStruct((n_rows, D), BF16),
        compiler_params=pltpu.CompilerParams(dimension_semantics=("arbitrary",),
                                             vmem_limit_bytes=VMEM_LIMIT_BYTES),
    )(dst_chunk, fill, n_fill, x2d, meta)


def _ffn_kernel(be_ref, nv_ref, xs_ref, wg_ref, wu_ref, wd_ref, ys_ref):
    del be_ref
    used = pl.program_id(0) < nv_ref[0]

    @pl.when(used)
    def _():
        xb = xs_ref[...]
        g = _dot(xb, wg_ref[...])
        up = _dot(xb, wu_ref[...])
        hmid = (g * _sigmoid(g) * up).astype(BF16)
        ys_ref[...] = _dot(hmid, wd_ref[...]).astype(ys_ref.dtype)

    @pl.when(jnp.logical_not(used))
    def _():
        ys_ref[...] = jnp.zeros_like(ys_ref)


def _ffn(xs, wg, wu, wd, block_expert, n_valid):
    R, D = xs.shape
    E, _, De = wg.shape
    last = lambda j, nv: jnp.minimum(j, nv[0] - 1)
    return pl.pallas_call(
        _ffn_kernel,
        grid_spec=pltpu.PrefetchScalarGridSpec(
            num_scalar_prefetch=2, grid=(R // MOE_BLOCK,),
            in_specs=[pl.BlockSpec((MOE_BLOCK, D), lambda j, be, nv: (last(j, nv), 0)),
                      pl.BlockSpec((None, D, De), lambda j, be, nv: (be[last(j, nv)], 0, 0)),
                      pl.BlockSpec((None, D, De), lambda j, be, nv: (be[last(j, nv)], 0, 0)),
                      pl.BlockSpec((None, De, D), lambda j, be, nv: (be[last(j, nv)], 0, 0))],
            out_specs=pl.BlockSpec((MOE_BLOCK, D), lambda j, be, nv: (j, 0))),
        out_shape=jax.ShapeDtypeStruct((R, D), BF16),
        compiler_params=pltpu.CompilerParams(dimension_semantics=("arbitrary",),
                                             vmem_limit_bytes=VMEM_LIMIT_BYTES),
    )(block_expert, n_valid, xs, wg, wu, wd)


def _combine_kernel(dst_ref, tot_ref, x_ref, metac_ref, ys_hbm, ln_g_ref, ln_b_ref, o_ref, buf, sem):
    i = pl.program_id(0)
    n = pl.num_programs(0)
    b = i % 2
    tm = x_ref.shape[0]

    def fetch(tile, slot):
        _for_tile_chunks(dst_ref, tile, lambda c, d: _chunk_copy(ys_hbm, buf, sem, slot, c, d, False).start())

    @pl.when(i == 0)
    def _():
        fetch(0, 0)

    @pl.when(i + 1 < n)
    def _():
        fetch(i + 1, 1 - b)

    _for_tile_chunks(dst_ref, i, lambda c, d: _chunk_copy(ys_hbm, buf, sem, b, c, d, False).wait())
    live = lax.broadcasted_iota(jnp.int32, (MOE_SLOTS, 1), 0) < tot_ref[i]
    ys = jnp.where(live, buf[b], jnp.zeros((), BF16))
    mc = metac_ref[...]
    s1 = mc[:, 0:1].astype(jnp.int32)
    s2 = mc[:, 1:2].astype(jnp.int32)
    lane = lax.broadcasted_iota(jnp.int32, (tm, MOE_SLOTS), 1)
    cmb = (jnp.where(lane == s1, mc[:, 2:3], 0.0) + jnp.where(lane == s2, mc[:, 3:4], 0.0)).astype(BF16)
    o_ref[...] = _layer_norm(ALPHA * x_ref[...] + _dot(cmb, ys), ln_g_ref[...], ln_b_ref[...])


def _combine(x2d, meta_col, ys, dst_chunk, total, ln_g, ln_b):
    N, D = x2d.shape
    nt = N // MOE_TILE
    return pl.pallas_call(
        _combine_kernel,
        grid_spec=pltpu.PrefetchScalarGridSpec(
            num_scalar_prefetch=2, grid=(nt,),
            in_specs=[pl.BlockSpec((MOE_TILE, D), lambda i, dst, tot: (i, 0)),
                      pl.BlockSpec((MOE_TILE, 8), lambda i, dst, tot: (i, 0)),
                      pl.BlockSpec(memory_space=pl.ANY),
                      pl.BlockSpec(ln_g.shape, lambda i, dst, tot: (0, 0)),
                      pl.BlockSpec(ln_b.shape, lambda i, dst, tot: (0, 0))],
            out_specs=pl.BlockSpec((MOE_TILE, D), lambda i, dst, tot: (i, 0)),
            scratch_shapes=[pltpu.VMEM((2, MOE_SLOTS, D), BF16), pltpu.SemaphoreType.DMA((2,))]),
        out_shape=jax.ShapeDtypeStruct((N, D), F32),
        compiler_params=pltpu.CompilerParams(dimension_semantics=("arbitrary",),
                                             vmem_limit_bytes=VMEM_LIMIT_BYTES),
    )(dst_chunk, total, x2d, meta_col, ys, ln_g, ln_b)


def _moe_sparse(x2d, rw, rb, wg, wu, wd, ln_g, ln_b):
    N, D = x2d.shape
    E = rw.shape[1]
    nt = N // MOE_TILE
    meta, pc = _route(x2d, rw.T, rb.reshape(E, 1))
    pc = pc[:, :, 0].astype(jnp.int32)
    n_rows = nt * MOE_SLOTS + E * MOE_BLOCK
    assert n_rows % MOE_BLOCK == 0
    dst_chunk, total, block_expert, n_valid, fill, n_fill = _moe_schedule(pc, n_rows // MOE_BLOCK)
    xs = _dispatch(x2d, meta, dst_chunk, fill, n_fill, n_rows)
    ys = _ffn(xs, wg, wu, wd, block_expert, n_valid)
    meta_col = jnp.swapaxes(meta, 1, 2).reshape(N, 8)
    return _combine(x2d, meta_col, ys, dst_chunk, total, ln_g, ln_b)


def _s5_weights_kernel(lam_col_ref, lam_row_ref, bt_ref, ct_ref, d_ref,
                       toe_ref, pst_ref, qst_ref, a_pow_ref):
    L, Cg = S5_CHUNK, S5_GROUP
    LC = L * Cg
    P = lam_row_ref.shape[-1]

    def discretise(a_re, a_im, log_dt):
        dt = jnp.exp(log_dt)
        lr, li = dt * a_re, dt * a_im
        er = jnp.exp(lr)
        abr, abi = er * jnp.cos(li), er * jnp.sin(li)
        nr, ni = abr - 1.0, abi
        den = a_re * a_re + a_im * a_im
        cr = (nr * a_re + ni * a_im) / den
        ci = (ni * a_re - nr * a_im) / den
        return lr, li, cr, ci

    def a_power(lr, li, tau):
        er = jnp.exp(tau * lr)
        return er * jnp.cos(tau * li), er * jnp.sin(tau * li)

    lc = lam_col_ref[...]
    lr_c, li_c, _, _ = discretise(lc[:, 0:1], lc[:, 1:2], lc[:, 2:3])
    lw = lam_row_ref[...]
    lr_r, li_r, cr_r, ci_r = discretise(lw[0:1, :], lw[1:2, :], lw[2:3, :])

    btr, bti = bt_ref[0], bt_ref[1]
    bbr = cr_r * btr - ci_r * bti
    bbi = cr_r * bti + ci_r * btr

    ctr, cti = ct_ref[0], ct_ref[1]
    tau_l = (lax.broadcasted_iota(jnp.int32, (P, LC), 1) // Cg).astype(F32)

    pr, pi = a_power(lr_c, li_c, tau_l)
    car = ctr * pr - cti * pi
    cai = ctr * pi + cti * pr
    t0 = (jnp.dot(bbr[0:Cg, :], car, preferred_element_type=F32, precision=HIGHEST)
          - jnp.dot(bbi[0:Cg, :], cai, preferred_element_type=F32, precision=HIGHEST))
    r16 = lax.broadcasted_iota(jnp.int32, (Cg, LC), 0)
    l16 = lax.broadcasted_iota(jnp.int32, (Cg, LC), 1)
    t0 = t0 + jnp.where(r16 == l16, d_ref[...], 0.0)
    toe_ref[0:Cg, :] = t0
    for s in range(1, L):
        toe_ref[s * Cg:(s + 1) * Cg, :] = jnp.where(l16 >= s * Cg, pltpu.roll(t0, s * Cg, axis=1), 0.0)

    pr, pi = a_power(lr_c, li_c, tau_l + 1.0)
    qst_ref[0:P, :] = ctr * pr - cti * pi
    qst_ref[P:2 * P, :] = -(ctr * pi + cti * pr)

    tau_r = (L - 1 - lax.broadcasted_iota(jnp.int32, (LC, P), 0) // Cg).astype(F32)
    pr, pi = a_power(lr_r, li_r, tau_r)
    pst_ref[:, 0:P] = bbr * pr - bbi * pi
    pst_ref[:, P:2 * P] = bbr * pi + bbi * pr

    pr, pi = a_power(lr_r, li_r, float(L))
    a_pow_ref[0:1, :] = pr
    a_pow_ref[1:2, :] = pi


def _s5_weights(a_re, a_im, log_dt, b_re, b_im, c_re, c_im, d):
    G, P = a_re.shape
    Cg, L = S5_GROUP, S5_CHUNK
    LC = L * Cg
    lam = jnp.stack([a_re, a_im, jnp.broadcast_to(log_dt[:, None], (G, P)), jnp.zeros((G, P), F32)], axis=1)
    lam_row = lam
    lam_col = jnp.swapaxes(lam, 1, 2)
    bt = jnp.stack([jnp.swapaxes(b_re, 1, 2), jnp.swapaxes(b_im, 1, 2)], axis=1)
    bt = jnp.tile(bt, (1, 1, L, 1))
    ct = jnp.stack([jnp.swapaxes(c_re, 1, 2), jnp.swapaxes(c_im, 1, 2)], axis=1)
    ct = jnp.tile(ct, (1, 1, 1, L))
    d_col = d.reshape(G, Cg, 1)
    blk = lambda shp: pl.BlockSpec((None,) + shp, lambda g: (g,) + (0,) * len(shp))
    return pl.pallas_call(
        _s5_weights_kernel,
        grid=(G,),
        in_specs=[blk((P, 4)), blk((4, P)), blk((2, LC, P)), blk((2, P, LC)), blk((Cg, 1))],
        out_specs=(blk((LC, LC)), blk((LC, 2 * P)), blk((2 * P, LC)), blk((2, P))),
        out_shape=(jax.ShapeDtypeStruct((G, LC, LC), F32),
                   jax.ShapeDtypeStruct((G, LC, 2 * P), F32),
                   jax.ShapeDtypeStruct((G, 2 * P, LC), F32),
                   jax.ShapeDtypeStruct((G, 2, P), F32)),
        compiler_params=pltpu.CompilerParams(dimension_semantics=("arbitrary",)),
    )(lam_col, lam_row, bt, ct, d_col)


def _proj_kernel(x_ref, w_ref, o_ref):
    o_ref[...] = _dot(x_ref[...].astype(BF16), w_ref[...]).astype(o_ref.dtype)


def _proj(x2d, w, tm, out_dtype):
    N, D = x2d.shape
    Do = w.shape[1]
    return pl.pallas_call(
        _proj_kernel,
        grid=(N // tm,),
        in_specs=[pl.BlockSpec((tm, D), lambda i: (i, 0)), pl.BlockSpec(w.shape, lambda i: (0, 0))],
        out_specs=pl.BlockSpec((tm, Do), lambda i: (i, 0)),
        out_shape=jax.ShapeDtypeStruct((N, Do), out_dtype),
        compiler_params=pltpu.CompilerParams(dimension_semantics=("arbitrary",),
                                             vmem_limit_bytes=VMEM_LIMIT_BYTES),
    )(x2d, w)


def _s5_scan_kernel(x_ref, s0_ref, toe_ref, pst_ref, qst_ref, apow_ref, y_ref, sfin_ref, z_scr, sin_scr,
                    *, n_chunks, batch):
    P = apow_ref.shape[-1]
    xg = x_ref[...]
    z_scr[...] = _dot(xg, pst_ref[...].astype(BF16))
    ar = apow_ref[0:1, :]
    ai = apow_ref[1:2, :]

    def step(n, carry):
        sr, si = carry
        rows = pl.ds(pl.multiple_of(n * batch, batch), batch)
        sin_scr[rows, 0:P] = sr
        sin_scr[rows, P:2 * P] = si
        zc = z_scr[rows, :]
        return (ar * sr - ai * si + zc[:, 0:P], ar * si + ai * sr + zc[:, P:2 * P])

    sr, si = lax.fori_loop(0, n_chunks, step, (s0_ref[0], s0_ref[1]))
    sfin_ref[0] = sr
    sfin_ref[1] = si
    y = _dot(xg, toe_ref[...].astype(BF16)) + _dot(sin_scr[...].astype(BF16), qst_ref[...].astype(BF16))
    y_ref[...] = y.astype(y_ref.dtype)


def _s5_scan(xg, s0, toe, pst, qst, apow, n_chunks, batch):
    G, R, LC = xg.shape
    P = apow.shape[-1]
    blk = lambda shp: pl.BlockSpec((None,) + shp, lambda g: (g,) + (0,) * len(shp))
    return pl.pallas_call(
        functools.partial(_s5_scan_kernel, n_chunks=n_chunks, batch=batch),
        grid=(G,),
        in_specs=[blk((R, LC)), blk((2, batch, P)), blk((LC, LC)), blk((LC, 2 * P)), blk((2 * P, LC)),
                  blk((2, P))],
        out_specs=(blk((R, LC)), blk((2, batch, P))),
        out_shape=(jax.ShapeDtypeStruct((G, R, LC), BF16), jax.ShapeDtypeStruct((G, 2, batch, P), F32)),
        scratch_shapes=[pltpu.VMEM((R, 2 * P), F32), pltpu.VMEM((R, 2 * P), F32)],
        compiler_params=pltpu.CompilerParams(dimension_semantics=("arbitrary",),
                                             vmem_limit_bytes=VMEM_LIMIT_BYTES),
    )(xg, s0, toe, pst, qst, apow)


def _s5_out_kernel(x_ref, y_ref, w_ref, ln_g_ref, ln_b_ref, o_ref, *, d_model):
    act = _gelu_tanh(y_ref[...].astype(F32)).astype(BF16)
    zz = _dot(act, w_ref[...])
    mix = zz[:, :d_model] * _sigmoid(zz[:, d_model:])
    o_ref[...] = _layer_norm(ALPHA * x_ref[...] + mix, ln_g_ref[...], ln_b_ref[...])


def _s5_out(x2d, y2d, w_out, ln_g, ln_b, tm):
    N, D = x2d.shape
    full = lambda a: pl.BlockSpec(a.shape, lambda i: (0,) * a.ndim)
    return pl.pallas_call(
        functools.partial(_s5_out_kernel, d_model=D),
        grid=(N // tm,),
        in_specs=[pl.BlockSpec((tm, D), lambda i: (i, 0)), pl.BlockSpec((tm, D), lambda i: (i, 0)),
                  full(w_out), full(ln_g), full(ln_b)],
        out_specs=pl.BlockSpec((tm, D), lambda i: (i, 0)),
        out_shape=jax.ShapeDtypeStruct((N, D), F32),
        compiler_params=pltpu.CompilerParams(dimension_semantics=("arbitrary",),
                                             vmem_limit_bytes=VMEM_LIMIT_BYTES),
    )(x2d, y2d, w_out, ln_g, ln_b)


def _s5_mixer(x, s0_re, s0_im, w_in, w_out, s5w, ln_g, ln_b, tm):
    B, T, D = x.shape
    toe, pst, qst, apow = s5w
    G = toe.shape[0]
    L, Cg = S5_CHUNK, S5_GROUP
    n = T // L
    x2d = x.reshape(B * T, D)
    u = _proj(x2d, w_in, tm, BF16)
    xg = u.reshape(B, n, L, G, Cg).transpose(3, 1, 0, 2, 4).reshape(G, n * B, L * Cg)
    s0 = jnp.stack([s0_re, s0_im], axis=0).transpose(2, 0, 1, 3)
    yg, sfin = _s5_scan(xg, s0, toe, pst, qst, apow, n, B)
    y2d = yg.reshape(G, n, B, L, Cg).transpose(2, 1, 3, 0, 4).reshape(B * T, D)
    out = _s5_out(x2d, y2d, w_out, ln_g, ln_b, tm).reshape(B, T, D)
    sfin = sfin.transpose(1, 2, 0, 3)
    return out, sfin[0], sfin[1]


def _block_diag(w):
    H, d, _ = w.shape
    eye = jnp.eye(H, dtype=w.dtype)
    return (eye[:, None, :, None] * w[:, :, None, :]).reshape(H * d, H * d)


def kernel(x_prompt, x_sample, state_lru_h, state_lru_conv, state_hgrn2, state_s5_re, state_s5_im, ln_mix_g, ln_mix_b, ln_ffn_g, ln_ffn_b, w_in_ab, w_out_ab, lru_conv_w, lru_conv_b, lru_w_r, lru_b_r, lru_w_i, lru_b_i, lru_lam, hgrn_lb_logits, hgrn_norm_g, w_in_s5, w_out_s5, s5_a_re, s5_a_im, s5_b_re, s5_b_im, s5_c_re, s5_c_im, s5_d, s5_log_dt, router_w, router_b, moe_w_gate, moe_w_up, moe_w_down):
    Bp, Tp, D = x_prompt.shape
    Bs, Ts, _ = x_sample.shape
    W = state_lru_h.shape[-1]
    dk = W // HG_HEADS
    row = lambda v: v.reshape(1, -1)

    def moe(y, l, sparse):
        fn = _moe_sparse if sparse else functools.partial(_moe_dense, tm=512)
        out = fn(y.reshape(-1, D), router_w, row(router_b), moe_w_gate[l].astype(BF16),
                 moe_w_up[l].astype(BF16), moe_w_down[l].astype(BF16), row(ln_ffn_g[l]), row(ln_ffn_b[l]))
        return out.reshape(y.shape)

    j = 0
    wts = (w_in_ab[j].astype(BF16), w_out_ab[j].astype(BF16), lru_conv_w[j], row(lru_conv_b[j]),
           _block_diag(lru_w_r[j]).astype(BF16), row(lru_b_r[j]),
           _block_diag(lru_w_i[j]).astype(BF16), row(lru_b_i[j]),
           row(lru_lam[j]), hgrn_lb_logits, row(hgrn_norm_g[j]), row(ln_mix_g[0]), row(ln_mix_b[0]))
    yp, p_h, p_conv, p_hg = _ab_mixer(
        x_prompt, jnp.zeros((Bp, 1, W), F32), jnp.zeros((Bp, 3, W), F32),
        jnp.zeros((Bp, HG_HEADS, dk, dk), F32), wts, 256)
    ys, s_h, s_conv, s_hg = _ab_mixer(
        x_sample, state_lru_h[j][:, None, :], state_lru_conv[j], state_hgrn2[j], wts, Ts)
    yp = moe(yp, 0, True)
    ys = moe(ys, 0, False)

    s5w = _s5_weights(s5_a_re[j], s5_a_im[j], s5_log_dt[j], s5_b_re[j], s5_b_im[j],
                      s5_c_re[j], s5_c_im[j], s5_d[j])
    G, P = s5_a_re[j].shape
    w_in5 = w_in_s5[j].astype(BF16)
    w_out5 = w_out_s5[j].astype(BF16)
    zp = jnp.zeros((Bp, G, P), F32)
    yp, p_re, p_im = _s5_mixer(yp, zp, zp, w_in5, w_out5, s5w, row(ln_mix_g[1]), row(ln_mix_b[1]), 512)
    ys, s_re, s_im = _s5_mixer(ys, state_s5_re[j], state_s5_im[j], w_in5, w_out5, s5w,
                               row(ln_mix_g[1]), row(ln_mix_b[1]), 512)
    yp = moe(yp, 1, True)
    ys = moe(ys, 1, False)

    return (yp, ys,
            p_h.reshape(1, Bp, W), p_conv[None], p_hg[None], p_re[None], p_im[None],
            s_h.reshape(1, Bs, W), s_conv[None], s_hg[None], s_re[None], s_im[None])
```

```python
import functools
import math

import jax
import jax.numpy as jnp
from jax import lax
from jax.experimental import pallas as pl
from jax.experimental.pallas import tpu as pltpu

F32 = jnp.float32
BF16 = jnp.bfloat16
HIGHEST = lax.Precision.HIGHEST

DEPTH = 2
RG_C = 8.0
LRU_HEADS = 8
HG_HEADS = 4
HG_CHUNK = 16
RMS_EPS = 1e-6
S5_GROUP = 16
S5_STEP_TOKENS = 32
S5_CHUNK = 16
N_EXPERT_GROUPS = 4
ALPHA = (2.0 * DEPTH) ** 0.25
LN_EPS = 1e-5
MOE_TILE = 256
MOE_CHUNK = 16
MOE_SLOTS = 768
MOE_BLOCK = 512
VMEM_LIMIT_BYTES = 56 * 1024 * 1024


def _dot(a, b):
    return jnp.dot(a, b, preferred_element_type=F32)


def _dot_nt(a, b):
    return lax.dot_general(a, b, (((1,), (1,)), ((), ())), preferred_element_type=F32)


def _sigmoid(x):
    return 1.0 / (1.0 + jnp.exp(-x))


def _gelu_tanh(x):
    c = math.sqrt(2.0 / math.pi)
    return 0.5 * x * (1.0 + jnp.tanh(c * (x + 0.044715 * (x * x * x))))


def _layer_norm(s, g, b):
    mu = jnp.mean(s, axis=-1, keepdims=True)
    d = s - mu
    var = jnp.mean(d * d, axis=-1, keepdims=True)
    return d * lax.rsqrt(var + LN_EPS) * g + b


def _shift_rows(x, s, row):
    del row
    return pltpu.roll(x, s, axis=0)


def _ab_kernel(x_ref, h0_ref, conv0_ref, s0_ref, w_in_ref, w_out_ref, conv_w_ref, conv_b_ref,
               wr_ref, br_ref, wi_ref, bi_ref, lam_ref, lb_logits_ref, norm_g_ref, ln_g_ref, ln_b_ref,
               y_ref, h_out_ref, conv_out_ref, s_out_ref,
               zx_buf, h_carry, st_carry, st_all, *, tb, width):
    t = pl.program_id(1)
    nt = pl.num_programs(1)
    W = width
    L = HG_CHUNK
    nc = tb // L
    dk = W // HG_HEADS

    @pl.when(t == 0)
    def _init():
        zx_buf[0:8, :] = jnp.zeros((8, W), F32)
        zx_buf[5:8, :] = conv0_ref[...]
        h_carry[...] = jnp.broadcast_to(h0_ref[...], h_carry.shape)
        for hd in range(HG_HEADS):
            st_carry[hd] = s0_ref[hd].T

    x = x_ref[...]
    z = _dot(x.astype(BF16), w_in_ref[...])
    z_gate = z[:, 0 * W:1 * W]
    z_x = z[:, 1 * W:2 * W]
    z_q = z[:, 2 * W:3 * W]
    z_f = z[:, 3 * W:4 * W]
    z_i = z[:, 4 * W:5 * W]
    z_g = z[:, 5 * W:6 * W]

    zx_buf[8:8 + tb, :] = z_x
    cw = conv_w_ref[...]
    u = (conv_b_ref[...] + cw[3:4, :] * z_x
         + cw[2:3, :] * zx_buf[7:7 + tb, :]
         + cw[1:2, :] * zx_buf[6:6 + tb, :]
         + cw[0:1, :] * zx_buf[5:5 + tb, :])
    tail = zx_buf[tb:tb + 8, :]
    zx_buf[0:8, :] = tail

    ub = u.astype(BF16)
    r = _sigmoid(_dot(ub, wr_ref[...]) + br_ref[...])
    ig = _sigmoid(_dot(ub, wi_ref[...]) + bi_ref[...])
    nlam = -lam_ref[...]
    softplus_nlam = jnp.maximum(nlam, 0.0) + jnp.log1p(jnp.exp(-jnp.abs(nlam)))
    log_a = (-RG_C) * r * softplus_nlam
    a = jnp.exp(log_a)
    th = jnp.tanh(log_a)
    b = jnp.sqrt(-2.0 * th / (1.0 - th)) * ig * u

    row = lax.broadcasted_iota(jnp.int32, (tb, W), 0)
    s = 1
    while s < tb:
        m = row >= s
        a_sh = _shift_rows(a, s, row)
        b_sh = _shift_rows(b, s, row)
        b = jnp.where(m, a * b_sh + b, b)
        a = jnp.where(m, a * a_sh, a)
        s *= 2
    h = b + a * h_carry[0:1, :]
    h_carry[...] = jnp.broadcast_to(h[tb - 1:tb, :], h_carry.shape)
    out_a = h * _gelu_tanh(z_gate)

    lg = lb_logits_ref[...]
    e = jnp.exp(lg - jnp.max(lg, axis=0, keepdims=True))
    lb = e[0:1, :] / jnp.sum(e, axis=0, keepdims=True)
    f = lb + (1.0 - lb) * _sigmoid(z_f)
    log_f = jnp.log(f)
    kk = 1.0 - f
    rin = row % L
    G = log_f
    s = 1
    while s < L:
        G = G + jnp.where(rin >= s, _shift_rows(G, s, row), 0.0)
        s *= 2
    G3 = G.reshape(nc, L, W)
    G_last = G3[:, L - 1:L, :]
    q_dec = (z_q * jnp.exp(G)).reshape(nc, L, W).astype(BF16)
    k_inc = (kk * jnp.exp(-G)).reshape(nc, L, W).astype(BF16)
    k_tail = (kk.reshape(nc, L, W) * jnp.exp(G_last - G3)).astype(BF16)
    decay = jnp.exp(G_last)
    v3 = z_i.reshape(nc, L, W).astype(BF16)
    causal = (lax.broadcasted_iota(jnp.int32, (nc, L, L), 1)
              >= lax.broadcasted_iota(jnp.int32, (nc, L, L), 2))

    o_heads = []
    for hd in range(HG_HEADS):
        sl = slice(hd * dk, (hd + 1) * dk)
        qd, ki, kt, vv = q_dec[:, :, sl], k_inc[:, :, sl], k_tail[:, :, sl], v3[:, :, sl]
        scores = jnp.einsum('ntk,nsk->nts', qd, ki, preferred_element_type=F32)
        scores = jnp.where(causal, scores, 0.0).astype(BF16)
        o_intra = jnp.einsum('nts,nsv->ntv', scores, vv, preferred_element_type=F32)
        vt = jnp.swapaxes(vv.astype(F32), 1, 2).astype(BF16)
        upd = jnp.einsum('nvs,nsk->nvk', vt, kt, preferred_element_type=F32)
        st = st_carry[hd]
        for n in range(nc):
            st_all[hd, n] = st.astype(BF16)
            st = st * decay[n, :, sl] + upd[n]
        st_carry[hd] = st
        o_inter = jnp.einsum('ntk,nvk->ntv', qd, st_all[hd], preferred_element_type=F32)
        o = (o_intra + o_inter).reshape(tb, dk)
        o = o * lax.rsqrt(jnp.mean(o * o, axis=-1, keepdims=True) + RMS_EPS)
        o_heads.append(o)
    o_all = jnp.concatenate(o_heads, axis=-1) * norm_g_ref[...]
    out_b = o_all * (z_g * _sigmoid(z_g))

    mixed = jnp.concatenate([out_a, out_b], axis=-1).astype(BF16)
    y = _dot(mixed, w_out_ref[...])
    y_ref[...] = _layer_norm(ALPHA * x + y, ln_g_ref[...], ln_b_ref[...])

    @pl.when(t == nt - 1)
    def _fin():
        h_out_ref[...] = h[tb - 1:tb, :]
        conv_out_ref[...] = tail[5:8, :]
        for hd in range(HG_HEADS):
            s_out_ref[hd] = st_carry[hd].T


def _ab_mixer(x, h0, conv0, s0, wts, tb):
    B, T, D = x.shape
    W = h0.shape[-1]
    dk = W // HG_HEADS
    nt = T // tb
    nc = tb // HG_CHUNK
    (w_in, w_out, conv_w, conv_b, wr, br, wi, bi, lam, lb_logits, norm_g, ln_g, ln_b) = wts
    full = lambda a: pl.BlockSpec(a.shape, lambda b, t: (0,) * a.ndim)
    per_b = lambda shp: pl.BlockSpec((None,) + shp, lambda b, t: (b,) + (0,) * len(shp))
    out_shape = (jax.ShapeDtypeStruct((B, T, D), F32),
                 jax.ShapeDtypeStruct((B, 1, W), F32),
                 jax.ShapeDtypeStruct((B, 3, W), F32),
                 jax.ShapeDtypeStruct((B, HG_HEADS, dk, dk), F32))
    return pl.pallas_call(
        functools.partial(_ab_kernel, tb=tb, width=W),
        grid=(B, nt),
        in_specs=[pl.BlockSpec((None, tb, D), lambda b, t: (b, t, 0)),
                  per_b((1, W)), per_b((3, W)), per_b((HG_HEADS, dk, dk))]
                 + [full(a) for a in wts],
        out_specs=(pl.BlockSpec((None, tb, D), lambda b, t: (b, t, 0)),
                   per_b((1, W)), per_b((3, W)), per_b((HG_HEADS, dk, dk))),
        out_shape=out_shape,
        scratch_shapes=[pltpu.VMEM((tb + 8, W), F32),
                        pltpu.VMEM((8, W), F32),
                        pltpu.VMEM((HG_HEADS, dk, dk), F32),
                        pltpu.VMEM((HG_HEADS, nc, dk, dk), BF16)],
        compiler_params=pltpu.CompilerParams(
            dimension_semantics=("arbitrary", "arbitrary"),
            vmem_limit_bytes=VMEM_LIMIT_BYTES),
    )(x, h0, conv0, s0, *wts)


def _router(x, rw, rb, n_experts):
    per_group = n_experts // N_EXPERT_GROUPS
    logits = jnp.dot(x, rw, preferred_element_type=F32, precision=HIGHEST) + rb
    logits = logits - jnp.max(logits, axis=-1, keepdims=True)
    ex = jnp.exp(logits)
    p = ex / jnp.sum(ex, axis=-1, keepdims=True)
    lane_i = lax.broadcasted_iota(jnp.int32, p.shape, 1)
    grp = lane_i // per_group
    lane = lane_i.astype(F32)
    big = float(n_experts)

    def top2(pm):
        m1 = jnp.max(pm, axis=-1, keepdims=True)
        i1 = jnp.min(jnp.where(pm == m1, lane, big), axis=-1, keepdims=True)
        pm2 = jnp.where(lane == i1, -1.0, pm)
        m2 = jnp.max(pm2, axis=-1, keepdims=True)
        i2 = jnp.min(jnp.where(pm2 == m2, lane, big), axis=-1, keepdims=True)
        return m1, i1, m2, i2

    best = None
    g_sel = None
    for j in range(N_EXPERT_GROUPS):
        m1, _, m2, _ = top2(jnp.where(grp == j, p, -1.0))
        score = m1 + m2
        if best is None:
            best, g_sel = score, jnp.zeros_like(lane_i[:, :1])
        else:
            better = score > best
            best = jnp.where(better, score, best)
            g_sel = jnp.where(better, j, g_sel)
    m1, i1, m2, i2 = top2(jnp.where(grp == g_sel, p, -1.0))
    den = m1 + m2
    return jnp.where(lane == i1, m1 / den, 0.0) + jnp.where(lane == i2, m2 / den, 0.0)


def _moe_dense_kernel(x_ref, rw_ref, rb_ref, wg_ref, wu_ref, wd_ref, ln_g_ref, ln_b_ref,
                      y_ref, xb_scr, comb_scr, acc_scr, *, n_experts):
    e = pl.program_id(1)

    @pl.when(e == 0)
    def _init():
        x = x_ref[...]
        xb_scr[...] = x.astype(BF16)
        comb_scr[...] = _router(x, rw_ref[...], rb_ref[...], n_experts)
        acc_scr[...] = jnp.zeros_like(acc_scr)

    comb = comb_scr[...]
    lane = lax.broadcasted_iota(jnp.int32, comb.shape, 1)
    c_e = jnp.sum(jnp.where(lane == e, comb, 0.0), axis=-1, keepdims=True)
    xb = xb_scr[...]
    g = _dot(xb, wg_ref[...])
    up = _dot(xb, wu_ref[...])
    hmid = (g * _sigmoid(g) * up).astype(BF16)
    acc_scr[...] += c_e * _dot(hmid, wd_ref[...])

    @pl.when(e == n_experts - 1)
    def _fin():
        y_ref[...] = _layer_norm(ALPHA * x_ref[...] + acc_scr[...], ln_g_ref[...], ln_b_ref[...])


def _moe_dense(x2d, rw, rb, wg, wu, wd, ln_g, ln_b, tm):
    N, D = x2d.shape
    E, _, De = wg.shape
    full = lambda a: pl.BlockSpec(a.shape, lambda i, e: (0,) * a.ndim)
    return pl.pallas_call(
        functools.partial(_moe_dense_kernel, n_experts=E),
        grid=(N // tm, E),
        in_specs=[pl.BlockSpec((tm, D), lambda i, e: (i, 0)), full(rw), full(rb),
                  pl.BlockSpec((None, D, De), lambda i, e: (e, 0, 0)),
                  pl.BlockSpec((None, D, De), lambda i, e: (e, 0, 0)),
                  pl.BlockSpec((None, De, D), lambda i, e: (e, 0, 0)),
                  full(ln_g), full(ln_b)],
        out_specs=pl.BlockSpec((tm, D), lambda i, e: (i, 0)),
        out_shape=jax.ShapeDtypeStruct((N, D), F32),
        scratch_shapes=[pltpu.VMEM((tm, D), BF16), pltpu.VMEM((tm, E), F32), pltpu.VMEM((tm, D), F32)],
        compiler_params=pltpu.CompilerParams(
            dimension_semantics=("arbitrary", "arbitrary"),
            vmem_limit_bytes=VMEM_LIMIT_BYTES),
    )(x2d, rw, rb, wg, wu, wd, ln_g, ln_b)


def _route_kernel(x_ref, rwt_ref, rbc_ref, meta_ref, pc_ref, *, n_experts):
    E = n_experts
    per_group = E // N_EXPERT_GROUPS
    x = x_ref[...]
    tm = x.shape[0]
    logits = lax.dot_general(rwt_ref[...], x, (((1,), (1,)), ((), ())),
                             precision=HIGHEST, preferred_element_type=F32) + rbc_ref[...]
    logits = logits - jnp.max(logits, axis=0, keepdims=True)
    ex = jnp.exp(logits)
    p = ex / jnp.sum(ex, axis=0, keepdims=True)
    eid_i = lax.broadcasted_iota(jnp.int32, (E, tm), 0)
    grp = eid_i // per_group
    eid = eid_i.astype(F32)
    big = float(E)

    def top2(pm):
        m1 = jnp.max(pm, axis=0, keepdims=True)
        i1 = jnp.min(jnp.where(pm == m1, eid, big), axis=0, keepdims=True)
        pm2 = jnp.where(eid == i1, -1.0, pm)
        m2 = jnp.max(pm2, axis=0, keepdims=True)
        i2 = jnp.min(jnp.where(pm2 == m2, eid, big), axis=0, keepdims=True)
        return m1, i1, m2, i2

    best = None
    g_sel = None
    for j in range(N_EXPERT_GROUPS):
        m1, _, m2, _ = top2(jnp.where(grp == j, p, -1.0))
        score = m1 + m2
        if best is None:
            best, g_sel = score, jnp.zeros_like(eid_i[:1, :])
        else:
            better = score > best
            best = jnp.where(better, score, best)
            g_sel = jnp.where(better, j, g_sel)
    m1, i1, m2, i2 = top2(jnp.where(grp == g_sel, p, -1.0))
    den = m1 + m2

    oh1 = jnp.where(eid == i1, 1.0, 0.0)
    oh2 = jnp.where(eid == i2, 1.0, 0.0)
    earlier = jnp.where(lax.broadcasted_iota(jnp.int32, (tm, tm), 0)
                        < lax.broadcasted_iota(jnp.int32, (tm, tm), 1), 1.0, 0.0).astype(BF16)
    r1 = _dot(oh1.astype(BF16), earlier)
    r2 = _dot(oh2.astype(BF16), earlier)
    cnt1 = jnp.sum(oh1, axis=1, keepdims=True)
    cnt2 = jnp.sum(oh2, axis=1, keepdims=True)
    pc = jnp.floor((cnt1 + cnt2 + (MOE_CHUNK - 1.0)) * (1.0 / MOE_CHUNK)) * MOE_CHUNK
    pc_b = jnp.broadcast_to(pc, (E, 128))
    below = jnp.where(lax.broadcasted_iota(jnp.int32, (E, E), 1)
                      < lax.broadcasted_iota(jnp.int32, (E, E), 0), 1.0, 0.0)
    run_off = jnp.dot(below, pc_b, precision=HIGHEST, preferred_element_type=F32)[:, 0:1]
    meta_ref[0:1, :] = jnp.sum(oh1 * (run_off + r1), axis=0, keepdims=True)
    meta_ref[1:2, :] = jnp.sum(oh2 * (run_off + cnt1 + r2), axis=0, keepdims=True)
    meta_ref[2:3, :] = m1 / den
    meta_ref[3:4, :] = m2 / den
    meta_ref[4:8, :] = jnp.zeros((4, tm), F32)
    pc_ref[...] = pc_b


def _route(x2d, rwt, rbc):
    N, D = x2d.shape
    E = rwt.shape[0]
    nt = N // MOE_TILE
    full = lambda a: pl.BlockSpec(a.shape, lambda i: (0,) * a.ndim)
    return pl.pallas_call(
        functools.partial(_route_kernel, n_experts=E),
        grid=(nt,),
        in_specs=[pl.BlockSpec((MOE_TILE, D), lambda i: (i, 0)), full(rwt), full(rbc)],
        out_specs=(pl.BlockSpec((None, 8, MOE_TILE), lambda i: (i, 0, 0)),
                   pl.BlockSpec((None, E, 128), lambda i: (i, 0, 0))),
        out_shape=(jax.ShapeDtypeStruct((nt, 8, MOE_TILE), F32), jax.ShapeDtypeStruct((nt, E, 128), F32)),
        compiler_params=pltpu.CompilerParams(dimension_semantics=("arbitrary",),
                                             vmem_limit_bytes=VMEM_LIMIT_BYTES),
    )(x2d, rwt, rbc)


def _moe_schedule(pc, n_blocks_max):
    nt, E = pc.shape
    cum = jnp.cumsum(pc, axis=1)
    run_off = cum - pc
    total = cum[:, -1]
    seg_len = jnp.sum(pc, axis=0)
    seg_pad = ((seg_len + MOE_BLOCK - 1) // MOE_BLOCK) * MOE_BLOCK
    seg_end = jnp.cumsum(seg_pad)
    goff = (seg_end - seg_pad)[None, :] + (jnp.cumsum(pc, axis=0) - pc)
    r = jnp.arange(MOE_SLOTS // MOE_CHUNK, dtype=jnp.int32) * MOE_CHUNK
    e_c = jnp.minimum(jnp.sum((cum[:, None, :] <= r[None, :, None]).astype(jnp.int32), axis=-1), E - 1)
    pick = e_c[:, :, None] == jnp.arange(E, dtype=jnp.int32)[None, None, :]
    dst_row = r[None, :] + jnp.sum(jnp.where(pick, (goff - run_off)[:, None, :], 0), axis=-1)
    dst_chunk = jnp.where(r[None, :] < total[:, None], dst_row // MOE_CHUNK, -1).astype(jnp.int32)
    blk_start = jnp.arange(n_blocks_max, dtype=jnp.int32) * MOE_BLOCK
    block_expert = jnp.minimum(jnp.sum((seg_end[None, :] <= blk_start[:, None]).astype(jnp.int32), axis=-1), E - 1)
    n_valid = (seg_end[-1] // MOE_BLOCK).reshape(1).astype(jnp.int32)
    n_chunks_all = n_blocks_max * (MOE_BLOCK // MOE_CHUNK)
    n_fill_max = n_chunks_all - (nt * 2 * MOE_TILE) // MOE_CHUNK
    gap_start = jnp.concatenate([seg_end - seg_pad + seg_len, seg_end[-1:]]) // MOE_CHUNK
    gap_len = jnp.concatenate([seg_pad - seg_len, n_chunks_all * MOE_CHUNK - seg_end[-1:]]) // MOE_CHUNK
    gap_cum = jnp.cumsum(gap_len)
    k = jnp.arange(n_fill_max, dtype=jnp.int32)
    which = jnp.minimum(jnp.sum((gap_cum[None, :] <= k[:, None]).astype(jnp.int32), axis=-1), E)
    pick_gap = which[:, None] == jnp.arange(E + 1, dtype=jnp.int32)[None, :]
    fill = k + jnp.sum(jnp.where(pick_gap, (gap_start - (gap_cum - gap_len))[None, :], 0), axis=-1)
    fill = jnp.where(k < gap_cum[-1], fill, 0).astype(jnp.int32)
    n_fill = gap_cum[-1].reshape(1).astype(jnp.int32)
    return (dst_chunk.reshape(-1), total.astype(jnp.int32), block_expert.astype(jnp.int32), n_valid, fill, n_fill)


def _chunk_copy(hbm_ref, buf, sem, b, c, d, to_hbm):
    local = buf.at[b, pl.ds(c * MOE_CHUNK, MOE_CHUNK)]
    remote = hbm_ref.at[pl.ds(pl.multiple_of(d * MOE_CHUNK, MOE_CHUNK), MOE_CHUNK)]
    if to_hbm:
        return pltpu.make_async_copy(local, remote, sem.at[b])
    return pltpu.make_async_copy(remote, local, sem.at[b])


def _for_tile_chunks(dst_ref, tile, fn):
    n_chunks = MOE_SLOTS // MOE_CHUNK
    for c in range(n_chunks):
        d = dst_ref[tile * n_chunks + c]

        @pl.when(d >= 0)
        def _():
            fn(c, d)


def _dispatch_kernel(dst_ref, fill_ref, nfill_ref, x_ref, meta_ref, xs_hbm, buf, sem, zbuf, zsem):
    i = pl.program_id(0)
    n = pl.num_programs(0)
    b = i % 2
    tm = x_ref.shape[0]

    def wait_tile(tile, slot):
        _for_tile_chunks(dst_ref, tile, lambda c, d: _chunk_copy(xs_hbm, buf, sem, slot, c, d, True).wait())

    def zero_copy(k):
        d = fill_ref[k]
        return pltpu.make_async_copy(
            zbuf, xs_hbm.at[pl.ds(pl.multiple_of(d * MOE_CHUNK, MOE_CHUNK), MOE_CHUNK)], zsem.at[0])

    @pl.when(i == 0)
    def _():
        zbuf[...] = jnp.zeros_like(zbuf)
        lax.fori_loop(0, nfill_ref[0], lambda k, c: (zero_copy(k).start(), c)[1], 0)

    @pl.when(i >= 2)
    def _():
        wait_tile(i - 2, b)

    s1 = meta_ref[0:1, :].astype(jnp.int32)
    s2 = meta_ref[1:2, :].astype(jnp.int32)
    rows = lax.broadcasted_iota(jnp.int32, (MOE_SLOTS, tm), 0)
    dsp = jnp.where((rows == s1) | (rows == s2), 1.0, 0.0).astype(BF16)
    buf[b] = _dot(dsp, x_ref[...].astype(BF16)).astype(BF16)
    _for_tile_chunks(dst_ref, i, lambda c, d: _chunk_copy(xs_hbm, buf, sem, b, c, d, True).start())

    @pl.when(i == n - 1)
    def _():
        wait_tile(i, b)

    @pl.when((i == n - 1) & (i >= 1))
    def _():
        wait_tile(i - 1, 1 - b)

    @pl.when(i == n - 1)
    def _():
        lax.fori_loop(0, nfill_ref[0], lambda k, c: (zero_copy(k).wait(), c)[1], 0)


def _dispatch(x2d, meta, dst_chunk, fill, n_fill, n_rows):
    N, D = x2d.shape
    nt = N // MOE_TILE
    return pl.pallas_call(
        _dispatch_kernel,
        grid_spec=pltpu.PrefetchScalarGridSpec(
            num_scalar_prefetch=3, grid=(nt,),
            in_specs=[pl.BlockSpec((MOE_TILE, D), lambda i, *_: (i, 0)),
                      pl.BlockSpec((None, 8, MOE_TILE), lambda i, *_: (i, 0, 0))],
            out_specs=pl.BlockSpec(memory_space=pl.ANY),
            scratch_shapes=[pltpu.VMEM((2, MOE_SLOTS, D), BF16), pltpu.SemaphoreType.DMA((2,)),
                            pltpu.VMEM((MOE_CHUNK, D), BF16), pltpu.SemaphoreType.DMA((1,))]),
        out_shape=jax.ShapeDtypeStruct((n_rows, D), BF16),
        compiler_params=pltpu.CompilerParams(dimension_semantics=("arbitrary",),
                                             vmem_limit_bytes=VMEM_LIMIT_BYTES),
    )(dst_chunk, fill, n_fill, x2d, meta)


def _ffn_kernel(be_ref, nv_ref, xs_ref, wg_ref, wu_ref, wd_ref, ys_ref):
    del be_ref
    used = pl.program_id(0) < nv_ref[0]

    @pl.when(used)
    def _():
        xb = xs_ref[...]
        g = _dot(xb, wg_ref[...])
        up = _dot(xb, wu_ref[...])
        hmid = (g * _sigmoid(g) * up).astype(BF16)
        ys_ref[...] = _dot(hmid, wd_ref[...]).astype(ys_ref.dtype)

    @pl.when(jnp.logical_not(used))
    def _():
        ys_ref[...] = jnp.zeros_like(ys_ref)


def _ffn(xs, wg, wu, wd, block_expert, n_valid):
    R, D = xs.shape
    E, _, De = wg.shape
    last = lambda j, nv: jnp.maximum(jnp.minimum(j, nv[0] - 1), 0)
    return pl.pallas_call(
        _ffn_kernel,
        grid_spec=pltpu.PrefetchScalarGridSpec(
            num_scalar_prefetch=2, grid=(R // MOE_BLOCK,),
            in_specs=[pl.BlockSpec((MOE_BLOCK, D), lambda j, be, nv: (last(j, nv), 0)),
                      pl.BlockSpec((None, D, De), lambda j, be, nv: (be[last(j, nv)], 0, 0)),
                      pl.BlockSpec((None, D, De), lambda j, be, nv: (be[last(j, nv)], 0, 0)),
                      pl.BlockSpec((None, De, D), lambda j, be, nv: (be[last(j, nv)], 0, 0))],
            out_specs=pl.BlockSpec((MOE_BLOCK, D), lambda j, be, nv: (j, 0))),
        out_shape=jax.ShapeDtypeStruct((R, D), BF16),
        compiler_params=pltpu.CompilerParams(dimension_semantics=("arbitrary",),
                                             vmem_limit_bytes=VMEM_LIMIT_BYTES),
    )(block_expert, n_valid, xs, wg, wu, wd)


def _combine_kernel(dst_ref, tot_ref, x_ref, metac_ref, ys_hbm, ln_g_ref, ln_b_ref, o_ref, buf, sem):
    i = pl.program_id(0)
    n = pl.num_programs(0)
    b = i % 2
    tm = x_ref.shape[0]

    def fetch(tile, slot):
        _for_tile_chunks(dst_ref, tile, lambda c, d: _chunk_copy(ys_hbm, buf, sem, slot, c, d, False).start())

    @pl.when(i == 0)
    def _():
        fetch(0, 0)

    @pl.when(i + 1 < n)
    def _():
        fetch(i + 1, 1 - b)

    _for_tile_chunks(dst_ref, i, lambda c, d: _chunk_copy(ys_hbm, buf, sem, b, c, d, False).wait())
    live = lax.broadcasted_iota(jnp.int32, (MOE_SLOTS, 1), 0) < tot_ref[i]
    ys = jnp.where(live, buf[b], jnp.zeros((), BF16))
    mc = metac_ref[...]
    s1 = mc[:, 0:1].astype(jnp.int32)
    s2 = mc[:, 1:2].astype(jnp.int32)
    lane = lax.broadcasted_iota(jnp.int32, (tm, MOE_SLOTS), 1)
    cmb = (jnp.where(lane == s1, mc[:, 2:3], 0.0) + jnp.where(lane == s2, mc[:, 3:4], 0.0)).astype(BF16)
    o_ref[...] = _layer_norm(ALPHA * x_ref[...] + _dot(cmb, ys), ln_g_ref[...], ln_b_ref[...])


def _combine(x2d, meta_col, ys, dst_chunk, total, ln_g, ln_b):
    N, D = x2d.shape
    nt = N // MOE_TILE
    return pl.pallas_call(
        _combine_kernel,
        grid_spec=pltpu.PrefetchScalarGridSpec(
            num_scalar_prefetch=2, grid=(nt,),
            in_specs=[pl.BlockSpec((MOE_TILE, D), lambda i, dst, tot: (i, 0)),
                      pl.BlockSpec((MOE_TILE, 8), lambda i, dst, tot: (i, 0)),
                      pl.BlockSpec(memory_space=pl.ANY),
                      pl.BlockSpec(ln_g.shape, lambda i, dst, tot: (0, 0)),
                      pl.BlockSpec(ln_b.shape, lambda i, dst, tot: (0, 0))],
            out_specs=pl.BlockSpec((MOE_TILE, D), lambda i, dst, tot: (i, 0)),
            scratch_shapes=[pltpu.VMEM((2, MOE_SLOTS, D), BF16), pltpu.SemaphoreType.DMA((2,))]),
        out_shape=jax.ShapeDtypeStruct((N, D), F32),
        compiler_params=pltpu.CompilerParams(dimension_semantics=("arbitrary",),
                                             vmem_limit_bytes=VMEM_LIMIT_BYTES),
    )(dst_chunk, total, x2d, meta_col, ys, ln_g, ln_b)


def _moe_sparse(x2d, rw, rb, wg, wu, wd, ln_g, ln_b):
    N, D = x2d.shape
    E = rw.shape[1]
    nt = N // MOE_TILE
    meta, pc = _route(x2d, rw.T, rb.reshape(E, 1))
    pc = pc[:, :, 0].astype(jnp.int32)
    n_rows = nt * MOE_SLOTS + E * MOE_BLOCK
    assert n_rows % MOE_BLOCK == 0
    dst_chunk, total, block_expert, n_valid, fill, n_fill = _moe_schedule(pc, n_rows // MOE_BLOCK)
    xs = _dispatch(x2d, meta, dst_chunk, fill, n_fill, n_rows)
    ys = _ffn(xs, wg, wu, wd, block_expert, n_valid)
    meta_col = jnp.swapaxes(meta, 1, 2).reshape(N, 8)
    return _combine(x2d, meta_col, ys, dst_chunk, total, ln_g, ln_b)


def _s5_weights_kernel(lam_col_ref, lam_row_ref, bt_ref, ct_ref, d_ref,
                       toe_ref, pst_ref, qst_ref, a_pow_ref):
    L, Cg = S5_CHUNK, S5_GROUP
    LC = L * Cg
    P = lam_row_ref.shape[-1]

    def discretise(a_re, a_im, log_dt):
        dt = jnp.exp(log_dt)
        lr, li = dt * a_re, dt * a_im
        er = jnp.exp(lr)
        abr, abi = er * jnp.cos(li), er * jnp.sin(li)
        nr, ni = abr - 1.0, abi
        den = a_re * a_re + a_im * a_im
        cr = (nr * a_re + ni * a_im) / den
        ci = (ni * a_re - nr * a_im) / den
        return lr, li, cr, ci

    def a_power(lr, li, tau):
        er = jnp.exp(tau * lr)
        return er * jnp.cos(tau * li), er * jnp.sin(tau * li)

    lc = lam_col_ref[...]
    lr_c, li_c, _, _ = discretise(lc[:, 0:1], lc[:, 1:2], lc[:, 2:3])
    lw = lam_row_ref[...]
    lr_r, li_r, cr_r, ci_r = discretise(lw[0:1, :], lw[1:2, :], lw[2:3, :])

    btr, bti = bt_ref[0], bt_ref[1]
    bbr = cr_r * btr - ci_r * bti
    bbi = cr_r * bti + ci_r * btr

    ctr, cti = ct_ref[0], ct_ref[1]
    tau_l = (lax.broadcasted_iota(jnp.int32, (P, LC), 1) // Cg).astype(F32)

    pr, pi = a_power(lr_c, li_c, tau_l)
    car = ctr * pr - cti * pi
    cai = ctr * pi + cti * pr
    t0 = (jnp.dot(bbr[0:Cg, :], car, preferred_element_type=F32, precision=HIGHEST)
          - jnp.dot(bbi[0:Cg, :], cai, preferred_element_type=F32, precision=HIGHEST))
    r16 = lax.broadcasted_iota(jnp.int32, (Cg, LC), 0)
    l16 = lax.broadcasted_iota(jnp.int32, (Cg, LC), 1)
    t0 = t0 + jnp.where(r16 == l16, d_ref[...], 0.0)
    toe_ref[0:Cg, :] = t0.astype(toe_ref.dtype)
    for s in range(1, L):
        toe_ref[s * Cg:(s + 1) * Cg, :] = jnp.where(
            l16 >= s * Cg, pltpu.roll(t0, s * Cg, axis=1), 0.0).astype(toe_ref.dtype)

    pr, pi = a_power(lr_c, li_c, tau_l + 1.0)
    qst_ref[0:P, :] = (ctr * pr - cti * pi).astype(qst_ref.dtype)
    qst_ref[P:2 * P, :] = (-(ctr * pi + cti * pr)).astype(qst_ref.dtype)

    tau_r = (L - 1 - lax.broadcasted_iota(jnp.int32, (LC, P), 0) // Cg).astype(F32)
    pr, pi = a_power(lr_r, li_r, tau_r)
    pst_ref[:, 0:P] = (bbr * pr - bbi * pi).astype(pst_ref.dtype)
    pst_ref[:, P:2 * P] = (bbr * pi + bbi * pr).astype(pst_ref.dtype)

    pr, pi = a_power(lr_r, li_r, float(L))
    a_pow_ref[0:1, :] = pr
    a_pow_ref[1:2, :] = pi


def _s5_weights(a_re, a_im, log_dt, b_re, b_im, c_re, c_im, d):
    G, P = a_re.shape
    Cg, L = S5_GROUP, S5_CHUNK
    LC = L * Cg
    lam = jnp.stack([a_re, a_im, jnp.broadcast_to(log_dt[:, None], (G, P)), jnp.zeros((G, P), F32)], axis=1)
    lam_row = lam
    lam_col = jnp.swapaxes(lam, 1, 2)
    bt = jnp.stack([jnp.swapaxes(b_re, 1, 2), jnp.swapaxes(b_im, 1, 2)], axis=1)
    bt = jnp.tile(bt, (1, 1, L, 1))
    ct = jnp.stack([jnp.swapaxes(c_re, 1, 2), jnp.swapaxes(c_im, 1, 2)], axis=1)
    ct = jnp.tile(ct, (1, 1, 1, L))
    d_col = d.reshape(G, Cg, 1)
    blk = lambda shp: pl.BlockSpec((None,) + shp, lambda g: (g,) + (0,) * len(shp))
    return pl.pallas_call(
        _s5_weights_kernel,
        grid=(G,),
        in_specs=[blk((P, 4)), blk((4, P)), blk((2, LC, P)), blk((2, P, LC)), blk((Cg, 1))],
        out_specs=(blk((LC, LC)), blk((LC, 2 * P)), blk((2 * P, LC)), blk((2, P))),
        out_shape=(jax.ShapeDtypeStruct((G, LC, LC), BF16),
                   jax.ShapeDtypeStruct((G, LC, 2 * P), BF16),
                   jax.ShapeDtypeStruct((G, 2 * P, LC), BF16),
                   jax.ShapeDtypeStruct((G, 2, P), F32)),
        compiler_params=pltpu.CompilerParams(dimension_semantics=("arbitrary",)),
    )(lam_col, lam_row, bt, ct, d_col)


def _granule_masks(rows):
    granule = lax.broadcasted_iota(jnp.int32, (rows, 128), 1) // S5_GROUP
    return [granule == q for q in range(128 // S5_GROUP)]


def _s5_in_kernel(x_ref, w_ref, o_ref, u_scr):
    B, tt, D = x_ref.shape
    L, Cg = S5_CHUNK, S5_GROUP
    gpb = 128 // Cg
    cpb = tt // L
    u = _dot(x_ref[...].reshape(B * tt, D).astype(BF16), w_ref[...])
    for j in range(D // 128):
        u_scr[j] = u[:, 128 * j:128 * (j + 1)]
    masks = _granule_masks(cpb * B)
    for j in range(D // 128):
        for h in range(L // gpb):
            pieces = [jnp.concatenate([u_scr[j, pl.ds(c * L + h * gpb + q, B, stride=tt), :]
                                       for c in range(cpb)], axis=0) for q in range(gpb)]
            for gg in range(gpb):
                dest = None
                for q in range(gpb):
                    moved = pieces[q] if q == gg else pltpu.roll(pieces[q], ((q - gg) % gpb) * Cg, axis=1)
                    dest = moved if dest is None else jnp.where(masks[q], moved, dest)
                o_ref[gpb * j + gg, :, 128 * h:128 * (h + 1)] = dest.astype(o_ref.dtype)


def _s5_in(x, w, tt):
    B, T, D = x.shape
    G = D // S5_GROUP
    LC = S5_CHUNK * S5_GROUP
    rows = (tt // S5_CHUNK) * B
    return pl.pallas_call(
        _s5_in_kernel,
        grid=(T // tt,),
        in_specs=[pl.BlockSpec((B, tt, D), lambda i: (0, i, 0)), pl.BlockSpec(w.shape, lambda i: (0, 0))],
        out_specs=pl.BlockSpec((G, rows, LC), lambda i: (0, i, 0)),
        out_shape=jax.ShapeDtypeStruct((G, (T // S5_CHUNK) * B, LC), BF16),
        scratch_shapes=[pltpu.VMEM((D // 128, B * tt, 128), F32)],
        compiler_params=pltpu.CompilerParams(dimension_semantics=("arbitrary",),
                                             vmem_limit_bytes=VMEM_LIMIT_BYTES),
    )(x, w)


def _s5_scan_kernel(x_ref, s0_ref, toe_ref, pst_ref, qst_ref, apow_ref, y_ref, sfin_ref,
                    zr_scr, zi_scr, sr_scr, si_scr, *, n_chunks, batch):
    P = apow_ref.shape[-1]
    xg = x_ref[...]
    z = _dot(xg, pst_ref[...])
    zr_scr[...] = z[:, 0:P]
    zi_scr[...] = z[:, P:2 * P]
    ar = apow_ref[0:1, :]
    ai = apow_ref[1:2, :]

    def step(n, carry):
        sr, si = carry
        rows = pl.ds(pl.multiple_of(n * batch, batch), batch)
        sr_scr[rows, :] = sr
        si_scr[rows, :] = si
        return (ar * sr - ai * si + zr_scr[rows, :], ar * si + ai * sr + zi_scr[rows, :])

    sr, si = lax.fori_loop(0, n_chunks, step, (s0_ref[0], s0_ref[1]), unroll=4)
    sfin_ref[0] = sr
    sfin_ref[1] = si
    y = (_dot(xg, toe_ref[...]) + _dot(sr_scr[...].astype(BF16), qst_ref[0:P, :])
         + _dot(si_scr[...].astype(BF16), qst_ref[P:2 * P, :]))
    y_ref[...] = y.astype(y_ref.dtype)


def _s5_scan(xg, s0, toe, pst, qst, apow, n_chunks, batch):
    G, R, LC = xg.shape
    P = apow.shape[-1]
    blk = lambda shp: pl.BlockSpec((None,) + shp, lambda g: (g,) + (0,) * len(shp))
    return pl.pallas_call(
        functools.partial(_s5_scan_kernel, n_chunks=n_chunks, batch=batch),
        grid=(G,),
        in_specs=[blk((R, LC)), blk((2, batch, P)), blk((LC, LC)), blk((LC, 2 * P)), blk((2 * P, LC)),
                  blk((2, P))],
        out_specs=(blk((R, LC)), blk((2, batch, P))),
        out_shape=(jax.ShapeDtypeStruct((G, R, LC), BF16), jax.ShapeDtypeStruct((G, 2, batch, P), F32)),
        scratch_shapes=[pltpu.VMEM((R, P), F32)] * 4,
        compiler_params=pltpu.CompilerParams(dimension_semantics=("arbitrary",),
                                             vmem_limit_bytes=VMEM_LIMIT_BYTES),
    )(xg, s0, toe, pst, qst, apow)


def _s5_out_kernel(x_ref, yg_ref, w_ref, ln_g_ref, ln_b_ref, o_ref, y_scr):
    B, tt, D = x_ref.shape
    L, Cg = S5_CHUNK, S5_GROUP
    gpb = 128 // Cg
    cpb = tt // L
    masks = _granule_masks(cpb * B)
    for j in range(D // 128):
        for h in range(L // gpb):
            srcs =[yg_ref[gpb * j + gg, :, 128 * h:128 * (h + 1)].astype(F32) for gg in range(gpb)]
            for q in range(gpb):
                dest = None
                for gg in range(gpb):
                    moved = srcs[gg] if gg == q else pltpu.roll(srcs[gg], ((gg - q) % gpb) * Cg, axis=1)
                    dest = moved if dest is None else jnp.where(masks[gg], moved, dest)
                for c in range(cpb):
                    y_scr[j, pl.ds(c * L + h * gpb + q, B, stride=tt), :] = dest[c * B:(c + 1) * B, :]
    y = jnp.concatenate([y_scr[j] for j in range(D // 128)], axis=1)
    act = _gelu_tanh(y).astype(BF16)
    zz = _dot(act, w_ref[...])
    mix = zz[:, :D] * _sigmoid(zz[:, D:])
    out = _layer_norm(ALPHA * x_ref[...].reshape(B * tt, D) + mix, ln_g_ref[...], ln_b_ref[...])
    o_ref[...] = out.reshape(B, tt, D)


def _s5_out(x, yg, w_out, ln_g, ln_b, tt):
    B, T, D = x.shape
    G, _, LC = yg.shape
    rows = (tt // S5_CHUNK) * B
    full = lambda a: pl.BlockSpec(a.shape, lambda i: (0,) * a.ndim)
    return pl.pallas_call(
        _s5_out_kernel,
        grid=(T // tt,),
        in_specs=[pl.BlockSpec((B, tt, D), lambda i: (0, i, 0)), pl.BlockSpec((G, rows, LC), lambda i: (0, i, 0)),
                  full(w_out), full(ln_g), full(ln_b)],
        out_specs=pl.BlockSpec((B, tt, D), lambda i: (0, i, 0)),
        out_shape=jax.ShapeDtypeStruct((B, T, D), F32),
        scratch_shapes=[pltpu.VMEM((D // 128, B * tt, 128), F32)],
        compiler_params=pltpu.CompilerParams(dimension_semantics=("arbitrary",),
                                             vmem_limit_bytes=VMEM_LIMIT_BYTES),
    )(x, yg, w_out, ln_g, ln_b)


def _s5_mixer(x, s0_re, s0_im, w_in, w_out, s5w, ln_g, ln_b):
    B, T, D = x.shape
    toe, pst, qst, apow = s5w
    xg = _s5_in(x, w_in, S5_STEP_TOKENS)
    s0 = jnp.stack([s0_re, s0_im], axis=0).transpose(2, 0, 1, 3)
    yg, sfin = _s5_scan(xg, s0, toe, pst, qst, apow, T // S5_CHUNK, B)
    out = _s5_out(x, yg, w_out, ln_g, ln_b, S5_STEP_TOKENS)
    sfin = sfin.transpose(1, 2, 0, 3)
    return out, sfin[0], sfin[1]


def _block_diag(w):
    H, d, _ = w.shape
    eye = jnp.eye(H, dtype=w.dtype)
    return (eye[:, None, :, None] * w[:, :, None, :]).reshape(H * d, H * d)


def kernel(x_prompt, x_sample, state_lru_h, state_lru_conv, state_hgrn2, state_s5_re, state_s5_im, ln_mix_g, ln_mix_b, ln_ffn_g, ln_ffn_b, w_in_ab, w_out_ab, lru_conv_w, lru_conv_b, lru_w_r, lru_b_r, lru_w_i, lru_b_i, lru_lam, hgrn_lb_logits, hgrn_norm_g, w_in_s5, w_out_s5, s5_a_re, s5_a_im, s5_b_re, s5_b_im, s5_c_re, s5_c_im, s5_d, s5_log_dt, router_w, router_b, moe_w_gate, moe_w_up, moe_w_down):
    Bp, Tp, D = x_prompt.shape
    Bs, Ts, _ = x_sample.shape
    W = state_lru_h.shape[-1]
    dk = W // HG_HEADS
    row = lambda v: v.reshape(1, -1)

    def moe(y, l, sparse):
        fn = _moe_sparse if sparse else functools.partial(_moe_dense, tm=512)
        out = fn(y.reshape(-1, D), router_w, row(router_b), moe_w_gate[l].astype(BF16),
                 moe_w_up[l].astype(BF16), moe_w_down[l].astype(BF16), row(ln_ffn_g[l]), row(ln_ffn_b[l]))
        return out.reshape(y.shape)

    j = 0
    wts = (w_in_ab[j].astype(BF16), w_out_ab[j].astype(BF16), lru_conv_w[j], row(lru_conv_b[j]),
           _block_diag(lru_w_r[j]).astype(BF16), row(lru_b_r[j]),
           _block_diag(lru_w_i[j]).astype(BF16), row(lru_b_i[j]),
           row(lru_lam[j]), hgrn_lb_logits, row(hgrn_norm_g[j]), row(ln_mix_g[0]), row(ln_mix_b[0]))
    yp, p_h, p_conv, p_hg = _ab_mixer(
        x_prompt, jnp.zeros((Bp, 1, W), F32), jnp.zeros((Bp, 3, W), F32),
        jnp.zeros((Bp, HG_HEADS, dk, dk), F32), wts, 256)
    ys, s_h, s_conv, s_hg = _ab_mixer(
        x_sample, state_lru_h[j][:, None, :], state_lru_conv[j], state_hgrn2[j], wts, Ts)
    yp = moe(yp, 0, True)
    ys = moe(ys, 0, False)

    s5w = _s5_weights(s5_a_re[j], s5_a_im[j], s5_log_dt[j], s5_b_re[j], s5_b_im[j],
                      s5_c_re[j], s5_c_im[j], s5_d[j])
    G, P = s5_a_re[j].shape
    w_in5 = w_in_s5[j].astype(BF16)
    w_out5 = w_out_s5[j].astype(BF16)
    zp = jnp.zeros((Bp, G, P), F32)
    yp, p_re, p_im = _s5_mixer(yp, zp, zp, w_in5, w_out5, s5w, row(ln_mix_g[1]), row(ln_mix_b[1]))
    ys, s_re, s_im = _s5_mixer(ys, state_s5_re[j], state_s5_im[j], w_in5, w_out5, s5w,
                               row(ln_mix_g[1]), row(ln_mix_b[1]))
    yp = moe(yp, 1, True)
    ys = moe(ys, 1, False)

    return (yp, ys,
            p_h.reshape(1, Bp, W), p_conv[None], p_hg[None], p_re[None], p_im[None],
            s_h.reshape(1, Bs, W), s_conv[None], s_hg[None], s_re[None], s_im[None])
```

```python
import functools
import math

import jax
import jax.numpy as jnp
from jax import lax
from jax.experimental import pallas as pl
from jax.experimental.pallas import tpu as pltpu

F32 = jnp.float32
BF16 = jnp.bfloat16
HIGHEST = lax.Precision.HIGHEST

DEPTH = 2
RG_C = 8.0
LRU_HEADS = 8
HG_HEADS = 4
AB_TILE = 256
AB_SEQS = 2
HG_CHUNK = 16
RMS_EPS = 1e-6
S5_GROUP = 16
S5_STEP_TOKENS = 32
S5_CHUNK = 16
N_EXPERT_GROUPS = 4
ALPHA = (2.0 * DEPTH) ** 0.25
LN_EPS = 1e-5
MOE_TILE = 256
MOE_CHUNK = 16
MOE_SLOTS = 768
MOE_BLOCK = 512
VMEM_LIMIT_BYTES = 56 * 1024 * 1024


def _dot(a, b):
    return jnp.dot(a, b, preferred_element_type=F32)


def _dot_nt(a, b):
    return lax.dot_general(a, b, (((1,), (1,)), ((), ())), preferred_element_type=F32)


def _sigmoid(x):
    return 1.0 / (1.0 + jnp.exp(-x))


def _gelu_tanh(x):
    c = math.sqrt(2.0 / math.pi)
    return 0.5 * x * (1.0 + jnp.tanh(c * (x + 0.044715 * (x * x * x))))


def _layer_norm(s, g, b):
    mu = jnp.mean(s, axis=-1, keepdims=True)
    d = s - mu
    var = jnp.mean(d * d, axis=-1, keepdims=True)
    return d * lax.rsqrt(var + LN_EPS) * g + b


def _shift_rows(x, s, row):
    del row
    return pltpu.roll(x, s, axis=0)


def _ab_kernel(x_ref, h0_ref, conv0_ref, s0_ref, w_in_ref, w_out_ref, conv_w_ref, conv_b_ref,
               wr_ref, br_ref, wi_ref, bi_ref, lam_ref, lb_logits_ref, norm_g_ref, ln_g_ref, ln_b_ref,
               y_ref, h_out_ref, conv_out_ref, s_out_ref,
               zx_buf, h_carry, st_carry, st_all, *, tb, width, nb):
    for si in range(nb):
        _ab_sequence_tile(si, x_ref, h0_ref, conv0_ref, s0_ref, w_in_ref, w_out_ref, conv_w_ref, conv_b_ref,
                          wr_ref, br_ref, wi_ref, bi_ref, lam_ref, lb_logits_ref, norm_g_ref, ln_g_ref, ln_b_ref,
                          y_ref, h_out_ref, conv_out_ref, s_out_ref,
                          zx_buf.at[si], h_carry.at[si], st_carry.at[si], st_all.at[si], tb, width)


def _ab_sequence_tile(si, x_ref, h0_ref, conv0_ref, s0_ref, w_in_ref, w_out_ref, conv_w_ref, conv_b_ref,
                      wr_ref, br_ref, wi_ref, bi_ref, lam_ref, lb_logits_ref, norm_g_ref, ln_g_ref, ln_b_ref,
                      y_ref, h_out_ref, conv_out_ref, s_out_ref, zx_buf, h_carry, st_carry, st_all, tb, width):
    t = pl.program_id(1)
    nt = pl.num_programs(1)
    W = width
    L = HG_CHUNK
    nc = tb // L
    dk = W // HG_HEADS

    @pl.when(t == 0)
    def _init():
        zx_buf[0:8, :] = jnp.zeros((8, W), F32)
        zx_buf[5:8, :] = conv0_ref[si]
        h_carry[...] = jnp.broadcast_to(h0_ref[si], h_carry.shape)
        for hd in range(HG_HEADS):
            st_carry[hd] = s0_ref[si, hd].T

    x = x_ref[si]
    z = _dot(x.astype(BF16), w_in_ref[...])
    z_gate = z[:, 0 * W:1 * W]
    z_x = z[:, 1 * W:2 * W]
    z_q = z[:, 2 * W:3 * W]
    z_f = z[:, 3 * W:4 * W]
    z_i = z[:, 4 * W:5 * W]
    z_g = z[:, 5 * W:6 * W]

    zx_buf[8:8 + tb, :] = z_x
    cw = conv_w_ref[...]
    u = (conv_b_ref[...] + cw[3:4, :] * z_x
         + cw[2:3, :] * zx_buf[7:7 + tb, :]
         + cw[1:2, :] * zx_buf[6:6 + tb, :]
         + cw[0:1, :] * zx_buf[5:5 + tb, :])
    tail = zx_buf[tb:tb + 8, :]
    zx_buf[0:8, :] = tail

    ub = u.astype(BF16)
    r = _sigmoid(_dot(ub, wr_ref[...]) + br_ref[...])
    ig = _sigmoid(_dot(ub, wi_ref[...]) + bi_ref[...])
    nlam = -lam_ref[...]
    softplus_nlam = jnp.maximum(nlam, 0.0) + jnp.log1p(jnp.exp(-jnp.abs(nlam)))
    log_a = (-RG_C) * r * softplus_nlam
    a = jnp.exp(log_a)
    th = jnp.tanh(log_a)
    b = jnp.sqrt(-2.0 * th / (1.0 - th)) * ig * u

    row = lax.broadcasted_iota(jnp.int32, (tb, W), 0)
    rin8 = row % 8
    s = 1
    while s < 8:
        m = rin8 >= s
        a_sh = _shift_rows(a, s, row)
        b_sh = _shift_rows(b, s, row)
        b = jnp.where(m, a * b_sh + b, b)
        a = jnp.where(m, a * a_sh, a)
        s *= 2
    a3 = a.reshape(tb // 8, 8, W)
    b3 = b.reshape(tb // 8, 8, W)
    h_prev = h_carry[0:1, :]
    h_groups = []
    for g in range(tb // 8):
        hg = b3[g] + a3[g] * h_prev
        h_groups.append(hg)
        h_prev = hg[7:8, :]
    h = jnp.concatenate(h_groups, axis=0)
    h_carry[...] = jnp.broadcast_to(h_prev, h_carry.shape)
    out_a = h * _gelu_tanh(z_gate)

    lg = lb_logits_ref[...]
    e = jnp.exp(lg - jnp.max(lg, axis=0, keepdims=True))
    lb = e[0:1, :] / jnp.sum(e, axis=0, keepdims=True)
    f = lb + (1.0 - lb) * _sigmoid(z_f)
    log_f = jnp.log(f)
    kk = 1.0 - f
    rin = row % L
    G = log_f
    s = 1
    while s < L:
        G = G + jnp.where(rin >= s, _shift_rows(G, s, row), 0.0)
        s *= 2
    G3 = G.reshape(nc, L, W)
    G_last = G3[:, L - 1:L, :]
    q_dec = (z_q * jnp.exp(G)).reshape(nc, L, W).astype(BF16)
    k_inc = (kk * jnp.exp(-G)).reshape(nc, L, W).astype(BF16)
    k_tail = (kk.reshape(nc, L, W) * jnp.exp(G_last - G3)).astype(BF16)
    decay = jnp.exp(G_last)
    v3 = z_i.reshape(nc, L, W).astype(BF16)
    causal = (lax.broadcasted_iota(jnp.int32, (nc, L, L), 1)
              >= lax.broadcasted_iota(jnp.int32, (nc, L, L), 2))

    o_heads = []
    for hd in range(HG_HEADS):
        sl = slice(hd * dk, (hd + 1) * dk)
        qd, ki, kt, vv = q_dec[:, :, sl], k_inc[:, :, sl], k_tail[:, :, sl], v3[:, :, sl]
        scores = jnp.einsum('ntk,nsk->nts', qd, ki, preferred_element_type=F32)
        scores = jnp.where(causal, scores, 0.0).astype(BF16)
        o_intra = jnp.einsum('nts,nsv->ntv', scores, vv, preferred_element_type=F32)
        vt = jnp.swapaxes(vv.astype(F32), 1, 2).astype(BF16)
        upd = jnp.einsum('nvs,nsk->nvk', vt, kt, preferred_element_type=F32)
        st = st_carry[hd]
        for n in range(nc):
            st_all[hd, n] = st.astype(BF16)
            st = st * decay[n, :, sl] + upd[n]
        st_carry[hd] = st
        o_inter = jnp.einsum('ntk,nvk->ntv', qd, st_all[hd], preferred_element_type=F32)
        o = (o_intra + o_inter).reshape(tb, dk)
        o = o * lax.rsqrt(jnp.mean(o * o, axis=-1, keepdims=True) + RMS_EPS)
        o_heads.append(o)
    o_all = jnp.concatenate(o_heads, axis=-1) * norm_g_ref[...]
    out_b = o_all * (z_g * _sigmoid(z_g))

    mixed = jnp.concatenate([out_a, out_b], axis=-1).astype(BF16)
    y = _dot(mixed, w_out_ref[...])
    y_ref[si] = _layer_norm(ALPHA * x + y, ln_g_ref[...], ln_b_ref[...])

    @pl.when(t == nt - 1)
    def _fin():
        h_out_ref[si] = h[tb - 1:tb, :]
        conv_out_ref[si] = tail[5:8, :]
        for hd in range(HG_HEADS):
            s_out_ref[si, hd] = st_carry[hd].T


def _ab_mixer(x, h0, conv0, s0, wts, tb, nb):
    B, T, D = x.shape
    W = h0.shape[-1]
    dk = W // HG_HEADS
    nt = T // tb
    nc = tb // HG_CHUNK
    (w_in, w_out, conv_w, conv_b, wr, br, wi, bi, lam, lb_logits, norm_g, ln_g, ln_b) = wts
    full = lambda a: pl.BlockSpec(a.shape, lambda b, t: (0,) * a.ndim)
    per_b = lambda shp: pl.BlockSpec((nb,) + shp, lambda b, t: (b,) + (0,) * len(shp))
    out_shape = (jax.ShapeDtypeStruct((B, T, D), F32),
                 jax.ShapeDtypeStruct((B, 1, W), F32),
                 jax.ShapeDtypeStruct((B, 3, W), F32),
                 jax.ShapeDtypeStruct((B, HG_HEADS, dk, dk), F32))
    return pl.pallas_call(
        functools.partial(_ab_kernel, tb=tb, width=W, nb=nb),
        grid=(B // nb, nt),
        in_specs=[pl.BlockSpec((nb, tb, D), lambda b, t: (b, t, 0)),
                  per_b((1, W)), per_b((3, W)), per_b((HG_HEADS, dk, dk))]
                 + [full(a) for a in wts],
        out_specs=(pl.BlockSpec((nb, tb, D), lambda b, t: (b, t, 0)),
                   per_b((1, W)), per_b((3, W)), per_b((HG_HEADS, dk, dk))),
        out_shape=out_shape,
        scratch_shapes=[pltpu.VMEM((nb, tb + 8, W), F32),
                        pltpu.VMEM((nb, 8, W), F32),
                        pltpu.VMEM((nb, HG_HEADS, dk, dk), F32),
                        pltpu.VMEM((nb, HG_HEADS, nc, dk, dk), BF16)],
        compiler_params=pltpu.CompilerParams(
            dimension_semantics=("arbitrary", "arbitrary"),
            vmem_limit_bytes=VMEM_LIMIT_BYTES),
    )(x, h0, conv0, s0, *wts)


def _router(x, rw, rb, n_experts):
    per_group = n_experts // N_EXPERT_GROUPS
    logits = jnp.dot(x, rw, preferred_element_type=F32, precision=HIGHEST) + rb
    logits = logits - jnp.max(logits, axis=-1, keepdims=True)
    ex = jnp.exp(logits)
    p = ex / jnp.sum(ex, axis=-1, keepdims=True)
    lane_i = lax.broadcasted_iota(jnp.int32, p.shape, 1)
    grp = lane_i // per_group
    lane = lane_i.astype(F32)
    big = float(n_experts)

    def top2(pm):
        m1 = jnp.max(pm, axis=-1, keepdims=True)
        i1 = jnp.min(jnp.where(pm == m1, lane, big), axis=-1, keepdims=True)
        pm2 = jnp.where(lane == i1, -1.0, pm)
        m2 = jnp.max(pm2, axis=-1, keepdims=True)
        i2 = jnp.min(jnp.where(pm2 == m2, lane, big), axis=-1, keepdims=True)
        return m1, i1, m2, i2

    best = None
    g_sel = None
    for j in range(N_EXPERT_GROUPS):
        m1, _, m2, _ = top2(jnp.where(grp == j, p, -1.0))
        score = m1 + m2
        if best is None:
            best, g_sel = score, jnp.zeros_like(lane_i[:, :1])
        else:
            better = score > best
            best = jnp.where(better, score, best)
            g_sel = jnp.where(better, j, g_sel)
    m1, i1, m2, i2 = top2(jnp.where(grp == g_sel, p, -1.0))
    den = m1 + m2
    return jnp.where(lane == i1, m1 / den, 0.0) + jnp.where(lane == i2, m2 / den, 0.0)


def _moe_dense_kernel(x_ref, rw_ref, rb_ref, wg_ref, wu_ref, wd_ref, ln_g_ref, ln_b_ref,
                      y_ref, xb_scr, comb_scr, acc_scr, *, n_experts):
    e = pl.program_id(1)

    @pl.when(e == 0)
    def _init():
        x = x_ref[...]
        xb_scr[...] = x.astype(BF16)
        comb_scr[...] = _router(x, rw_ref[...], rb_ref[...], n_experts)
        acc_scr[...] = jnp.zeros_like(acc_scr)

    comb = comb_scr[...]
    lane = lax.broadcasted_iota(jnp.int32, comb.shape, 1)
    c_e = jnp.sum(jnp.where(lane == e, comb, 0.0), axis=-1, keepdims=True)
    xb = xb_scr[...]
    g = _dot(xb, wg_ref[...].astype(BF16))
    up = _dot(xb, wu_ref[...].astype(BF16))
    hmid = (g * _sigmoid(g) * up).astype(BF16)
    acc_scr[...] += c_e * _dot(hmid, wd_ref[...].astype(BF16))

    @pl.when(e == n_experts - 1)
    def _fin():
        y_ref[...] = _layer_norm(ALPHA * x_ref[...] + acc_scr[...], ln_g_ref[...], ln_b_ref[...])


def _moe_dense(x2d, rw, rb, wg, wu, wd, layer, ln_g, ln_b, tm):
    N, D = x2d.shape
    _, E, _, De = wg.shape
    full = lambda a: pl.BlockSpec(a.shape, lambda i, e: (0,) * a.ndim)
    return pl.pallas_call(
        functools.partial(_moe_dense_kernel, n_experts=E),
        grid=(N // tm, E),
        in_specs=[pl.BlockSpec((tm, D), lambda i, e: (i, 0)), full(rw), full(rb),
                  pl.BlockSpec((None, None, D, De), lambda i, e: (layer, e, 0, 0)),
                  pl.BlockSpec((None, None, D, De), lambda i, e: (layer, e, 0, 0)),
                  pl.BlockSpec((None, None, De, D), lambda i, e: (layer, e, 0, 0)),
                  full(ln_g), full(ln_b)],
        out_specs=pl.BlockSpec((tm, D), lambda i, e: (i, 0)),
        out_shape=jax.ShapeDtypeStruct((N, D), F32),
        scratch_shapes=[pltpu.VMEM((tm, D), BF16), pltpu.VMEM((tm, E), F32), pltpu.VMEM((tm, D), F32)],
        compiler_params=pltpu.CompilerParams(
            dimension_semantics=("arbitrary", "arbitrary"),
            vmem_limit_bytes=VMEM_LIMIT_BYTES),
    )(x2d, rw, rb, wg, wu, wd, ln_g, ln_b)


def _route_kernel(x_ref, rwt_ref, rbc_ref, meta_ref, pc_ref, *, n_experts):
    E = n_experts
    per_group = E // N_EXPERT_GROUPS
    x = x_ref[...]
    tm = x.shape[0]
    logits = lax.dot_general(rwt_ref[...], x, (((1,), (1,)), ((), ())),
                             precision=HIGHEST, preferred_element_type=F32) + rbc_ref[...]
    logits = logits - jnp.max(logits, axis=0, keepdims=True)
    ex = jnp.exp(logits)
    p = ex / jnp.sum(ex, axis=0, keepdims=True)
    eid_i = lax.broadcasted_iota(jnp.int32, (E, tm), 0)
    grp = eid_i // per_group
    eid = eid_i.astype(F32)
    big = float(E)

    def top2(pm):
        m1 = jnp.max(pm, axis=0, keepdims=True)
        i1 = jnp.min(jnp.where(pm == m1, eid, big), axis=0, keepdims=True)
        pm2 = jnp.where(eid == i1, -1.0, pm)
        m2 = jnp.max(pm2, axis=0, keepdims=True)
        i2 = jnp.min(jnp.where(pm2 == m2, eid, big), axis=0, keepdims=True)
        return m1, i1, m2, i2

    best = None
    g_sel = None
    for j in range(N_EXPERT_GROUPS):
        m1, _, m2, _ = top2(jnp.where(grp == j, p, -1.0))
        score = m1 + m2
        if best is None:
            best, g_sel = score, jnp.zeros_like(eid_i[:1, :])
        else:
            better = score > best
            best = jnp.where(better, score, best)
            g_sel = jnp.where(better, j, g_sel)
    m1, i1, m2, i2 = top2(jnp.where(grp == g_sel, p, -1.0))
    den = m1 + m2

    oh1 = jnp.where(eid == i1, 1.0, 0.0)
    oh2 = jnp.where(eid == i2, 1.0, 0.0)
    earlier = jnp.where(lax.broadcasted_iota(jnp.int32, (tm, tm), 0)
                        < lax.broadcasted_iota(jnp.int32, (tm, tm), 1), 1.0, 0.0).astype(BF16)
    r1 = _dot(oh1.astype(BF16), earlier)
    r2 = _dot(oh2.astype(BF16), earlier)
    cnt1 = jnp.sum(oh1, axis=1, keepdims=True)
    cnt2 = jnp.sum(oh2, axis=1, keepdims=True)
    pc = jnp.floor((cnt1 + cnt2 + (MOE_CHUNK - 1.0)) * (1.0 / MOE_CHUNK)) * MOE_CHUNK
    pc_b = jnp.broadcast_to(pc, (E, 128))
    below = jnp.where(lax.broadcasted_iota(jnp.int32, (E, E), 1)
                      < lax.broadcasted_iota(jnp.int32, (E, E), 0), 1.0, 0.0)
    run_off = jnp.dot(below, pc_b, precision=HIGHEST, preferred_element_type=F32)[:, 0:1]
    meta_ref[0:1, :] = jnp.sum(oh1 * (run_off + r1), axis=0, keepdims=True)
    meta_ref[1:2, :] = jnp.sum(oh2 * (run_off + cnt1 + r2), axis=0, keepdims=True)
    meta_ref[2:3, :] = m1 / den
    meta_ref[3:4, :] = m2 / den
    meta_ref[4:8, :] = jnp.zeros((4, tm), F32)
    pc_ref[...] = pc_b


def _route(x2d, rwt, rbc):
    N, D = x2d.shape
    E = rwt.shape[0]
    nt = N // MOE_TILE
    full = lambda a: pl.BlockSpec(a.shape, lambda i: (0,) * a.ndim)
    return pl.pallas_call(
        functools.partial(_route_kernel, n_experts=E),
        grid=(nt,),
        in_specs=[pl.BlockSpec((MOE_TILE, D), lambda i: (i, 0)), full(rwt), full(rbc)],
        out_specs=(pl.BlockSpec((None, 8, MOE_TILE), lambda i: (i, 0, 0)),
                   pl.BlockSpec((None, E, 128), lambda i: (i, 0, 0))),
        out_shape=(jax.ShapeDtypeStruct((nt, 8, MOE_TILE), F32), jax.ShapeDtypeStruct((nt, E, 128), F32)),
        compiler_params=pltpu.CompilerParams(dimension_semantics=("arbitrary",),
                                             vmem_limit_bytes=VMEM_LIMIT_BYTES),
    )(x2d, rwt, rbc)


def _moe_schedule(pc, n_blocks_max):
    nt, E = pc.shape
    cum = jnp.cumsum(pc, axis=1)
    run_off = cum - pc
    total = cum[:, -1]
    seg_len = jnp.sum(pc, axis=0)
    seg_pad = ((seg_len + MOE_BLOCK - 1) // MOE_BLOCK) * MOE_BLOCK
    seg_end = jnp.cumsum(seg_pad)
    goff = (seg_end - seg_pad)[None, :] + (jnp.cumsum(pc, axis=0) - pc)
    r = jnp.arange(MOE_SLOTS // MOE_CHUNK, dtype=jnp.int32) * MOE_CHUNK
    e_c = jnp.minimum(jnp.sum((cum[:, None, :] <= r[None, :, None]).astype(jnp.int32), axis=-1), E - 1)
    pick = e_c[:, :, None] == jnp.arange(E, dtype=jnp.int32)[None, None, :]
    dst_row = r[None, :] + jnp.sum(jnp.where(pick, (goff - run_off)[:, None, :], 0), axis=-1)
    dst_chunk = jnp.where(r[None, :] < total[:, None], dst_row // MOE_CHUNK, -1).astype(jnp.int32)
    blk_start = jnp.arange(n_blocks_max, dtype=jnp.int32) * MOE_BLOCK
    block_expert = jnp.minimum(jnp.sum((seg_end[None, :] <= blk_start[:, None]).astype(jnp.int32), axis=-1), E - 1)
    n_valid = (seg_end[-1] // MOE_BLOCK).reshape(1).astype(jnp.int32)
    n_chunks_all = n_blocks_max * (MOE_BLOCK // MOE_CHUNK)
    n_fill_max = n_chunks_all - (nt * 2 * MOE_TILE) // MOE_CHUNK
    gap_start = jnp.concatenate([seg_end - seg_pad + seg_len, seg_end[-1:]]) // MOE_CHUNK
    gap_len = jnp.concatenate([seg_pad - seg_len, n_chunks_all * MOE_CHUNK - seg_end[-1:]]) // MOE_CHUNK
    gap_cum = jnp.cumsum(gap_len)
    k = jnp.arange(n_fill_max, dtype=jnp.int32)
    which = jnp.minimum(jnp.sum((gap_cum[None, :] <= k[:, None]).astype(jnp.int32), axis=-1), E)
    pick_gap = which[:, None] == jnp.arange(E + 1, dtype=jnp.int32)[None, :]
    fill = k + jnp.sum(jnp.where(pick_gap, (gap_start - (gap_cum - gap_len))[None, :], 0), axis=-1)
    fill = jnp.where(k < gap_cum[-1], fill, 0).astype(jnp.int32)
    n_fill = gap_cum[-1].reshape(1).astype(jnp.int32)
    return (dst_chunk.reshape(-1), total.astype(jnp.int32), block_expert.astype(jnp.int32), n_valid, fill, n_fill)


def _chunk_copy(hbm_ref, buf, sem, b, c, d, to_hbm):
    local = buf.at[b, pl.ds(c * MOE_CHUNK, MOE_CHUNK)]
    remote = hbm_ref.at[pl.ds(pl.multiple_of(d * MOE_CHUNK, MOE_CHUNK), MOE_CHUNK)]
    if to_hbm:
        return pltpu.make_async_copy(local, remote, sem.at[b])
    return pltpu.make_async_copy(remote, local, sem.at[b])


def _for_tile_chunks(dst_ref, tile, fn):
    n_chunks = MOE_SLOTS // MOE_CHUNK
    for c in range(n_chunks):
        d = dst_ref[tile * n_chunks + c]

        @pl.when(d >= 0)
        def _():
            fn(c, d)


def _dispatch_kernel(dst_ref, fill_ref, nfill_ref, x_ref, meta_ref, xs_hbm, buf, sem, zbuf, zsem):
    i = pl.program_id(0)
    n = pl.num_programs(0)
    b = i % 2
    tm = x_ref.shape[0]

    def wait_tile(tile, slot):
        _for_tile_chunks(dst_ref, tile, lambda c, d: _chunk_copy(xs_hbm, buf, sem, slot, c, d, True).wait())

    def zero_copy(k):
        d = fill_ref[k]
        return pltpu.make_async_copy(
            zbuf, xs_hbm.at[pl.ds(pl.multiple_of(d * MOE_CHUNK, MOE_CHUNK), MOE_CHUNK)], zsem.at[0])

    @pl.when(i == 0)
    def _():
        zbuf[...] = jnp.zeros_like(zbuf)
        lax.fori_loop(0, nfill_ref[0], lambda k, c: (zero_copy(k).start(), c)[1], 0)

    @pl.when(i >= 2)
    def _():
        wait_tile(i - 2, b)

    s1 = meta_ref[0:1, :].astype(jnp.int32)
    s2 = meta_ref[1:2, :].astype(jnp.int32)
    rows = lax.broadcasted_iota(jnp.int32, (MOE_SLOTS, tm), 0)
    dsp = jnp.where((rows == s1) | (rows == s2), 1.0, 0.0).astype(BF16)
    buf[b] = _dot(dsp, x_ref[...].astype(BF16)).astype(BF16)
    _for_tile_chunks(dst_ref, i, lambda c, d: _chunk_copy(xs_hbm, buf, sem, b, c, d, True).start())

    @pl.when(i == n - 1)
    def _():
        wait_tile(i, b)

    @pl.when((i == n - 1) & (i >= 1))
    def _():
        wait_tile(i - 1, 1 - b)

    @pl.when(i == n - 1)
    def _():
        lax.fori_loop(0, nfill_ref[0], lambda k, c: (zero_copy(k).wait(), c)[1], 0)


def _dispatch(x2d, meta, dst_chunk, fill, n_fill, n_rows):
    N, D = x2d.shape
    nt = N // MOE_TILE
    return pl.pallas_call(
        _dispatch_kernel,
        grid_spec=pltpu.PrefetchScalarGridSpec(
            num_scalar_prefetch=3, grid=(nt,),
            in_specs=[pl.BlockSpec((MOE_TILE, D), lambda i, *_: (i, 0)),
                      pl.BlockSpec((None, 8, MOE_TILE), lambda i, *_: (i, 0, 0))],
            out_specs=pl.BlockSpec(memory_space=pl.ANY),
            scratch_shapes=[pltpu.VMEM((2, MOE_SLOTS, D), BF16), pltpu.SemaphoreType.DMA((2,)),
                            pltpu.VMEM((MOE_CHUNK, D), BF16), pltpu.SemaphoreType.DMA((1,))]),
        out_shape=jax.ShapeDtypeStruct((n_rows, D), BF16),
        compiler_params=pltpu.CompilerParams(dimension_semantics=("arbitrary",),
                                             vmem_limit_bytes=VMEM_LIMIT_BYTES),
    )(dst_chunk, fill, n_fill, x2d, meta)


def _ffn_kernel(be_ref, nv_ref, xs_ref, wg_ref, wu_ref, wd_ref, ys_ref, wg_b, wu_b, wd_b):
    j = pl.program_id(0)
    used = j < nv_ref[0]
    new_expert = (j == 0) | (be_ref[j] != be_ref[jnp.maximum(j - 1, 0)])

    @pl.when(used & new_expert)
    def _():
        wg_b[...] = wg_ref[...].astype(BF16)
        wu_b[...] = wu_ref[...].astype(BF16)
        wd_b[...] = wd_ref[...].astype(BF16)

    @pl.when(used)
    def _():
        xb = xs_ref[...]
        g = _dot(xb, wg_b[...])
        up = _dot(xb, wu_b[...])
        hmid = (g * _sigmoid(g) * up).astype(BF16)
        ys_ref[...] = _dot(hmid, wd_b[...]).astype(ys_ref.dtype)

    @pl.when(jnp.logical_not(used))
    def _():
        ys_ref[...] = jnp.zeros_like(ys_ref)


def _ffn(xs, wg, wu, wd, layer, block_expert, n_valid):
    R, D = xs.shape
    _, E, _, De = wg.shape
    last = lambda j, nv: jnp.maximum(jnp.minimum(j, nv[0] - 1), 0)
    return pl.pallas_call(
        _ffn_kernel,
        grid_spec=pltpu.PrefetchScalarGridSpec(
            num_scalar_prefetch=2, grid=(R // MOE_BLOCK,),
            in_specs=[pl.BlockSpec((MOE_BLOCK, D), lambda j, be, nv: (last(j, nv), 0)),
                      pl.BlockSpec((None, None, D, De), lambda j, be, nv: (layer, be[last(j, nv)], 0, 0)),
                      pl.BlockSpec((None, None, D, De), lambda j, be, nv: (layer, be[last(j, nv)], 0, 0)),
                      pl.BlockSpec((None, None, De, D), lambda j, be, nv: (layer, be[last(j, nv)], 0, 0))],
            out_specs=pl.BlockSpec((MOE_BLOCK, D), lambda j, be, nv: (j, 0)),
            scratch_shapes=[pltpu.VMEM((D, De), BF16), pltpu.VMEM((D, De), BF16), pltpu.VMEM((De, D), BF16)]),
        out_shape=jax.ShapeDtypeStruct((R, D), BF16),
        compiler_params=pltpu.CompilerParams(dimension_semantics=("arbitrary",),
                                             vmem_limit_bytes=VMEM_LIMIT_BYTES),
    )(block_expert, n_valid, xs, wg, wu, wd)


def _combine_kernel(dst_ref, tot_ref, x_ref, metac_ref, ys_hbm, ln_g_ref, ln_b_ref, o_ref, buf, sem):
    i = pl.program_id(0)
    n = pl.num_programs(0)
    b = i % 2
    tm = x_ref.shape[0]

    def fetch(tile, slot):
        _for_tile_chunks(dst_ref, tile, lambda c, d: _chunk_copy(ys_hbm, buf, sem, slot, c, d, False).start())

    @pl.when(i == 0)
    def _():
        fetch(0, 0)

    @pl.when(i + 1 < n)
    def _():
        fetch(i + 1, 1 - b)

    _for_tile_chunks(dst_ref, i, lambda c, d: _chunk_copy(ys_hbm, buf, sem, b, c, d, False).wait())
    live = lax.broadcasted_iota(jnp.int32, (MOE_SLOTS, 1), 0) < tot_ref[i]
    ys = jnp.where(live, buf[b], jnp.zeros((), BF16))
    mc = metac_ref[...]
    s1 = mc[:, 0:1].astype(jnp.int32)
    s2 = mc[:, 1:2].astype(jnp.int32)
    lane = lax.broadcasted_iota(jnp.int32, (tm, MOE_SLOTS), 1)
    cmb = (jnp.where(lane == s1, mc[:, 2:3], 0.0) + jnp.where(lane == s2, mc[:, 3:4], 0.0)).astype(BF16)
    o_ref[...] = _layer_norm(ALPHA * x_ref[...] + _dot(cmb, ys), ln_g_ref[...], ln_b_ref[...])


def _combine(x2d, meta_col, ys, dst_chunk, total, ln_g, ln_b):
    N, D = x2d.shape
    nt = N // MOE_TILE
    return pl.pallas_call(
        _combine_kernel,
        grid_spec=pltpu.PrefetchScalarGridSpec(
            num_scalar_prefetch=2, grid=(nt,),
            in_specs=[pl.BlockSpec((MOE_TILE, D), lambda i, dst, tot: (i, 0)),
                      pl.BlockSpec((MOE_TILE, 8), lambda i, dst, tot: (i, 0)),
                      pl.BlockSpec(memory_space=pl.ANY),
                      pl.BlockSpec(ln_g.shape, lambda i, dst, tot: (0, 0)),
                      pl.BlockSpec(ln_b.shape, lambda i, dst, tot: (0, 0))],
            out_specs=pl.BlockSpec((MOE_TILE, D), lambda i, dst, tot: (i, 0)),
            scratch_shapes=[pltpu.VMEM((2, MOE_SLOTS, D), BF16), pltpu.SemaphoreType.DMA((2,))]),
        out_shape=jax.ShapeDtypeStruct((N, D), F32),
        compiler_params=pltpu.CompilerParams(dimension_semantics=("arbitrary",),
                                             vmem_limit_bytes=VMEM_LIMIT_BYTES),
    )(dst_chunk, total, x2d, meta_col, ys, ln_g, ln_b)


def _moe_sparse(x2d, rw, rb, wg, wu, wd, layer, ln_g, ln_b):
    N, D = x2d.shape
    E = rw.shape[1]
    nt = N // MOE_TILE
    meta, pc = _route(x2d, rw.T, rb.reshape(E, 1))
    pc = pc[:, :, 0].astype(jnp.int32)
    n_rows = nt * MOE_SLOTS + E * MOE_BLOCK
    assert n_rows % MOE_BLOCK == 0
    dst_chunk, total, block_expert, n_valid, fill, n_fill = _moe_schedule(pc, n_rows // MOE_BLOCK)
    xs = _dispatch(x2d, meta, dst_chunk, fill, n_fill, n_rows)
    ys = _ffn(xs, wg, wu, wd, layer, block_expert, n_valid)
    meta_col = jnp.swapaxes(meta, 1, 2).reshape(N, 8)
    return _combine(x2d, meta_col, ys, dst_chunk, total, ln_g, ln_b)


def _s5_weights_kernel(lam_col_ref, lam_row_ref, bt_ref, ct_ref, d_ref,
                       toe_ref, pst_ref, qst_ref, a_pow_ref):
    L, Cg = S5_CHUNK, S5_GROUP
    LC = L * Cg
    P = lam_row_ref.shape[-1]

    def discretise(a_re, a_im, log_dt):
        dt = jnp.exp(log_dt)
        lr, li = dt * a_re, dt * a_im
        er = jnp.exp(lr)
        abr, abi = er * jnp.cos(li), er * jnp.sin(li)
        nr, ni = abr - 1.0, abi
        den = a_re * a_re + a_im * a_im
        cr = (nr * a_re + ni * a_im) / den
        ci = (ni * a_re - nr * a_im) / den
        return lr, li, cr, ci

    def a_power(lr, li, tau):
        er = jnp.exp(tau * lr)
        return er * jnp.cos(tau * li), er * jnp.sin(tau * li)

    lc = lam_col_ref[...]
    lr_c, li_c, _, _ = discretise(lc[:, 0:1], lc[:, 1:2], lc[:, 2:3])
    lw = lam_row_ref[...]
    lr_r, li_r, cr_r, ci_r = discretise(lw[0:1, :], lw[1:2, :], lw[2:3, :])

    btr, bti = bt_ref[0], bt_ref[1]
    bbr = cr_r * btr - ci_r * bti
    bbi = cr_r * bti + ci_r * btr

    ctr, cti = ct_ref[0], ct_ref[1]
    tau_l = (lax.broadcasted_iota(jnp.int32, (P, LC), 1) // Cg).astype(F32)

    pr, pi = a_power(lr_c, li_c, tau_l)
    car = ctr * pr - cti * pi
    cai = ctr * pi + cti * pr
    t0 = (jnp.dot(bbr[0:Cg, :], car, preferred_element_type=F32, precision=HIGHEST)
          - jnp.dot(bbi[0:Cg, :], cai, preferred_element_type=F32, precision=HIGHEST))
    r16 = lax.broadcasted_iota(jnp.int32, (Cg, LC), 0)
    l16 = lax.broadcasted_iota(jnp.int32, (Cg, LC), 1)
    t0 = t0 + jnp.where(r16 == l16, d_ref[...], 0.0)
    toe_ref[0:Cg, :] = t0.astype(toe_ref.dtype)
    for s in range(1, L):
        toe_ref[s * Cg:(s + 1) * Cg, :] = jnp.where(
            l16 >= s * Cg, pltpu.roll(t0, s * Cg, axis=1), 0.0).astype(toe_ref.dtype)

    pr, pi = a_power(lr_c, li_c, tau_l + 1.0)
    qst_ref[0:P, :] = (ctr * pr - cti * pi).astype(qst_ref.dtype)
    qst_ref[P:2 * P, :] = (-(ctr * pi + cti * pr)).astype(qst_ref.dtype)

    tau_r = (L - 1 - lax.broadcasted_iota(jnp.int32, (LC, P), 0) // Cg).astype(F32)
    pr, pi = a_power(lr_r, li_r, tau_r)
    pst_ref[:, 0:P] = (bbr * pr - bbi * pi).astype(pst_ref.dtype)
    pst_ref[:, P:2 * P] = (bbr * pi + bbi * pr).astype(pst_ref.dtype)

    pr, pi = a_power(lr_r, li_r, float(L))
    a_pow_ref[0:1, :] = pr
    a_pow_ref[1:2, :] = pi


def _s5_weights(a_re, a_im, log_dt, b_re, b_im, c_re, c_im, d):
    G, P = a_re.shape
    Cg, L = S5_GROUP, S5_CHUNK
    LC = L * Cg
    lam = jnp.stack([a_re, a_im, jnp.broadcast_to(log_dt[:, None], (G, P)), jnp.zeros((G, P), F32)], axis=1)
    lam_row = lam
    lam_col = jnp.swapaxes(lam, 1, 2)
    bt = jnp.stack([jnp.swapaxes(b_re, 1, 2), jnp.swapaxes(b_im, 1, 2)], axis=1)
    bt = jnp.tile(bt, (1, 1, L, 1))
    ct = jnp.stack([jnp.swapaxes(c_re, 1, 2), jnp.swapaxes(c_im, 1, 2)], axis=1)
    ct = jnp.tile(ct, (1, 1, 1, L))
    d_col = d.reshape(G, Cg, 1)
    blk = lambda shp: pl.BlockSpec((None,) + shp, lambda g: (g,) + (0,) * len(shp))
    return pl.pallas_call(
        _s5_weights_kernel,
        grid=(G,),
        in_specs=[blk((P, 4)), blk((4, P)), blk((2, LC, P)), blk((2, P, LC)), blk((Cg, 1))],
        out_specs=(blk((LC, LC)), blk((LC, 2 * P)), blk((2 * P, LC)), blk((2, P))),
        out_shape=(jax.ShapeDtypeStruct((G, LC, LC), BF16),
                   jax.ShapeDtypeStruct((G, LC, 2 * P), BF16),
                   jax.ShapeDtypeStruct((G, 2 * P, LC), BF16),
                   jax.ShapeDtypeStruct((G, 2, P), F32)),
        compiler_params=pltpu.CompilerParams(dimension_semantics=("arbitrary",)),
    )(lam_col, lam_row, bt, ct, d_col)


def _granule_masks(rows):
    granule = lax.broadcasted_iota(jnp.int32, (rows, 128), 1) // S5_GROUP
    return [granule == q for q in range(128 // S5_GROUP)]


def _s5_in_kernel(x_ref, w_ref, o_ref, u_scr):
    B, tt, D = x_ref.shape
    L, Cg = S5_CHUNK, S5_GROUP
    gpb = 128 // Cg
    cpb = tt // L
    u = _dot(x_ref[...].reshape(B * tt, D).astype(BF16), w_ref[...])
    for j in range(D // 128):
        u_scr[j] = u[:, 128 * j:128 * (j + 1)]
    masks = _granule_masks(cpb * B)
    for j in range(D // 128):
        for h in range(L // gpb):
            pieces = [jnp.concatenate([u_scr[j, pl.ds(c * L + h * gpb + q, B, stride=tt), :]
                                       for c in range(cpb)], axis=0) for q in range(gpb)]
            for gg in range(gpb):
                dest = None
                for q in range(gpb):
                    moved = pieces[q] if q == gg else pltpu.roll(pieces[q], ((q - gg) % gpb) * Cg, axis=1)
                    dest = moved if dest is None else jnp.where(masks[q], moved, dest)
                o_ref[gpb * j + gg, :, 128 * h:128 * (h + 1)] = dest.astype(o_ref.dtype)


def _s5_in(x, w, tt):
    B, T, D = x.shape
    G = D // S5_GROUP
    LC = S5_CHUNK * S5_GROUP
    rows = (tt // S5_CHUNK) * B
    return pl.pallas_call(
        _s5_in_kernel,
        grid=(T // tt,),
        in_specs=[pl.BlockSpec((B, tt, D), lambda i: (0, i, 0)), pl.BlockSpec(w.shape, lambda i: (0, 0))],
        out_specs=pl.BlockSpec((G, rows, LC), lambda i: (0, i, 0)),
        out_shape=jax.ShapeDtypeStruct((G, (T // S5_CHUNK) * B, LC), BF16),
        scratch_shapes=[pltpu.VMEM((D // 128, B * tt, 128), F32)],
        compiler_params=pltpu.CompilerParams(dimension_semantics=("arbitrary",),
                                             vmem_limit_bytes=VMEM_LIMIT_BYTES),
    )(x, w)


def _s5_scan_kernel(x_ref, s0_ref, toe_ref, pst_ref, qst_ref, apow_ref, y_ref, sfin_ref,
                    zr_scr, zi_scr, sr_scr, si_scr, *, n_chunks, batch):
    P = apow_ref.shape[-1]
    xg = x_ref[...]
    z = _dot(xg, pst_ref[...])
    zr_scr[...] = z[:, 0:P]
    zi_scr[...] = z[:, P:2 * P]
    ar = apow_ref[0:1, :]
    ai = apow_ref[1:2, :]

    def step(n, carry):
        sr, si = carry
        rows = pl.ds(pl.multiple_of(n * batch, batch), batch)
        sr_scr[rows, :] = sr
        si_scr[rows, :] = si
        return (ar * sr - ai * si + zr_scr[rows, :], ar * si + ai * sr + zi_scr[rows, :])

    sr, si = lax.fori_loop(0, n_chunks, step, (s0_ref[0], s0_ref[1]), unroll=4)
    sfin_ref[0] = sr
    sfin_ref[1] = si
    y = (_dot(xg, toe_ref[...]) + _dot(sr_scr[...].astype(BF16), qst_ref[0:P, :])
         + _dot(si_scr[...].astype(BF16), qst_ref[P:2 * P, :]))
    y_ref[...] = y.astype(y_ref.dtype)


def _s5_scan(xg, s0, toe, pst, qst, apow, n_chunks, batch):
    G, R, LC = xg.shape
    P = apow.shape[-1]
    blk = lambda shp: pl.BlockSpec((None,) + shp, lambda g: (g,) + (0,) * len(shp))
    return pl.pallas_call(
        functools.partial(_s5_scan_kernel, n_chunks=n_chunks, batch=batch),
        grid=(G,),
        in_specs=[blk((R, LC)), blk((2, batch, P)), blk((LC, LC)), blk((LC, 2 * P)), blk((2 * P, LC)),
                  blk((2, P))],
        out_specs=(blk((R, LC)), blk((2, batch, P))),
        out_shape=(jax.ShapeDtypeStruct((G, R, LC), BF16), jax.ShapeDtypeStruct((G, 2, batch, P), F32)),
        scratch_shapes=[pltpu.VMEM((R, P), F32)] * 4,
        compiler_params=pltpu.CompilerParams(dimension_semantics=("arbitrary",),
                                             vmem_limit_bytes=VMEM_LIMIT_BYTES),
    )(xg, s0, toe, pst, qst, apow)


def _s5_out_kernel(x_ref, yg_ref, w_ref, ln_g_ref, ln_b_ref, o_ref, y_scr):
    B, tt, D = x_ref.shape
    L, Cg = S5_CHUNK, S5_GROUP
    gpb = 128 // Cg
    cpb = tt // L
    masks = _granule_masks(cpb * B)
    for j in range(D // 128):
        for h in range(L // gpb):
            srcs =[yg_ref[gpb * j + gg, :, 128 * h:128 * (h + 1)].astype(F32) for gg in range(gpb)]
            for q in range(gpb):
                dest = None
                for gg in range(gpb):
                    moved = srcs[gg] if gg == q else pltpu.roll(srcs[gg], ((gg - q) % gpb) * Cg, axis=1)
                    dest = moved if dest is None else jnp.where(masks[gg], moved, dest)
                for c in range(cpb):
                    y_scr[j, pl.ds(c * L + h * gpb + q, B, stride=tt), :] = dest[c * B:(c + 1) * B, :]
    y = jnp.concatenate([y_scr[j] for j in range(D // 128)], axis=1)
    act = _gelu_tanh(y).astype(BF16)
    zz = _dot(act, w_ref[...])
    mix = zz[:, :D] * _sigmoid(zz[:, D:])
    out = _layer_norm(ALPHA * x_ref[...].reshape(B * tt, D) + mix, ln_g_ref[...], ln_b_ref[...])
    o_ref[...] = out.reshape(B, tt, D)


def _s5_out(x, yg, w_out, ln_g, ln_b, tt):
    B, T, D = x.shape
    G, _, LC = yg.shape
    rows = (tt // S5_CHUNK) * B
    full = lambda a: pl.BlockSpec(a.shape, lambda i: (0,) * a.ndim)
    return pl.pallas_call(
        _s5_out_kernel,
        grid=(T // tt,),
        in_specs=[pl.BlockSpec((B, tt, D), lambda i: (0, i, 0)), pl.BlockSpec((G, rows, LC), lambda i: (0, i, 0)),
                  full(w_out), full(ln_g), full(ln_b)],
        out_specs=pl.BlockSpec((B, tt, D), lambda i: (0, i, 0)),
        out_shape=jax.ShapeDtypeStruct((B, T, D), F32),
        scratch_shapes=[pltpu.VMEM((D // 128, B * tt, 128), F32)],
        compiler_params=pltpu.CompilerParams(dimension_semantics=("arbitrary",),
                                             vmem_limit_bytes=VMEM_LIMIT_BYTES),
    )(x, yg, w_out, ln_g, ln_b)


def _s5_mixer(x, s0_re, s0_im, w_in, w_out, s5w, ln_g, ln_b):
    B, T, D = x.shape
    toe, pst, qst, apow = s5w
    xg = _s5_in(x, w_in, S5_STEP_TOKENS)
    s0 = jnp.stack([s0_re, s0_im], axis=0).transpose(2, 0, 1, 3)
    yg, sfin = _s5_scan(xg, s0, toe, pst, qst, apow, T // S5_CHUNK, B)
    out = _s5_out(x, yg, w_out, ln_g, ln_b, S5_STEP_TOKENS)
    sfin = sfin.transpose(1, 2, 0, 3)
    return out, sfin[0], sfin[1]


def _block_diag(w):
    H, d, _ = w.shape
    eye = jnp.eye(H, dtype=w.dtype)
    return (eye[:, None, :, None] * w[:, :, None, :]).reshape(H * d, H * d)


def kernel(x_prompt, x_sample, state_lru_h, state_lru_conv, state_hgrn2, state_s5_re, state_s5_im, ln_mix_g, ln_mix_b, ln_ffn_g, ln_ffn_b, w_in_ab, w_out_ab, lru_conv_w, lru_conv_b, lru_w_r, lru_b_r, lru_w_i, lru_b_i, lru_lam, hgrn_lb_logits, hgrn_norm_g, w_in_s5, w_out_s5, s5_a_re, s5_a_im, s5_b_re, s5_b_im, s5_c_re, s5_c_im, s5_d, s5_log_dt, router_w, router_b, moe_w_gate, moe_w_up, moe_w_down):
    Bp, Tp, D = x_prompt.shape
    Bs, Ts, _ = x_sample.shape
    W = state_lru_h.shape[-1]
    dk = W // HG_HEADS
    row = lambda v: v.reshape(1, -1)

    def moe(y, l, sparse):
        fn = _moe_sparse if sparse else functools.partial(_moe_dense, tm=512)
        out = fn(y.reshape(-1, D), router_w, row(router_b), moe_w_gate, moe_w_up, moe_w_down, l,
                 row(ln_ffn_g[l]), row(ln_ffn_b[l]))
        return out.reshape(y.shape)

    j = 0
    wts = (w_in_ab[j].astype(BF16), w_out_ab[j].astype(BF16), lru_conv_w[j], row(lru_conv_b[j]),
           _block_diag(lru_w_r[j]).astype(BF16), row(lru_b_r[j]),
           _block_diag(lru_w_i[j]).astype(BF16), row(lru_b_i[j]),
           row(lru_lam[j]), hgrn_lb_logits, row(hgrn_norm_g[j]), row(ln_mix_g[0]), row(ln_mix_b[0]))
    yp, p_h, p_conv, p_hg = _ab_mixer(
        x_prompt, jnp.zeros((Bp, 1, W), F32), jnp.zeros((Bp, 3, W), F32),
        jnp.zeros((Bp, HG_HEADS, dk, dk), F32), wts, AB_TILE, AB_SEQS)
    ys, s_h, s_conv, s_hg = _ab_mixer(
        x_sample, state_lru_h[j][:, None, :], state_lru_conv[j], state_hgrn2[j], wts, Ts, AB_SEQS)
    yp = moe(yp, 0, True)
    ys = moe(ys, 0, False)

    s5w = _s5_weights(s5_a_re[j], s5_a_im[j], s5_log_dt[j], s5_b_re[j], s5_b_im[j],
                      s5_c_re[j], s5_c_im[j], s5_d[j])
    G, P = s5_a_re[j].shape
    w_in5 = w_in_s5[j].astype(BF16)
    w_out5 = w_out_s5[j].astype(BF16)
    zp = jnp.zeros((Bp, G, P), F32)
    yp, p_re, p_im = _s5_mixer(yp, zp, zp, w_in5, w_out5, s5w, row(ln_mix_g[1]), row(ln_mix_b[1]))
    ys, s_re, s_im = _s5_mixer(ys, state_s5_re[j], state_s5_im[j], w_in5, w_out5, s5w,
                               row(ln_mix_g[1]), row(ln_mix_b[1]))
    yp = moe(yp, 1, True)
    ys = moe(ys, 1, False)

    return (yp, ys,
            p_h.reshape(1, Bp, W), p_conv[None], p_hg[None], p_re[None], p_im[None],
            s_h.reshape(1, Bs, W), s_conv[None], s_hg[None], s_re[None], s_im[None])
```

```python
import functools
import math

import jax
import jax.numpy as jnp
from jax import lax
from jax.experimental import pallas as pl
from jax.experimental.pallas import tpu as pltpu

F32 = jnp.float32
BF16 = jnp.bfloat16
HIGHEST = lax.Precision.HIGHEST

DEPTH = 2
RG_C = 8.0
LRU_HEADS = 8
HG_HEADS = 4
AB_TILE = 256
AB_SEQS = 2
HG_CHUNK = 16
RMS_EPS = 1e-6
S5_GROUP = 16
S5_SCAN_ROWS_PER_STEP = 2048
S5_STEP_TOKENS = 32
S5_CHUNK = 16
N_EXPERT_GROUPS = 4
ALPHA = (2.0 * DEPTH) ** 0.25
LN_EPS = 1e-5
MOE_TILE = 256
MOE_CHUNK = 16
MOE_SLOTS = 768
MOE_BLOCK = 1024
ROUTE_TILES_PER_STEP = 4
VMEM_LIMIT_BYTES = 56 * 1024 * 1024


def _dot(a, b):
    return jnp.dot(a, b, preferred_element_type=F32)


def _dot_nt(a, b):
    return lax.dot_general(a, b, (((1,), (1,)), ((), ())), preferred_element_type=F32)


def _sigmoid(x):
    return 1.0 / (1.0 + jnp.exp(-x))


def _gelu_tanh(x):
    c = math.sqrt(2.0 / math.pi)
    return 0.5 * x * (1.0 + jnp.tanh(c * (x + 0.044715 * (x * x * x))))


def _layer_norm(s, g, b):
    mu = jnp.mean(s, axis=-1, keepdims=True)
    d = s - mu
    var = jnp.mean(d * d, axis=-1, keepdims=True)
    return d * lax.rsqrt(var + LN_EPS) * g + b


def _shift_rows(x, s, row):
    del row
    return pltpu.roll(x, s, axis=0)


def _ab_kernel(x_ref, h0_ref, conv0_ref, s0_ref, w_in_ref, w_out_ref, conv_w_ref, conv_b_ref,
               wr_ref, br_ref, wi_ref, bi_ref, lam_ref, lb_logits_ref, norm_g_ref, ln_g_ref, ln_b_ref,
               y_ref, h_out_ref, conv_out_ref, s_out_ref,
               zx_buf, h_carry, st_carry, st_all, *, tb, width, nb):
    for si in range(nb):
        _ab_sequence_tile(si, x_ref, h0_ref, conv0_ref, s0_ref, w_in_ref, w_out_ref, conv_w_ref, conv_b_ref,
                          wr_ref, br_ref, wi_ref, bi_ref, lam_ref, lb_logits_ref, norm_g_ref, ln_g_ref, ln_b_ref,
                          y_ref, h_out_ref, conv_out_ref, s_out_ref,
                          zx_buf.at[si], h_carry.at[si], st_carry.at[si], st_all.at[si], tb, width)


def _ab_sequence_tile(si, x_ref, h0_ref, conv0_ref, s0_ref, w_in_ref, w_out_ref, conv_w_ref, conv_b_ref,
                      wr_ref, br_ref, wi_ref, bi_ref, lam_ref, lb_logits_ref, norm_g_ref, ln_g_ref, ln_b_ref,
                      y_ref, h_out_ref, conv_out_ref, s_out_ref, zx_buf, h_carry, st_carry, st_all, tb, width):
    t = pl.program_id(1)
    nt = pl.num_programs(1)
    W = width
    L = HG_CHUNK
    nc = tb // L
    dk = W // HG_HEADS

    @pl.when(t == 0)
    def _init():
        zx_buf[0:8, :] = jnp.zeros((8, W), F32)
        zx_buf[5:8, :] = conv0_ref[si]
        h_carry[...] = jnp.broadcast_to(h0_ref[si], h_carry.shape)
        for hd in range(HG_HEADS):
            st_carry[hd] = s0_ref[si, hd].T

    x = x_ref[si]
    z = _dot(x.astype(BF16), w_in_ref[...])
    z_gate = z[:, 0 * W:1 * W]
    z_x = z[:, 1 * W:2 * W]
    z_q = z[:, 2 * W:3 * W]
    z_f = z[:, 3 * W:4 * W]
    z_i = z[:, 4 * W:5 * W]
    z_g = z[:, 5 * W:6 * W]

    zx_buf[8:8 + tb, :] = z_x
    cw = conv_w_ref[...]
    u = (conv_b_ref[...] + cw[3:4, :] * z_x
         + cw[2:3, :] * zx_buf[7:7 + tb, :]
         + cw[1:2, :] * zx_buf[6:6 + tb, :]
         + cw[0:1, :] * zx_buf[5:5 + tb, :])
    tail = zx_buf[tb:tb + 8, :]
    zx_buf[0:8, :] = tail

    ub = u.astype(BF16)
    r = _sigmoid(_dot(ub, wr_ref[...]) + br_ref[...])
    ig = _sigmoid(_dot(ub, wi_ref[...]) + bi_ref[...])
    nlam = -lam_ref[...]
    softplus_nlam = jnp.maximum(nlam, 0.0) + jnp.log1p(jnp.exp(-jnp.abs(nlam)))
    log_a = (-RG_C) * r * softplus_nlam
    a = jnp.exp(log_a)
    th = jnp.tanh(log_a)
    b = jnp.sqrt(-2.0 * th / (1.0 - th)) * ig * u

    row = lax.broadcasted_iota(jnp.int32, (tb, W), 0)
    rin8 = row % 8
    s = 1
    while s < 8:
        m = rin8 >= s
        a_sh = _shift_rows(a, s, row)
        b_sh = _shift_rows(b, s, row)
        b = jnp.where(m, a * b_sh + b, b)
        a = jnp.where(m, a * a_sh, a)
        s *= 2
    a3 = a.reshape(tb // 8, 8, W)
    b3 = b.reshape(tb // 8, 8, W)
    h_prev = h_carry[0:1, :]
    h_groups = []
    for g in range(tb // 8):
        hg = b3[g] + a3[g] * h_prev
        h_groups.append(hg)
        h_prev = hg[7:8, :]
    h = jnp.concatenate(h_groups, axis=0)
    h_carry[...] = jnp.broadcast_to(h_prev, h_carry.shape)
    out_a = h * _gelu_tanh(z_gate)

    lg = lb_logits_ref[...]
    e = jnp.exp(lg - jnp.max(lg, axis=0, keepdims=True))
    lb = e[0:1, :] / jnp.sum(e, axis=0, keepdims=True)
    f = lb + (1.0 - lb) * _sigmoid(z_f)
    log_f = jnp.log(f)
    kk = 1.0 - f
    rin = row % L
    G = log_f
    s = 1
    while s < L:
        G = G + jnp.where(rin >= s, _shift_rows(G, s, row), 0.0)
        s *= 2
    G3 = G.reshape(nc, L, W)
    G_last = G3[:, L - 1:L, :]
    q_dec = (z_q * jnp.exp(G)).reshape(nc, L, W).astype(BF16)
    k_inc = (kk * jnp.exp(-G)).reshape(nc, L, W).astype(BF16)
    k_tail = (kk.reshape(nc, L, W) * jnp.exp(G_last - G3)).astype(BF16)
    decay = jnp.exp(G_last)
    v3 = z_i.reshape(nc, L, W).astype(BF16)
    causal = (lax.broadcasted_iota(jnp.int32, (nc, L, L), 1)
              >= lax.broadcasted_iota(jnp.int32, (nc, L, L), 2))

    o_heads = []
    for hd in range(HG_HEADS):
        sl = slice(hd * dk, (hd + 1) * dk)
        qd, ki, kt, vv = q_dec[:, :, sl], k_inc[:, :, sl], k_tail[:, :, sl], v3[:, :, sl]
        scores = jnp.einsum('ntk,nsk->nts', qd, ki, preferred_element_type=F32)
        scores = jnp.where(causal, scores, 0.0).astype(BF16)
        o_intra = jnp.einsum('nts,nsv->ntv', scores, vv, preferred_element_type=F32)
        vt = jnp.swapaxes(vv.astype(F32), 1, 2).astype(BF16)
        upd = jnp.einsum('nvs,nsk->nvk', vt, kt, preferred_element_type=F32)
        st = st_carry[hd]
        for n in range(nc):
            st_all[hd, n] = st.astype(BF16)
            st = st * decay[n, :, sl] + upd[n]
        st_carry[hd] = st
        o_inter = jnp.einsum('ntk,nvk->ntv', qd, st_all[hd], preferred_element_type=F32)
        o = (o_intra + o_inter).reshape(tb, dk)
        o = o * lax.rsqrt(jnp.mean(o * o, axis=-1, keepdims=True) + RMS_EPS)
        o_heads.append(o)
    o_all = jnp.concatenate(o_heads, axis=-1) * norm_g_ref[...]
    out_b = o_all * (z_g * _sigmoid(z_g))

    mixed = jnp.concatenate([out_a, out_b], axis=-1).astype(BF16)
    y = _dot(mixed, w_out_ref[...])
    y_ref[si] = _layer_norm(ALPHA * x + y, ln_g_ref[...], ln_b_ref[...])

    @pl.when(t == nt - 1)
    def _fin():
        h_out_ref[si] = h[tb - 1:tb, :]
        conv_out_ref[si] = tail[5:8, :]
        for hd in range(HG_HEADS):
            s_out_ref[si, hd] = st_carry[hd].T


def _ab_mixer(x, h0, conv0, s0, wts, tb, nb):
    B, T, D = x.shape
    W = h0.shape[-1]
    dk = W // HG_HEADS
    nt = T // tb
    nc = tb // HG_CHUNK
    (w_in, w_out, conv_w, conv_b, wr, br, wi, bi, lam, lb_logits, norm_g, ln_g, ln_b) = wts
    full = lambda a: pl.BlockSpec(a.shape, lambda b, t: (0,) * a.ndim)
    per_b = lambda shp: pl.BlockSpec((nb,) + shp, lambda b, t: (b,) + (0,) * len(shp))
    out_shape = (jax.ShapeDtypeStruct((B, T, D), F32),
                 jax.ShapeDtypeStruct((B, 1, W), F32),
                 jax.ShapeDtypeStruct((B, 3, W), F32),
                 jax.ShapeDtypeStruct((B, HG_HEADS, dk, dk), F32))
    return pl.pallas_call(
        functools.partial(_ab_kernel, tb=tb, width=W, nb=nb),
        grid=(B // nb, nt),
        in_specs=[pl.BlockSpec((nb, tb, D), lambda b, t: (b, t, 0)),
                  per_b((1, W)), per_b((3, W)), per_b((HG_HEADS, dk, dk))]
                 + [full(a) for a in wts],
        out_specs=(pl.BlockSpec((nb, tb, D), lambda b, t: (b, t, 0)),
                   per_b((1, W)), per_b((3, W)), per_b((HG_HEADS, dk, dk))),
        out_shape=out_shape,
        scratch_shapes=[pltpu.VMEM((nb, tb + 8, W), F32),
                        pltpu.VMEM((nb, 8, W), F32),
                        pltpu.VMEM((nb, HG_HEADS, dk, dk), F32),
                        pltpu.VMEM((nb, HG_HEADS, nc, dk, dk), BF16)],
        compiler_params=pltpu.CompilerParams(
            dimension_semantics=("arbitrary", "arbitrary"),
            vmem_limit_bytes=VMEM_LIMIT_BYTES),
    )(x, h0, conv0, s0, *wts)


def _router(x, rw, rb, n_experts):
    per_group = n_experts // N_EXPERT_GROUPS
    logits = jnp.dot(x, rw, preferred_element_type=F32, precision=HIGHEST) + rb
    logits = logits - jnp.max(logits, axis=-1, keepdims=True)
    ex = jnp.exp(logits)
    p = ex / jnp.sum(ex, axis=-1, keepdims=True)
    lane_i = lax.broadcasted_iota(jnp.int32, p.shape, 1)
    grp = lane_i // per_group
    lane = lane_i.astype(F32)
    big = float(n_experts)

    def top2(pm):
        m1 = jnp.max(pm, axis=-1, keepdims=True)
        i1 = jnp.min(jnp.where(pm == m1, lane, big), axis=-1, keepdims=True)
        pm2 = jnp.where(lane == i1, -1.0, pm)
        m2 = jnp.max(pm2, axis=-1, keepdims=True)
        i2 = jnp.min(jnp.where(pm2 == m2, lane, big), axis=-1, keepdims=True)
        return m1, i1, m2, i2

    best = None
    g_sel = None
    for j in range(N_EXPERT_GROUPS):
        m1, _, m2, _ = top2(jnp.where(grp == j, p, -1.0))
        score = m1 + m2
        if best is None:
            best, g_sel = score, jnp.zeros_like(lane_i[:, :1])
        else:
            better = score > best
            best = jnp.where(better, score, best)
            g_sel = jnp.where(better, j, g_sel)
    m1, i1, m2, i2 = top2(jnp.where(grp == g_sel, p, -1.0))
    den = m1 + m2
    return jnp.where(lane == i1, m1 / den, 0.0) + jnp.where(lane == i2, m2 / den, 0.0)


def _moe_dense_kernel(x_ref, rw_ref, rb_ref, wg_ref, wu_ref, wd_ref, ln_g_ref, ln_b_ref,
                      y_ref, xb_scr, comb_scr, acc_scr, *, n_experts):
    e = pl.program_id(1)

    @pl.when(e == 0)
    def _init():
        x = x_ref[...]
        xb_scr[...] = x.astype(BF16)
        comb_scr[...] = _router(x, rw_ref[...], rb_ref[...], n_experts)
        acc_scr[...] = jnp.zeros_like(acc_scr)

    comb = comb_scr[...]
    lane = lax.broadcasted_iota(jnp.int32, comb.shape, 1)
    c_e = jnp.sum(jnp.where(lane == e, comb, 0.0), axis=-1, keepdims=True)
    xb = xb_scr[...]
    g = _dot(xb, wg_ref[...].astype(BF16))
    up = _dot(xb, wu_ref[...].astype(BF16))
    hmid = (g * _sigmoid(g) * up).astype(BF16)
    acc_scr[...] += c_e * _dot(hmid, wd_ref[...].astype(BF16))

    @pl.when(e == n_experts - 1)
    def _fin():
        y_ref[...] = _layer_norm(ALPHA * x_ref[...] + acc_scr[...], ln_g_ref[...], ln_b_ref[...])


def _moe_dense(x2d, rw, rb, wg, wu, wd, layer, ln_g, ln_b, tm):
    N, D = x2d.shape
    _, E, _, De = wg.shape
    full = lambda a: pl.BlockSpec(a.shape, lambda i, e: (0,) * a.ndim)
    return pl.pallas_call(
        functools.partial(_moe_dense_kernel, n_experts=E),
        grid=(N // tm, E),
        in_specs=[pl.BlockSpec((tm, D), lambda i, e: (i, 0)), full(rw), full(rb),
                  pl.BlockSpec((None, None, D, De), lambda i, e: (layer, e, 0, 0)),
                  pl.BlockSpec((None, None, D, De), lambda i, e: (layer, e, 0, 0)),
                  pl.BlockSpec((None, None, De, D), lambda i, e: (layer, e, 0, 0)),
                  full(ln_g), full(ln_b)],
        out_specs=pl.BlockSpec((tm, D), lambda i, e: (i, 0)),
        out_shape=jax.ShapeDtypeStruct((N, D), F32),
        scratch_shapes=[pltpu.VMEM((tm, D), BF16), pltpu.VMEM((tm, E), F32), pltpu.VMEM((tm, D), F32)],
        compiler_params=pltpu.CompilerParams(
            dimension_semantics=("arbitrary", "arbitrary"),
            vmem_limit_bytes=VMEM_LIMIT_BYTES),
    )(x2d, rw, rb, wg, wu, wd, ln_g, ln_b)


def _route_kernel(x_ref, rwt_ref, rbc_ref, meta_ref, pc_ref, *, n_experts):
    E = n_experts
    per_group = E // N_EXPERT_GROUPS
    x = x_ref[...]
    tm = x.shape[0]
    logits = lax.dot_general(rwt_ref[...], x, (((1,), (1,)), ((), ())),
                             precision=HIGHEST, preferred_element_type=F32) + rbc_ref[...]
    logits = logits - jnp.max(logits, axis=0, keepdims=True)
    ex = jnp.exp(logits)
    p = ex / jnp.sum(ex, axis=0, keepdims=True)
    eid_i = lax.broadcasted_iota(jnp.int32, (E, tm), 0)
    grp = eid_i // per_group
    eid = eid_i.astype(F32)
    big = float(E)

    def top2(pm):
        m1 = jnp.max(pm, axis=0, keepdims=True)
        i1 = jnp.min(jnp.where(pm == m1, eid, big), axis=0, keepdims=True)
        pm2 = jnp.where(eid == i1, -1.0, pm)
        m2 = jnp.max(pm2, axis=0, keepdims=True)
        i2 = jnp.min(jnp.where(pm2 == m2, eid, big), axis=0, keepdims=True)
        return m1, i1, m2, i2

    best = None
    g_sel = None
    for j in range(N_EXPERT_GROUPS):
        m1, _, m2, _ = top2(jnp.where(grp == j, p, -1.0))
        score = m1 + m2
        if best is None:
            best, g_sel = score, jnp.zeros_like(eid_i[:1, :])
        else:
            better = score > best
            best = jnp.where(better, score, best)
            g_sel = jnp.where(better, j, g_sel)
    m1, i1, m2, i2 = top2(jnp.where(grp == g_sel, p, -1.0))
    den = m1 + m2

    oh1_all = jnp.where(eid == i1, 1.0, 0.0)
    oh2_all = jnp.where(eid == i2, 1.0, 0.0)
    g1_all = m1 / den
    g2_all = m2 / den
    tl = MOE_TILE
    earlier = jnp.where(lax.broadcasted_iota(jnp.int32, (tl, tl), 0)
                        < lax.broadcasted_iota(jnp.int32, (tl, tl), 1), 1.0, 0.0).astype(BF16)
    below = jnp.where(lax.broadcasted_iota(jnp.int32, (E, E), 1)
                      < lax.broadcasted_iota(jnp.int32, (E, E), 0), 1.0, 0.0)
    for k in range(tm // tl):
        lanes = slice(k * tl, (k + 1) * tl)
        oh1, oh2 = oh1_all[:, lanes], oh2_all[:, lanes]
        r1 = _dot(oh1.astype(BF16), earlier)
        r2 = _dot(oh2.astype(BF16), earlier)
        cnt1 = jnp.sum(oh1, axis=1, keepdims=True)
        cnt2 = jnp.sum(oh2, axis=1, keepdims=True)
        pc = jnp.floor((cnt1 + cnt2 + (MOE_CHUNK - 1.0)) * (1.0 / MOE_CHUNK)) * MOE_CHUNK
        pc_b = jnp.broadcast_to(pc, (E, 128))
        run_off = jnp.dot(below, pc_b, precision=HIGHEST, preferred_element_type=F32)[:, 0:1]
        meta_ref[k, 0:1, :] = jnp.sum(oh1 * (run_off + r1), axis=0, keepdims=True)
        meta_ref[k, 1:2, :] = jnp.sum(oh2 * (run_off + cnt1 + r2), axis=0, keepdims=True)
        meta_ref[k, 2:3, :] = g1_all[:, lanes]
        meta_ref[k, 3:4, :] = g2_all[:, lanes]
        meta_ref[k, 4:8, :] = jnp.zeros((4, tl), F32)
        pc_ref[k] = pc_b


def _route(x2d, rwt, rbc):
    N, D = x2d.shape
    E = rwt.shape[0]
    nt = N // MOE_TILE
    tps = ROUTE_TILES_PER_STEP if nt % ROUTE_TILES_PER_STEP == 0 else 1
    full = lambda a: pl.BlockSpec(a.shape, lambda i: (0,) * a.ndim)
    return pl.pallas_call(
        functools.partial(_route_kernel, n_experts=E),
        grid=(nt // tps,),
        in_specs=[pl.BlockSpec((tps * MOE_TILE, D), lambda i: (i, 0)), full(rwt), full(rbc)],
        out_specs=(pl.BlockSpec((tps, 8, MOE_TILE), lambda i: (i, 0, 0)),
                   pl.BlockSpec((tps, E, 128), lambda i: (i, 0, 0))),
        out_shape=(jax.ShapeDtypeStruct((nt, 8, MOE_TILE), F32), jax.ShapeDtypeStruct((nt, E, 128), F32)),
        compiler_params=pltpu.CompilerParams(dimension_semantics=("arbitrary",),
                                             vmem_limit_bytes=VMEM_LIMIT_BYTES),
    )(x2d, rwt, rbc)


def _moe_schedule(pc, n_blocks_max):
    nt, E = pc.shape
    cum = jnp.cumsum(pc, axis=1)
    run_off = cum - pc
    total = cum[:, -1]
    seg_len = jnp.sum(pc, axis=0)
    seg_pad = ((seg_len + MOE_BLOCK - 1) // MOE_BLOCK) * MOE_BLOCK
    seg_end = jnp.cumsum(seg_pad)
    goff = (seg_end - seg_pad)[None, :] + (jnp.cumsum(pc, axis=0) - pc)
    r = jnp.arange(MOE_SLOTS // MOE_CHUNK, dtype=jnp.int32) * MOE_CHUNK
    e_c = jnp.minimum(jnp.sum((cum[:, None, :] <= r[None, :, None]).astype(jnp.int32), axis=-1), E - 1)
    pick = e_c[:, :, None] == jnp.arange(E, dtype=jnp.int32)[None, None, :]
    dst_row = r[None, :] + jnp.sum(jnp.where(pick, (goff - run_off)[:, None, :], 0), axis=-1)
    dst_chunk = jnp.where(r[None, :] < total[:, None], dst_row // MOE_CHUNK, -1).astype(jnp.int32)
    blk_start = jnp.arange(n_blocks_max, dtype=jnp.int32) * MOE_BLOCK
    block_expert = jnp.minimum(jnp.sum((seg_end[None, :] <= blk_start[:, None]).astype(jnp.int32), axis=-1), E - 1)
    n_valid = (seg_end[-1] // MOE_BLOCK).reshape(1).astype(jnp.int32)
    n_chunks_all = n_blocks_max * (MOE_BLOCK // MOE_CHUNK)
    n_fill_max = n_chunks_all - (nt * 2 * MOE_TILE) // MOE_CHUNK
    gap_start = jnp.concatenate([seg_end - seg_pad + seg_len, seg_end[-1:]]) // MOE_CHUNK
    gap_len = jnp.concatenate([seg_pad - seg_len, n_chunks_all * MOE_CHUNK - seg_end[-1:]]) // MOE_CHUNK
    gap_cum = jnp.cumsum(gap_len)
    k = jnp.arange(n_fill_max, dtype=jnp.int32)
    which = jnp.minimum(jnp.sum((gap_cum[None, :] <= k[:, None]).astype(jnp.int32), axis=-1), E)
    pick_gap = which[:, None] == jnp.arange(E + 1, dtype=jnp.int32)[None, :]
    fill = k + jnp.sum(jnp.where(pick_gap, (gap_start - (gap_cum - gap_len))[None, :], 0), axis=-1)
    fill = jnp.where(k < gap_cum[-1], fill, 0).astype(jnp.int32)
    n_fill = gap_cum[-1].reshape(1).astype(jnp.int32)
    return (dst_chunk.reshape(-1), total.astype(jnp.int32), block_expert.astype(jnp.int32), n_valid, fill, n_fill)


def _chunk_copy(hbm_ref, buf, sem, b, c, d, to_hbm):
    local = buf.at[b, pl.ds(c * MOE_CHUNK, MOE_CHUNK)]
    remote = hbm_ref.at[pl.ds(pl.multiple_of(d * MOE_CHUNK, MOE_CHUNK), MOE_CHUNK)]
    if to_hbm:
        return pltpu.make_async_copy(local, remote, sem.at[b])
    return pltpu.make_async_copy(remote, local, sem.at[b])


def _for_tile_chunks(dst_ref, tile, fn):
    n_chunks = MOE_SLOTS // MOE_CHUNK
    for c in range(n_chunks):
        d = dst_ref[tile * n_chunks + c]

        @pl.when(d >= 0)
        def _():
            fn(c, d)


def _dispatch_kernel(dst_ref, fill_ref, nfill_ref, x_ref, meta_ref, xs_hbm, buf, sem, zbuf, zsem):
    i = pl.program_id(0)
    n = pl.num_programs(0)
    b = i % 2
    tm = x_ref.shape[0]

    def wait_tile(tile, slot):
        _for_tile_chunks(dst_ref, tile, lambda c, d: _chunk_copy(xs_hbm, buf, sem, slot, c, d, True).wait())

    def zero_copy(k):
        d = fill_ref[k]
        return pltpu.make_async_copy(
            zbuf, xs_hbm.at[pl.ds(pl.multiple_of(d * MOE_CHUNK, MOE_CHUNK), MOE_CHUNK)], zsem.at[0])

    @pl.when(i == 0)
    def _():
        zbuf[...] = jnp.zeros_like(zbuf)
        lax.fori_loop(0, nfill_ref[0], lambda k, c: (zero_copy(k).start(), c)[1], 0)

    @pl.when(i >= 2)
    def _():
        wait_tile(i - 2, b)

    s1 = meta_ref[0:1, :].astype(jnp.int32)
    s2 = meta_ref[1:2, :].astype(jnp.int32)
    rows = lax.broadcasted_iota(jnp.int32, (MOE_SLOTS, tm), 0)
    dsp = jnp.where((rows == s1) | (rows == s2), 1.0, 0.0).astype(BF16)
    buf[b] = _dot(dsp, x_ref[...].astype(BF16)).astype(BF16)
    _for_tile_chunks(dst_ref, i, lambda c, d: _chunk_copy(xs_hbm, buf, sem, b, c, d, True).start())

    @pl.when(i == n - 1)
    def _():
        wait_tile(i, b)

    @pl.when((i == n - 1) & (i >= 1))
    def _():
        wait_tile(i - 1, 1 - b)

    @pl.when(i == n - 1)
    def _():
        lax.fori_loop(0, nfill_ref[0], lambda k, c: (zero_copy(k).wait(), c)[1], 0)


def _dispatch(x2d, meta, dst_chunk, fill, n_fill, n_rows):
    N, D = x2d.shape
    nt = N // MOE_TILE
    return pl.pallas_call(
        _dispatch_kernel,
        grid_spec=pltpu.PrefetchScalarGridSpec(
            num_scalar_prefetch=3, grid=(nt,),
            in_specs=[pl.BlockSpec((MOE_TILE, D), lambda i, *_: (i, 0)),
                      pl.BlockSpec((None, 8, MOE_TILE), lambda i, *_: (i, 0, 0))],
            out_specs=pl.BlockSpec(memory_space=pl.ANY),
            scratch_shapes=[pltpu.VMEM((2, MOE_SLOTS, D), BF16), pltpu.SemaphoreType.DMA((2,)),
                            pltpu.VMEM((MOE_CHUNK, D), BF16), pltpu.SemaphoreType.DMA((1,))]),
        out_shape=jax.ShapeDtypeStruct((n_rows, D), BF16),
        compiler_params=pltpu.CompilerParams(dimension_semantics=("arbitrary",),
                                             vmem_limit_bytes=VMEM_LIMIT_BYTES),
    )(dst_chunk, fill, n_fill, x2d, meta)


def _ffn_kernel(be_ref, nv_ref, xs_ref, wg_ref, wu_ref, wd_ref, ys_ref, wg_b, wu_b, wd_b):
    j = pl.program_id(0)
    used = j < nv_ref[0]
    new_expert = (j == 0) | (be_ref[j] != be_ref[jnp.maximum(j - 1, 0)])

    @pl.when(used & new_expert)
    def _():
        wg_b[...] = wg_ref[...].astype(BF16)
        wu_b[...] = wu_ref[...].astype(BF16)
        wd_b[...] = wd_ref[...].astype(BF16)

    @pl.when(used)
    def _():
        xb = xs_ref[...]
        g = _dot(xb, wg_b[...])
        up = _dot(xb, wu_b[...])
        hmid = (g * _sigmoid(g) * up).astype(BF16)
        ys_ref[...] = _dot(hmid, wd_b[...]).astype(ys_ref.dtype)

    @pl.when(jnp.logical_not(used))
    def _():
        ys_ref[...] = jnp.zeros_like(ys_ref)


def _ffn(xs, wg, wu, wd, layer, block_expert, n_valid):
    R, D = xs.shape
    _, E, _, De = wg.shape
    last = lambda j, nv: jnp.maximum(jnp.minimum(j, nv[0] - 1), 0)
    return pl.pallas_call(
        _ffn_kernel,
        grid_spec=pltpu.PrefetchScalarGridSpec(
            num_scalar_prefetch=2, grid=(R // MOE_BLOCK,),
            in_specs=[pl.BlockSpec((MOE_BLOCK, D), lambda j, be, nv: (last(j, nv), 0)),
                      pl.BlockSpec((None, None, D, De), lambda j, be, nv: (layer, be[last(j, nv)], 0, 0)),
                      pl.BlockSpec((None, None, D, De), lambda j, be, nv: (layer, be[last(j, nv)], 0, 0)),
                      pl.BlockSpec((None, None, De, D), lambda j, be, nv: (layer, be[last(j, nv)], 0, 0))],
            out_specs=pl.BlockSpec((MOE_BLOCK, D), lambda j, be, nv: (j, 0)),
            scratch_shapes=[pltpu.VMEM((D, De), BF16), pltpu.VMEM((D, De), BF16), pltpu.VMEM((De, D), BF16)]),
        out_shape=jax.ShapeDtypeStruct((R, D), BF16),
        compiler_params=pltpu.CompilerParams(dimension_semantics=("arbitrary",),
                                             vmem_limit_bytes=VMEM_LIMIT_BYTES),
    )(block_expert, n_valid, xs, wg, wu, wd)


def _combine_kernel(dst_ref, tot_ref, x_ref, metac_ref, ys_hbm, ln_g_ref, ln_b_ref, o_ref, buf, sem):
    i = pl.program_id(0)
    n = pl.num_programs(0)
    b = i % 2
    tm = x_ref.shape[0]

    def fetch(tile, slot):
        _for_tile_chunks(dst_ref, tile, lambda c, d: _chunk_copy(ys_hbm, buf, sem, slot, c, d, False).start())

    @pl.when(i == 0)
    def _():
        fetch(0, 0)

    @pl.when(i + 1 < n)
    def _():
        fetch(i + 1, 1 - b)

    _for_tile_chunks(dst_ref, i, lambda c, d: _chunk_copy(ys_hbm, buf, sem, b, c, d, False).wait())
    live = lax.broadcasted_iota(jnp.int32, (MOE_SLOTS, 1), 0) < tot_ref[i]
    ys = jnp.where(live, buf[b], jnp.zeros((), BF16))
    mc = metac_ref[...]
    s1 = mc[:, 0:1].astype(jnp.int32)
    s2 = mc[:, 1:2].astype(jnp.int32)
    lane = lax.broadcasted_iota(jnp.int32, (tm, MOE_SLOTS), 1)
    cmb = (jnp.where(lane == s1, mc[:, 2:3], 0.0) + jnp.where(lane == s2, mc[:, 3:4], 0.0)).astype(BF16)
    o_ref[...] = _layer_norm(ALPHA * x_ref[...] + _dot(cmb, ys), ln_g_ref[...], ln_b_ref[...])


def _combine(x2d, meta_col, ys, dst_chunk, total, ln_g, ln_b):
    N, D = x2d.shape
    nt = N // MOE_TILE
    return pl.pallas_call(
        _combine_kernel,
        grid_spec=pltpu.PrefetchScalarGridSpec(
            num_scalar_prefetch=2, grid=(nt,),
            in_specs=[pl.BlockSpec((MOE_TILE, D), lambda i, dst, tot: (i, 0)),
                      pl.BlockSpec((MOE_TILE, 8), lambda i, dst, tot: (i, 0)),
                      pl.BlockSpec(memory_space=pl.ANY),
                      pl.BlockSpec(ln_g.shape, lambda i, dst, tot: (0, 0)),
                      pl.BlockSpec(ln_b.shape, lambda i, dst, tot: (0, 0))],
            out_specs=pl.BlockSpec((MOE_TILE, D), lambda i, dst, tot: (i, 0)),
            scratch_shapes=[pltpu.VMEM((2, MOE_SLOTS, D), BF16), pltpu.SemaphoreType.DMA((2,))]),
        out_shape=jax.ShapeDtypeStruct((N, D), F32),
        compiler_params=pltpu.CompilerParams(dimension_semantics=("arbitrary",),
                                             vmem_limit_bytes=VMEM_LIMIT_BYTES),
    )(dst_chunk, total, x2d, meta_col, ys, ln_g, ln_b)


def _moe_sparse(x2d, rw, rb, wg, wu, wd, layer, ln_g, ln_b):
    N, D = x2d.shape
    E = rw.shape[1]
    nt = N // MOE_TILE
    meta, pc = _route(x2d, rw.T, rb.reshape(E, 1))
    pc = pc[:, :, 0].astype(jnp.int32)
    n_rows = nt * MOE_SLOTS + E * MOE_BLOCK
    assert n_rows % MOE_BLOCK == 0
    dst_chunk, total, block_expert, n_valid, fill, n_fill = _moe_schedule(pc, n_rows // MOE_BLOCK)
    xs = _dispatch(x2d, meta, dst_chunk, fill, n_fill, n_rows)
    ys = _ffn(xs, wg, wu, wd, layer, block_expert, n_valid)
    meta_col = jnp.swapaxes(meta, 1, 2).reshape(N, 8)
    return _combine(x2d, meta_col, ys, dst_chunk, total, ln_g, ln_b)


def _s5_weights_kernel(lam_col_ref, lam_row_ref, bt_ref, ct_ref, d_ref,
                       toe_ref, pst_ref, qst_ref, a_pow_ref):
    L, Cg = S5_CHUNK, S5_GROUP
    LC = L * Cg
    P = lam_row_ref.shape[-1]

    def discretise(a_re, a_im, log_dt):
        dt = jnp.exp(log_dt)
        lr, li = dt * a_re, dt * a_im
        er = jnp.exp(lr)
        abr, abi = er * jnp.cos(li), er * jnp.sin(li)
        nr, ni = abr - 1.0, abi
        den = a_re * a_re + a_im * a_im
        cr = (nr * a_re + ni * a_im) / den
        ci = (ni * a_re - nr * a_im) / den
        return lr, li, cr, ci

    def a_power(lr, li, tau):
        er = jnp.exp(tau * lr)
        return er * jnp.cos(tau * li), er * jnp.sin(tau * li)

    lc = lam_col_ref[...]
    lr_c, li_c, _, _ = discretise(lc[:, 0:1], lc[:, 1:2], lc[:, 2:3])
    lw = lam_row_ref[...]
    lr_r, li_r, cr_r, ci_r = discretise(lw[0:1, :], lw[1:2, :], lw[2:3, :])

    btr, bti = bt_ref[0], bt_ref[1]
    bbr = cr_r * btr - ci_r * bti
    bbi = cr_r * bti + ci_r * btr

    ctr, cti = ct_ref[0], ct_ref[1]
    tau_l = (lax.broadcasted_iota(jnp.int32, (P, LC), 1) // Cg).astype(F32)

    pr, pi = a_power(lr_c, li_c, tau_l)
    car = ctr * pr - cti * pi
    cai = ctr * pi + cti * pr
    t0 = (jnp.dot(bbr[0:Cg, :], car, preferred_element_type=F32, precision=HIGHEST)
          - jnp.dot(bbi[0:Cg, :], cai, preferred_element_type=F32, precision=HIGHEST))
    r16 = lax.broadcasted_iota(jnp.int32, (Cg, LC), 0)
    l16 = lax.broadcasted_iota(jnp.int32, (Cg, LC), 1)
    t0 = t0 + jnp.where(r16 == l16, d_ref[...], 0.0)
    toe_ref[0:Cg, :] = t0.astype(toe_ref.dtype)
    for s in range(1, L):
        toe_ref[s * Cg:(s + 1) * Cg, :] = jnp.where(
            l16 >= s * Cg, pltpu.roll(t0, s * Cg, axis=1), 0.0).astype(toe_ref.dtype)

    pr, pi = a_power(lr_c, li_c, tau_l + 1.0)
    qst_ref[0:P, :] = (ctr * pr - cti * pi).astype(qst_ref.dtype)
    qst_ref[P:2 * P, :] = (-(ctr * pi + cti * pr)).astype(qst_ref.dtype)

    tau_r = (L - 1 - lax.broadcasted_iota(jnp.int32, (LC, P), 0) // Cg).astype(F32)
    pr, pi = a_power(lr_r, li_r, tau_r)
    pst_ref[:, 0:P] = (bbr * pr - bbi * pi).astype(pst_ref.dtype)
    pst_ref[:, P:2 * P] = (bbr * pi + bbi * pr).astype(pst_ref.dtype)

    pr, pi = a_power(lr_r, li_r, float(L))
    a_pow_ref[0:1, :] = pr
    a_pow_ref[1:2, :] = pi


def _s5_weights(a_re, a_im, log_dt, b_re, b_im, c_re, c_im, d):
    G, P = a_re.shape
    Cg, L = S5_GROUP, S5_CHUNK
    LC = L * Cg
    lam = jnp.stack([a_re, a_im, jnp.broadcast_to(log_dt[:, None], (G, P)), jnp.zeros((G, P), F32)], axis=1)
    lam_row = lam
    lam_col = jnp.swapaxes(lam, 1, 2)
    bt = jnp.stack([jnp.swapaxes(b_re, 1, 2), jnp.swapaxes(b_im, 1, 2)], axis=1)
    bt = jnp.tile(bt, (1, 1, L, 1))
    ct = jnp.stack([jnp.swapaxes(c_re, 1, 2), jnp.swapaxes(c_im, 1, 2)], axis=1)
    ct = jnp.tile(ct, (1, 1, 1, L))
    d_col = d.reshape(G, Cg, 1)
    blk = lambda shp: pl.BlockSpec((None,) + shp, lambda g: (g,) + (0,) * len(shp))
    return pl.pallas_call(
        _s5_weights_kernel,
        grid=(G,),
        in_specs=[blk((P, 4)), blk((4, P)), blk((2, LC, P)), blk((2, P, LC)), blk((Cg, 1))],
        out_specs=(blk((LC, LC)), blk((LC, 2 * P)), blk((2 * P, LC)), blk((2, P))),
        out_shape=(jax.ShapeDtypeStruct((G, LC, LC), BF16),
                   jax.ShapeDtypeStruct((G, LC, 2 * P), BF16),
                   jax.ShapeDtypeStruct((G, 2 * P, LC), BF16),
                   jax.ShapeDtypeStruct((G, 2, P), F32)),
        compiler_params=pltpu.CompilerParams(dimension_semantics=("arbitrary",)),
    )(lam_col, lam_row, bt, ct, d_col)


def _granule_masks(rows):
    granule = lax.broadcasted_iota(jnp.int32, (rows, 128), 1) // S5_GROUP
    return [granule == q for q in range(128 // S5_GROUP)]


def _s5_in_kernel(x_ref, w_ref, o_ref, u_scr):
    B, tt, D = x_ref.shape
    L, Cg = S5_CHUNK, S5_GROUP
    gpb = 128 // Cg
    cpb = tt // L
    u = _dot(x_ref[...].reshape(B * tt, D).astype(BF16), w_ref[...])
    for j in range(D // 128):
        u_scr[j] = u[:, 128 * j:128 * (j + 1)]
    masks = _granule_masks(cpb * B)
    for j in range(D // 128):
        for h in range(L // gpb):
            pieces = [jnp.concatenate([u_scr[j, pl.ds(c * L + h * gpb + q, B, stride=tt), :]
                                       for c in range(cpb)], axis=0) for q in range(gpb)]
            for gg in range(gpb):
                dest = None
                for q in range(gpb):
                    moved = pieces[q] if q == gg else pltpu.roll(pieces[q], ((q - gg) % gpb) * Cg, axis=1)
                    dest = moved if dest is None else jnp.where(masks[q], moved, dest)
                o_ref[gpb * j + gg, :, 128 * h:128 * (h + 1)] = dest.astype(o_ref.dtype)


def _s5_in(x, w, tt):
    B, T, D = x.shape
    G = D // S5_GROUP
    LC = S5_CHUNK * S5_GROUP
    rows = (tt // S5_CHUNK) * B
    return pl.pallas_call(
        _s5_in_kernel,
        grid=(T // tt,),
        in_specs=[pl.BlockSpec((B, tt, D), lambda i: (0, i, 0)), pl.BlockSpec(w.shape, lambda i: (0, 0))],
        out_specs=pl.BlockSpec((G, rows, LC), lambda i: (0, i, 0)),
        out_shape=jax.ShapeDtypeStruct((G, (T // S5_CHUNK) * B, LC), BF16),
        scratch_shapes=[pltpu.VMEM((D // 128, B * tt, 128), F32)],
        compiler_params=pltpu.CompilerParams(dimension_semantics=("arbitrary",),
                                             vmem_limit_bytes=VMEM_LIMIT_BYTES),
    )(x, w)


def _s5_scan_kernel(x_ref, s0_ref, toe_ref, pst_ref, qst_ref, apow_ref, y_ref, sfin_ref,
                    zr_scr, zi_scr, sr_scr, si_scr, *, n_chunks, batch):
    P = apow_ref.shape[-1]
    for gi in range(x_ref.shape[0]):
        xg = x_ref[gi]
        z = _dot(xg, pst_ref[gi])
        zr_scr[...] = z[:, 0:P]
        zi_scr[...] = z[:, P:2 * P]
        ar = apow_ref[gi, 0:1, :]
        ai = apow_ref[gi, 1:2, :]

        def step(n, carry, ar=ar, ai=ai):
            sr, si = carry
            rows = pl.ds(pl.multiple_of(n * batch, batch), batch)
            sr_scr[rows, :] = sr
            si_scr[rows, :] = si
            return (ar * sr - ai * si + zr_scr[rows, :], ar * si + ai * sr + zi_scr[rows, :])

        sr, si = lax.fori_loop(0, n_chunks, step, (s0_ref[gi, 0], s0_ref[gi, 1]), unroll=4)
        sfin_ref[gi, 0] = sr
        sfin_ref[gi, 1] = si
        y = (_dot(xg, toe_ref[gi]) + _dot(sr_scr[...].astype(BF16), qst_ref[gi, 0:P, :])
             + _dot(si_scr[...].astype(BF16), qst_ref[gi, P:2 * P, :]))
        y_ref[gi] = y.astype(y_ref.dtype)


def _s5_scan(xg, s0, toe, pst, qst, apow, n_chunks, batch):
    G, R, LC = xg.shape
    P = apow.shape[-1]
    gb = max(1, min(8, S5_SCAN_ROWS_PER_STEP // R))
    blk = lambda shp: pl.BlockSpec((gb,) + shp, lambda g: (g,) + (0,) * len(shp))
    return pl.pallas_call(
        functools.partial(_s5_scan_kernel, n_chunks=n_chunks, batch=batch),
        grid=(G // gb,),
        in_specs=[blk((R, LC)), blk((2, batch, P)), blk((LC, LC)), blk((LC, 2 * P)), blk((2 * P, LC)),
                  blk((2, P))],
        out_specs=(blk((R, LC)), blk((2, batch, P))),
        out_shape=(jax.ShapeDtypeStruct((G, R, LC), BF16), jax.ShapeDtypeStruct((G, 2, batch, P), F32)),
        scratch_shapes=[pltpu.VMEM((R, P), F32)] * 4,
        compiler_params=pltpu.CompilerParams(dimension_semantics=("arbitrary",),
                                             vmem_limit_bytes=VMEM_LIMIT_BYTES),
    )(xg, s0, toe, pst, qst, apow)


def _s5_out_kernel(x_ref, yg_ref, w_ref, ln_g_ref, ln_b_ref, o_ref, y_scr):
    B, tt, D = x_ref.shape
    L, Cg = S5_CHUNK, S5_GROUP
    gpb = 128 // Cg
    cpb = tt // L
    masks = _granule_masks(cpb * B)
    for j in range(D // 128):
        for h in range(L // gpb):
            srcs =[yg_ref[gpb * j + gg, :, 128 * h:128 * (h + 1)].astype(F32) for gg in range(gpb)]
            for q in range(gpb):
                dest = None
                for gg in range(gpb):
                    moved = srcs[gg] if gg == q else pltpu.roll(srcs[gg], ((gg - q) % gpb) * Cg, axis=1)
                    dest = moved if dest is None else jnp.where(masks[gg], moved, dest)
                for c in range(cpb):
                    y_scr[j, pl.ds(c * L + h * gpb + q, B, stride=tt), :] = dest[c * B:(c + 1) * B, :]
    y = jnp.concatenate([y_scr[j] for j in range(D // 128)], axis=1)
    act = _gelu_tanh(y).astype(BF16)
    zz = _dot(act, w_ref[...])
    mix = zz[:, :D] * _sigmoid(zz[:, D:])
    out = _layer_norm(ALPHA * x_ref[...].reshape(B * tt, D) + mix, ln_g_ref[...], ln_b_ref[...])
    o_ref[...] = out.reshape(B, tt, D)


def _s5_out(x, yg, w_out, ln_g, ln_b, tt):
    B, T, D = x.shape
    G, _, LC = yg.shape
    rows = (tt // S5_CHUNK) * B
    full = lambda a: pl.BlockSpec(a.shape, lambda i: (0,) * a.ndim)
    return pl.pallas_call(
        _s5_out_kernel,
        grid=(T // tt,),
        in_specs=[pl.BlockSpec((B, tt, D), lambda i: (0, i, 0)), pl.BlockSpec((G, rows, LC), lambda i: (0, i, 0)),
                  full(w_out), full(ln_g), full(ln_b)],
        out_specs=pl.BlockSpec((B, tt, D), lambda i: (0, i, 0)),
        out_shape=jax.ShapeDtypeStruct((B, T, D), F32),
        scratch_shapes=[pltpu.VMEM((D // 128, B * tt, 128), F32)],
        compiler_params=pltpu.CompilerParams(dimension_semantics=("arbitrary",),
                                             vmem_limit_bytes=VMEM_LIMIT_BYTES),
    )(x, yg, w_out, ln_g, ln_b)


def _s5_mixer(x, s0_re, s0_im, w_in, w_out, s5w, ln_g, ln_b):
    B, T, D = x.shape
    toe, pst, qst, apow = s5w
    xg = _s5_in(x, w_in, S5_STEP_TOKENS)
    s0 = jnp.stack([s0_re, s0_im], axis=0).transpose(2, 0, 1, 3)
    yg, sfin = _s5_scan(xg, s0, toe, pst, qst, apow, T // S5_CHUNK, B)
    out = _s5_out(x, yg, w_out, ln_g, ln_b, S5_STEP_TOKENS)
    sfin = sfin.transpose(1, 2, 0, 3)
    return out, sfin[0], sfin[1]


def _block_diag(w):
    H, d, _ = w.shape
    eye = jnp.eye(H, dtype=w.dtype)
    return (eye[:, None, :, None] * w[:, :, None, :]).reshape(H * d, H * d)


def kernel(x_prompt, x_sample, state_lru_h, state_lru_conv, state_hgrn2, state_s5_re, state_s5_im, ln_mix_g, ln_mix_b, ln_ffn_g, ln_ffn_b, w_in_ab, w_out_ab, lru_conv_w, lru_conv_b, lru_w_r, lru_b_r, lru_w_i, lru_b_i, lru_lam, hgrn_lb_logits, hgrn_norm_g, w_in_s5, w_out_s5, s5_a_re, s5_a_im, s5_b_re, s5_b_im, s5_c_re, s5_c_im, s5_d, s5_log_dt, router_w, router_b, moe_w_gate, moe_w_up, moe_w_down):
    Bp, Tp, D = x_prompt.shape
    Bs, Ts, _ = x_sample.shape
    W = state_lru_h.shape[-1]
    dk = W // HG_HEADS
    row = lambda v: v.reshape(1, -1)

    def moe(y, l, sparse):
        fn = _moe_sparse if sparse else functools.partial(_moe_dense, tm=512)
        out = fn(y.reshape(-1, D), router_w, row(router_b), moe_w_gate, moe_w_up, moe_w_down, l,
                 row(ln_ffn_g[l]), row(ln_ffn_b[l]))
        return out.reshape(y.shape)

    j = 0
    wts = (w_in_ab[j].astype(BF16), w_out_ab[j].astype(BF16), lru_conv_w[j], row(lru_conv_b[j]),
           _block_diag(lru_w_r[j]).astype(BF16), row(lru_b_r[j]),
           _block_diag(lru_w_i[j]).astype(BF16), row(lru_b_i[j]),
           row(lru_lam[j]), hgrn_lb_logits, row(hgrn_norm_g[j]), row(ln_mix_g[0]), row(ln_mix_b[0]))
    yp, p_h, p_conv, p_hg = _ab_mixer(
        x_prompt, jnp.zeros((Bp, 1, W), F32), jnp.zeros((Bp, 3, W), F32),
        jnp.zeros((Bp, HG_HEADS, dk, dk), F32), wts, AB_TILE, AB_SEQS)
    ys, s_h, s_conv, s_hg = _ab_mixer(
        x_sample, state_lru_h[j][:, None, :], state_lru_conv[j], state_hgrn2[j], wts, Ts, AB_SEQS)
    yp = moe(yp, 0, True)
    ys = moe(ys, 0, False)

    s5w = _s5_weights(s5_a_re[j], s5_a_im[j], s5_log_dt[j], s5_b_re[j], s5_b_im[j],
                      s5_c_re[j], s5_c_im[j], s5_d[j])
    G, P = s5_a_re[j].shape
    w_in5 = w_in_s5[j].astype(BF16)
    w_out5 = w_out_s5[j].astype(BF16)
    zp = jnp.zeros((Bp, G, P), F32)
    yp, p_re, p_im = _s5_mixer(yp, zp, zp, w_in5, w_out5, s5w, row(ln_mix_g[1]), row(ln_mix_b[1]))
    ys, s_re, s_im = _s5_mixer(ys, state_s5_re[j], state_s5_im[j], w_in5, w_out5, s5w,
                               row(ln_mix_g[1]), row(ln_mix_b[1]))
    yp = moe(yp, 1, True)
    ys = moe(ys, 1, False)

    return (yp, ys,
            p_h.reshape(1, Bp, W), p_conv[None], p_hg[None], p_re[None], p_im[None],
            s_h.reshape(1, Bs, W), s_conv[None], s_hg[None], s_re[None], s_im[None])
```

```python
import functools
import math

import jax
import jax.numpy as jnp
from jax import lax
from jax.experimental import pallas as pl
from jax.experimental.pallas import tpu as pltpu

F32 = jnp.float32
BF16 = jnp.bfloat16
HIGHEST = lax.Precision.HIGHEST

DEPTH = 2
RG_C = 8.0
LRU_HEADS = 8
HG_HEADS = 4
AB_TILE = 512
AB_SEQS = 1
HG_CHUNK = 16
RMS_EPS = 1e-6
S5_GROUP = 16
S5_SCAN_ROWS_PER_STEP = 2048
S5_STEP_TOKENS = 32
S5_CHUNK = 16
N_EXPERT_GROUPS = 4
ALPHA = (2.0 * DEPTH) ** 0.25
LN_EPS = 1e-5
MOE_TILE = 256
MOE_CHUNK = 16
MOE_SLOTS = 768
MOE_BLOCK = 1024
ROUTE_TILES_PER_STEP = 4
VMEM_LIMIT_BYTES = 56 * 1024 * 1024


def _dot(a, b):
    return jnp.dot(a, b, preferred_element_type=F32)


def _dot_nt(a, b):
    return lax.dot_general(a, b, (((1,), (1,)), ((), ())), preferred_element_type=F32)


def _sigmoid(x):
    return 0.5 * jnp.tanh(0.5 * x) + 0.5


def _gelu_tanh(x):
    c = math.sqrt(2.0 / math.pi)
    return 0.5 * x * (1.0 + jnp.tanh(c * (x + 0.044715 * (x * x * x))))


def _layer_norm(s, g, b):
    mu = jnp.mean(s, axis=-1, keepdims=True)
    d = s - mu
    var = jnp.mean(d * d, axis=-1, keepdims=True)
    return d * lax.rsqrt(var + LN_EPS) * g + b


def _shift_rows(x, s, row):
    del row
    return pltpu.roll(x, s, axis=0)


def _ab_kernel(x_ref, h0_ref, conv0_ref, s0_ref, w_in_ref, w_out_ref, conv_w_ref, conv_b_ref,
               wr_ref, br_ref, wi_ref, bi_ref, lam_ref, lb_logits_ref, norm_g_ref, ln_g_ref, ln_b_ref,
               y_ref, h_out_ref, conv_out_ref, s_out_ref,
               zx_buf, h_carry, st_carry, st_all, *, tb, width, nb):
    for si in range(nb):
        _ab_sequence_tile(si, x_ref, h0_ref, conv0_ref, s0_ref, w_in_ref, w_out_ref, conv_w_ref, conv_b_ref,
                          wr_ref, br_ref, wi_ref, bi_ref, lam_ref, lb_logits_ref, norm_g_ref, ln_g_ref, ln_b_ref,
                          y_ref, h_out_ref, conv_out_ref, s_out_ref,
                          zx_buf.at[si], h_carry.at[si], st_carry.at[si], st_all.at[si], tb, width)


def _ab_sequence_tile(si, x_ref, h0_ref, conv0_ref, s0_ref, w_in_ref, w_out_ref, conv_w_ref, conv_b_ref,
                      wr_ref, br_ref, wi_ref, bi_ref, lam_ref, lb_logits_ref, norm_g_ref, ln_g_ref, ln_b_ref,
                      y_ref, h_out_ref, conv_out_ref, s_out_ref, zx_buf, h_carry, st_carry, st_all, tb, width):
    t = pl.program_id(1)
    nt = pl.num_programs(1)
    W = width
    L = HG_CHUNK
    nc = tb // L
    dk = W // HG_HEADS

    @pl.when(t == 0)
    def _init():
        zx_buf[0:8, :] = jnp.zeros((8, W), F32)
        zx_buf[5:8, :] = conv0_ref[si]
        h_carry[...] = jnp.broadcast_to(h0_ref[si], h_carry.shape)
        for hd in range(HG_HEADS):
            st_carry[hd] = s0_ref[si, hd].T

    x = x_ref[si]
    z = _dot(x.astype(BF16), w_in_ref[...])
    z_gate = z[:, 0 * W:1 * W]
    z_x = z[:, 1 * W:2 * W]
    z_q = z[:, 2 * W:3 * W]
    z_f = z[:, 3 * W:4 * W]
    z_i = z[:, 4 * W:5 * W]
    z_g = z[:, 5 * W:6 * W]

    zx_buf[8:8 + tb, :] = z_x
    cw = conv_w_ref[...]
    u = (conv_b_ref[...] + cw[3:4, :] * z_x
         + cw[2:3, :] * zx_buf[7:7 + tb, :]
         + cw[1:2, :] * zx_buf[6:6 + tb, :]
         + cw[0:1, :] * zx_buf[5:5 + tb, :])
    tail = zx_buf[tb:tb + 8, :]
    zx_buf[0:8, :] = tail

    ub = u.astype(BF16)
    r = _sigmoid(_dot(ub, wr_ref[...]) + br_ref[...])
    ig = _sigmoid(_dot(ub, wi_ref[...]) + bi_ref[...])
    nlam = -lam_ref[...]
    softplus_nlam = jnp.maximum(nlam, 0.0) + jnp.log1p(jnp.exp(-jnp.abs(nlam)))
    log_a = (-RG_C) * r * softplus_nlam
    a = jnp.exp(log_a)
    th = jnp.tanh(log_a)
    b = jnp.sqrt(-2.0 * th / (1.0 - th)) * ig * u

    row = lax.broadcasted_iota(jnp.int32, (tb, W), 0)
    rin8 = row % 8
    s = 1
    while s < 8:
        m = rin8 >= s
        a_sh = _shift_rows(a, s, row)
        b_sh = _shift_rows(b, s, row)
        b = jnp.where(m, a * b_sh + b, b)
        a = jnp.where(m, a * a_sh, a)
        s *= 2
    a3 = a.reshape(tb // 8, 8, W)
    b3 = b.reshape(tb // 8, 8, W)
    h_prev = h_carry[0:1, :]
    h_groups = []
    for g in range(tb // 8):
        hg = b3[g] + a3[g] * h_prev
        h_groups.append(hg)
        h_prev = hg[7:8, :]
    h = jnp.concatenate(h_groups, axis=0)
    h_carry[...] = jnp.broadcast_to(h_prev, h_carry.shape)
    out_a = h * _gelu_tanh(z_gate)

    lg = lb_logits_ref[...]
    e = jnp.exp(lg - jnp.max(lg, axis=0, keepdims=True))
    lb = e[0:1, :] / jnp.sum(e, axis=0, keepdims=True)
    f = lb + (1.0 - lb) * _sigmoid(z_f)
    log_f = jnp.log(f)
    kk = 1.0 - f
    rin = row % L
    G = log_f
    s = 1
    while s < L:
        G = G + jnp.where(rin >= s, _shift_rows(G, s, row), 0.0)
        s *= 2
    G3 = G.reshape(nc, L, W)
    G_last = G3[:, L - 1:L, :]
    q_dec = (z_q * jnp.exp(G)).reshape(nc, L, W).astype(BF16)
    k_inc = (kk * jnp.exp(-G)).reshape(nc, L, W).astype(BF16)
    k_tail = (kk.reshape(nc, L, W) * jnp.exp(G_last - G3)).astype(BF16)
    decay = jnp.exp(G_last)
    v3 = z_i.reshape(nc, L, W).astype(BF16)
    causal = (lax.broadcasted_iota(jnp.int32, (nc, L, L), 1)
              >= lax.broadcasted_iota(jnp.int32, (nc, L, L), 2))

    o_heads = []
    for hd in range(HG_HEADS):
        sl = slice(hd * dk, (hd + 1) * dk)
        qd, ki, kt, vv = q_dec[:, :, sl], k_inc[:, :, sl], k_tail[:, :, sl], v3[:, :, sl]
        scores = jnp.einsum('ntk,nsk->nts', qd, ki, preferred_element_type=F32)
        scores = jnp.where(causal, scores, 0.0).astype(BF16)
        o_intra = jnp.einsum('nts,nsv->ntv', scores, vv, preferred_element_type=F32)
        vt = jnp.swapaxes(vv.astype(F32), 1, 2).astype(BF16)
        upd = jnp.einsum('nvs,nsk->nvk', vt, kt, preferred_element_type=F32)
        st = st_carry[hd]
        for n in range(nc):
            st_all[hd, n] = st.astype(BF16)
            st = st * decay[n, :, sl] + upd[n]
        st_carry[hd] = st
        o_inter = jnp.einsum('ntk,nvk->ntv', qd, st_all[hd], preferred_element_type=F32)
        o = (o_intra + o_inter).reshape(tb, dk)
        o = o * lax.rsqrt(jnp.mean(o * o, axis=-1, keepdims=True) + RMS_EPS)
        o_heads.append(o)
    o_all = jnp.concatenate(o_heads, axis=-1) * norm_g_ref[...]
    out_b = o_all * (z_g * _sigmoid(z_g))

    mixed = jnp.concatenate([out_a, out_b], axis=-1).astype(BF16)
    y = _dot(mixed, w_out_ref[...])
    y_ref[si] = _layer_norm(ALPHA * x + y, ln_g_ref[...], ln_b_ref[...])

    @pl.when(t == nt - 1)
    def _fin():
        h_out_ref[si] = h[tb - 1:tb, :]
        conv_out_ref[si] = tail[5:8, :]
        for hd in range(HG_HEADS):
            s_out_ref[si, hd] = st_carry[hd].T


def _ab_mixer(x, h0, conv0, s0, wts, tb, nb):
    B, T, D = x.shape
    W = h0.shape[-1]
    dk = W // HG_HEADS
    nt = T // tb
    nc = tb // HG_CHUNK
    (w_in, w_out, conv_w, conv_b, wr, br, wi, bi, lam, lb_logits, norm_g, ln_g, ln_b) = wts
    full = lambda a: pl.BlockSpec(a.shape, lambda b, t: (0,) * a.ndim)
    per_b = lambda shp: pl.BlockSpec((nb,) + shp, lambda b, t: (b,) + (0,) * len(shp))
    out_shape = (jax.ShapeDtypeStruct((B, T, D), F32),
                 jax.ShapeDtypeStruct((B, 1, W), F32),
                 jax.ShapeDtypeStruct((B, 3, W), F32),
                 jax.ShapeDtypeStruct((B, HG_HEADS, dk, dk), F32))
    return pl.pallas_call(
        functools.partial(_ab_kernel, tb=tb, width=W, nb=nb),
        grid=(B // nb, nt),
        in_specs=[pl.BlockSpec((nb, tb, D), lambda b, t: (b, t, 0)),
                  per_b((1, W)), per_b((3, W)), per_b((HG_HEADS, dk, dk))]
                 + [full(a) for a in wts],
        out_specs=(pl.BlockSpec((nb, tb, D), lambda b, t: (b, t, 0)),
                   per_b((1, W)), per_b((3, W)), per_b((HG_HEADS, dk, dk))),
        out_shape=out_shape,
        scratch_shapes=[pltpu.VMEM((nb, tb + 8, W), F32),
                        pltpu.VMEM((nb, 8, W), F32),
                        pltpu.VMEM((nb, HG_HEADS, dk, dk), F32),
                        pltpu.VMEM((nb, HG_HEADS, nc, dk, dk), BF16)],
        compiler_params=pltpu.CompilerParams(
            dimension_semantics=("arbitrary", "arbitrary"),
            vmem_limit_bytes=VMEM_LIMIT_BYTES),
    )(x, h0, conv0, s0, *wts)


def _router(x, rw, rb, n_experts):
    per_group = n_experts // N_EXPERT_GROUPS
    logits = jnp.dot(x, rw, preferred_element_type=F32, precision=HIGHEST) + rb
    logits = logits - jnp.max(logits, axis=-1, keepdims=True)
    ex = jnp.exp(logits)
    p = ex / jnp.sum(ex, axis=-1, keepdims=True)
    lane_i = lax.broadcasted_iota(jnp.int32, p.shape, 1)
    grp = lane_i // per_group
    lane = lane_i.astype(F32)
    big = float(n_experts)

    def top2(pm):
        m1 = jnp.max(pm, axis=-1, keepdims=True)
        i1 = jnp.min(jnp.where(pm == m1, lane, big), axis=-1, keepdims=True)
        pm2 = jnp.where(lane == i1, -1.0, pm)
        m2 = jnp.max(pm2, axis=-1, keepdims=True)
        i2 = jnp.min(jnp.where(pm2 == m2, lane, big), axis=-1, keepdims=True)
        return m1, i1, m2, i2

    best = None
    g_sel = None
    for j in range(N_EXPERT_GROUPS):
        m1, _, m2, _ = top2(jnp.where(grp == j, p, -1.0))
        score = m1 + m2
        if best is None:
            best, g_sel = score, jnp.zeros_like(lane_i[:, :1])
        else:
            better = score > best
            best = jnp.where(better, score, best)
            g_sel = jnp.where(better, j, g_sel)
    m1, i1, m2, i2 = top2(jnp.where(grp == g_sel, p, -1.0))
    den = m1 + m2
    return jnp.where(lane == i1, m1 / den, 0.0) + jnp.where(lane == i2, m2 / den, 0.0)


def _moe_dense_kernel(x_ref, rw_ref, rb_ref, wg_ref, wu_ref, wd_ref, ln_g_ref, ln_b_ref,
                      y_ref, xb_scr, comb_scr, acc_scr, *, n_experts):
    e = pl.program_id(1)

    @pl.when(e == 0)
    def _init():
        x = x_ref[...]
        xb_scr[...] = x.astype(BF16)
        comb_scr[...] = _router(x, rw_ref[...], rb_ref[...], n_experts)
        acc_scr[...] = jnp.zeros_like(acc_scr)

    comb = comb_scr[...]
    lane = lax.broadcasted_iota(jnp.int32, comb.shape, 1)
    c_e = jnp.sum(jnp.where(lane == e, comb, 0.0), axis=-1, keepdims=True)
    xb = xb_scr[...]
    g = _dot(xb, wg_ref[...].astype(BF16))
    up = _dot(xb, wu_ref[...].astype(BF16))
    hmid = (g * _sigmoid(g) * up).astype(BF16)
    acc_scr[...] += c_e * _dot(hmid, wd_ref[...].astype(BF16))

    @pl.when(e == n_experts - 1)
    def _fin():
        y_ref[...] = _layer_norm(ALPHA * x_ref[...] + acc_scr[...], ln_g_ref[...], ln_b_ref[...])


def _moe_dense(x2d, rw, rb, wg, wu, wd, layer, ln_g, ln_b, tm):
    N, D = x2d.shape
    _, E, _, De = wg.shape
    full = lambda a: pl.BlockSpec(a.shape, lambda i, e: (0,) * a.ndim)
    return pl.pallas_call(
        functools.partial(_moe_dense_kernel, n_experts=E),
        grid=(N // tm, E),
        in_specs=[pl.BlockSpec((tm, D), lambda i, e: (i, 0)), full(rw), full(rb),
                  pl.BlockSpec((None, None, D, De), lambda i, e: (layer, e, 0, 0)),
                  pl.BlockSpec((None, None, D, De), lambda i, e: (layer, e, 0, 0)),
                  pl.BlockSpec((None, None, De, D), lambda i, e: (layer, e, 0, 0)),
                  full(ln_g), full(ln_b)],
        out_specs=pl.BlockSpec((tm, D), lambda i, e: (i, 0)),
        out_shape=jax.ShapeDtypeStruct((N, D), F32),
        scratch_shapes=[pltpu.VMEM((tm, D), BF16), pltpu.VMEM((tm, E), F32), pltpu.VMEM((tm, D), F32)],
        compiler_params=pltpu.CompilerParams(
            dimension_semantics=("arbitrary", "arbitrary"),
            vmem_limit_bytes=VMEM_LIMIT_BYTES),
    )(x2d, rw, rb, wg, wu, wd, ln_g, ln_b)


def _route_kernel(x_ref, rwt_ref, rbc_ref, meta_ref, pc_ref, *, n_experts):
    E = n_experts
    per_group = E // N_EXPERT_GROUPS
    x = x_ref[...]
    tm = x.shape[0]
    logits = lax.dot_general(rwt_ref[...], x, (((1,), (1,)), ((), ())),
                             precision=HIGHEST, preferred_element_type=F32) + rbc_ref[...]
    logits = logits - jnp.max(logits, axis=0, keepdims=True)
    ex = jnp.exp(logits)
    p = ex / jnp.sum(ex, axis=0, keepdims=True)
    eid_i = lax.broadcasted_iota(jnp.int32, (E, tm), 0)
    grp = eid_i // per_group
    eid = eid_i.astype(F32)
    big = float(E)

    def top2(pm):
        m1 = jnp.max(pm, axis=0, keepdims=True)
        i1 = jnp.min(jnp.where(pm == m1, eid, big), axis=0, keepdims=True)
        pm2 = jnp.where(eid == i1, -1.0, pm)
        m2 = jnp.max(pm2, axis=0, keepdims=True)
        i2 = jnp.min(jnp.where(pm2 == m2, eid, big), axis=0, keepdims=True)
        return m1, i1, m2, i2

    best = None
    g_sel = None
    for j in range(N_EXPERT_GROUPS):
        m1, _, m2, _ = top2(jnp.where(grp == j, p, -1.0))
        score = m1 + m2
        if best is None:
            best, g_sel = score, jnp.zeros_like(eid_i[:1, :])
        else:
            better = score > best
            best = jnp.where(better, score, best)
            g_sel = jnp.where(better, j, g_sel)
    m1, i1, m2, i2 = top2(jnp.where(grp == g_sel, p, -1.0))
    den = m1 + m2

    oh1_all = jnp.where(eid == i1, 1.0, 0.0)
    oh2_all = jnp.where(eid == i2, 1.0, 0.0)
    g1_all = m1 / den
    g2_all = m2 / den
    tl = MOE_TILE
    earlier = jnp.where(lax.broadcasted_iota(jnp.int32, (tl, tl), 0)
                        < lax.broadcasted_iota(jnp.int32, (tl, tl), 1), 1.0, 0.0).astype(BF16)
    below = jnp.where(lax.broadcasted_iota(jnp.int32, (E, E), 1)
                      < lax.broadcasted_iota(jnp.int32, (E, E), 0), 1.0, 0.0)
    for k in range(tm // tl):
        lanes = slice(k * tl, (k + 1) * tl)
        oh1, oh2 = oh1_all[:, lanes], oh2_all[:, lanes]
        r1 = _dot(oh1.astype(BF16), earlier)
        r2 = _dot(oh2.astype(BF16), earlier)
        cnt1 = jnp.sum(oh1, axis=1, keepdims=True)
        cnt2 = jnp.sum(oh2, axis=1, keepdims=True)
        pc = jnp.floor((cnt1 + cnt2 + (MOE_CHUNK - 1.0)) * (1.0 / MOE_CHUNK)) * MOE_CHUNK
        pc_b = jnp.broadcast_to(pc, (E, 128))
        run_off = jnp.dot(below, pc_b, precision=HIGHEST, preferred_element_type=F32)[:, 0:1]
        meta_ref[k, 0:1, :] = jnp.sum(oh1 * (run_off + r1), axis=0, keepdims=True)
        meta_ref[k, 1:2, :] = jnp.sum(oh2 * (run_off + cnt1 + r2), axis=0, keepdims=True)
        meta_ref[k, 2:3, :] = g1_all[:, lanes]
        meta_ref[k, 3:4, :] = g2_all[:, lanes]
        meta_ref[k, 4:8, :] = jnp.zeros((4, tl), F32)
        pc_ref[k] = pc_b


def _route(x2d, rwt, rbc):
    N, D = x2d.shape
    E = rwt.shape[0]
    nt = N // MOE_TILE
    tps = ROUTE_TILES_PER_STEP if nt % ROUTE_TILES_PER_STEP == 0 else 1
    full = lambda a: pl.BlockSpec(a.shape, lambda i: (0,) * a.ndim)
    return pl.pallas_call(
        functools.partial(_route_kernel, n_experts=E),
        grid=(nt // tps,),
        in_specs=[pl.BlockSpec((tps * MOE_TILE, D), lambda i: (i, 0)), full(rwt), full(rbc)],
        out_specs=(pl.BlockSpec((tps, 8, MOE_TILE), lambda i: (i, 0, 0)),
                   pl.BlockSpec((tps, E, 128), lambda i: (i, 0, 0))),
        out_shape=(jax.ShapeDtypeStruct((nt, 8, MOE_TILE), F32), jax.ShapeDtypeStruct((nt, E, 128), F32)),
        compiler_params=pltpu.CompilerParams(dimension_semantics=("arbitrary",),
                                             vmem_limit_bytes=VMEM_LIMIT_BYTES),
    )(x2d, rwt, rbc)


def _moe_schedule(pc, n_blocks_max):
    nt, E = pc.shape
    cum = jnp.cumsum(pc, axis=1)
    run_off = cum - pc
    total = cum[:, -1]
    seg_len = jnp.sum(pc, axis=0)
    seg_pad = ((seg_len + MOE_BLOCK - 1) // MOE_BLOCK) * MOE_BLOCK
    seg_end = jnp.cumsum(seg_pad)
    goff = (seg_end - seg_pad)[None, :] + (jnp.cumsum(pc, axis=0) - pc)
    r = jnp.arange(MOE_SLOTS // MOE_CHUNK, dtype=jnp.int32) * MOE_CHUNK
    e_c = jnp.minimum(jnp.sum((cum[:, None, :] <= r[None, :, None]).astype(jnp.int32), axis=-1), E - 1)
    pick = e_c[:, :, None] == jnp.arange(E, dtype=jnp.int32)[None, None, :]
    dst_row = r[None, :] + jnp.sum(jnp.where(pick, (goff - run_off)[:, None, :], 0), axis=-1)
    dst_chunk = jnp.where(r[None, :] < total[:, None], dst_row // MOE_CHUNK, -1).astype(jnp.int32)
    blk_start = jnp.arange(n_blocks_max, dtype=jnp.int32) * MOE_BLOCK
    block_expert = jnp.minimum(jnp.sum((seg_end[None, :] <= blk_start[:, None]).astype(jnp.int32), axis=-1), E - 1)
    n_valid = (seg_end[-1] // MOE_BLOCK).reshape(1).astype(jnp.int32)
    n_chunks_all = n_blocks_max * (MOE_BLOCK // MOE_CHUNK)
    n_fill_max = n_chunks_all - (nt * 2 * MOE_TILE) // MOE_CHUNK
    gap_start = jnp.concatenate([seg_end - seg_pad + seg_len, seg_end[-1:]]) // MOE_CHUNK
    gap_len = jnp.concatenate([seg_pad - seg_len, n_chunks_all * MOE_CHUNK - seg_end[-1:]]) // MOE_CHUNK
    gap_cum = jnp.cumsum(gap_len)
    k = jnp.arange(n_fill_max, dtype=jnp.int32)
    which = jnp.minimum(jnp.sum((gap_cum[None, :] <= k[:, None]).astype(jnp.int32), axis=-1), E)
    pick_gap = which[:, None] == jnp.arange(E + 1, dtype=jnp.int32)[None, :]
    fill = k + jnp.sum(jnp.where(pick_gap, (gap_start - (gap_cum - gap_len))[None, :], 0), axis=-1)
    fill = jnp.where(k < gap_cum[-1], fill, 0).astype(jnp.int32)
    n_fill = gap_cum[-1].reshape(1).astype(jnp.int32)
    return (dst_chunk.reshape(-1), total.astype(jnp.int32), block_expert.astype(jnp.int32), n_valid, fill, n_fill)


def _chunk_copy(hbm_ref, buf, sem, b, c, d, to_hbm):
    local = buf.at[b, pl.ds(c * MOE_CHUNK, MOE_CHUNK)]
    remote = hbm_ref.at[pl.ds(pl.multiple_of(d * MOE_CHUNK, MOE_CHUNK), MOE_CHUNK)]
    if to_hbm:
        return pltpu.make_async_copy(local, remote, sem.at[b])
    return pltpu.make_async_copy(remote, local, sem.at[b])


def _for_tile_chunks(dst_ref, tile, fn):
    n_chunks = MOE_SLOTS // MOE_CHUNK
    for c in range(n_chunks):
        d = dst_ref[tile * n_chunks + c]

        @pl.when(d >= 0)
        def _():
            fn(c, d)


def _dispatch_kernel(dst_ref, fill_ref, nfill_ref, x_ref, meta_ref, xs_hbm, buf, sem, zbuf, zsem):
    i = pl.program_id(0)
    n = pl.num_programs(0)
    b = i % 2
    tm = x_ref.shape[0]

    def wait_tile(tile, slot):
        _for_tile_chunks(dst_ref, tile, lambda c, d: _chunk_copy(xs_hbm, buf, sem, slot, c, d, True).wait())

    def zero_copy(k):
        d = fill_ref[k]
        return pltpu.make_async_copy(
            zbuf, xs_hbm.at[pl.ds(pl.multiple_of(d * MOE_CHUNK, MOE_CHUNK), MOE_CHUNK)], zsem.at[0])

    @pl.when(i == 0)
    def _():
        zbuf[...] = jnp.zeros_like(zbuf)
        lax.fori_loop(0, nfill_ref[0], lambda k, c: (zero_copy(k).start(), c)[1], 0)

    @pl.when(i >= 2)
    def _():
        wait_tile(i - 2, b)

    s1 = meta_ref[0:1, :].astype(jnp.int32)
    s2 = meta_ref[1:2, :].astype(jnp.int32)
    rows = lax.broadcasted_iota(jnp.int32, (MOE_SLOTS, tm), 0)
    dsp = jnp.where((rows == s1) | (rows == s2), 1.0, 0.0).astype(BF16)
    buf[b] = _dot(dsp, x_ref[...].astype(BF16)).astype(BF16)
    _for_tile_chunks(dst_ref, i, lambda c, d: _chunk_copy(xs_hbm, buf, sem, b, c, d, True).start())

    @pl.when(i == n - 1)
    def _():
        wait_tile(i, b)

    @pl.when((i == n - 1) & (i >= 1))
    def _():
        wait_tile(i - 1, 1 - b)

    @pl.when(i == n - 1)
    def _():
        lax.fori_loop(0, nfill_ref[0], lambda k, c: (zero_copy(k).wait(), c)[1], 0)


def _dispatch(x2d, meta, dst_chunk, fill, n_fill, n_rows):
    N, D = x2d.shape
    nt = N // MOE_TILE
    return pl.pallas_call(
        _dispatch_kernel,
        grid_spec=pltpu.PrefetchScalarGridSpec(
            num_scalar_prefetch=3, grid=(nt,),
            in_specs=[pl.BlockSpec((MOE_TILE, D), lambda i, *_: (i, 0)),
                      pl.BlockSpec((None, 8, MOE_TILE), lambda i, *_: (i, 0, 0))],
            out_specs=pl.BlockSpec(memory_space=pl.ANY),
            scratch_shapes=[pltpu.VMEM((2, MOE_SLOTS, D), BF16), pltpu.SemaphoreType.DMA((2,)),
                            pltpu.VMEM((MOE_CHUNK, D), BF16), pltpu.SemaphoreType.DMA((1,))]),
        out_shape=jax.ShapeDtypeStruct((n_rows, D), BF16),
        compiler_params=pltpu.CompilerParams(dimension_semantics=("arbitrary",),
                                             vmem_limit_bytes=VMEM_LIMIT_BYTES),
    )(dst_chunk, fill, n_fill, x2d, meta)


def _ffn_kernel(be_ref, nv_ref, xs_ref, wg_ref, wu_ref, wd_ref, ys_ref, wg_b, wu_b, wd_b):
    j = pl.program_id(0)
    used = j < nv_ref[0]
    new_expert = (j == 0) | (be_ref[j] != be_ref[jnp.maximum(j - 1, 0)])

    @pl.when(used & new_expert)
    def _():
        wg_b[...] = wg_ref[...].astype(BF16)
        wu_b[...] = wu_ref[...].astype(BF16)
        wd_b[...] = wd_ref[...].astype(BF16)

    @pl.when(used)
    def _():
        xb = xs_ref[...]
        g = _dot(xb, wg_b[...])
        up = _dot(xb, wu_b[...])
        hmid = (g * _sigmoid(g) * up).astype(BF16)
        ys_ref[...] = _dot(hmid, wd_b[...]).astype(ys_ref.dtype)

    @pl.when(jnp.logical_not(used))
    def _():
        ys_ref[...] = jnp.zeros_like(ys_ref)


def _ffn(xs, wg, wu, wd, layer, block_expert, n_valid):
    R, D = xs.shape
    _, E, _, De = wg.shape
    last = lambda j, nv: jnp.maximum(jnp.minimum(j, nv[0] - 1), 0)
    return pl.pallas_call(
        _ffn_kernel,
        grid_spec=pltpu.PrefetchScalarGridSpec(
            num_scalar_prefetch=2, grid=(R // MOE_BLOCK,),
            in_specs=[pl.BlockSpec((MOE_BLOCK, D), lambda j, be, nv: (last(j, nv), 0)),
                      pl.BlockSpec((None, None, D, De), lambda j, be, nv: (layer, be[last(j, nv)], 0, 0)),
                      pl.BlockSpec((None, None, D, De), lambda j, be, nv: (layer, be[last(j, nv)], 0, 0)),
                      pl.BlockSpec((None, None, De, D), lambda j, be, nv: (layer, be[last(j, nv)], 0, 0))],
            out_specs=pl.BlockSpec((MOE_BLOCK, D), lambda j, be, nv: (j, 0)),
            scratch_shapes=[pltpu.VMEM((D, De), BF16), pltpu.VMEM((D, De), BF16), pltpu.VMEM((De, D), BF16)]),
        out_shape=jax.ShapeDtypeStruct((R, D), BF16),
        compiler_params=pltpu.CompilerParams(dimension_semantics=("arbitrary",),
                                             vmem_limit_bytes=VMEM_LIMIT_BYTES),
    )(block_expert, n_valid, xs, wg, wu, wd)


def _combine_kernel(dst_ref, tot_ref, x_ref, metac_ref, ys_hbm, ln_g_ref, ln_b_ref, o_ref, buf, sem):
    i = pl.program_id(0)
    n = pl.num_programs(0)
    b = i % 2
    tm = x_ref.shape[0]

    def fetch(tile, slot):
        _for_tile_chunks(dst_ref, tile, lambda c, d: _chunk_copy(ys_hbm, buf, sem, slot, c, d, False).start())

    @pl.when(i == 0)
    def _():
        fetch(0, 0)

    @pl.when(i + 1 < n)
    def _():
        fetch(i + 1, 1 - b)

    _for_tile_chunks(dst_ref, i, lambda c, d: _chunk_copy(ys_hbm, buf, sem, b, c, d, False).wait())
    live = lax.broadcasted_iota(jnp.int32, (MOE_SLOTS, 1), 0) < tot_ref[i]
    ys = jnp.where(live, buf[b], jnp.zeros((), BF16))
    mc = metac_ref[...]
    s1 = mc[:, 0:1].astype(jnp.int32)
    s2 = mc[:, 1:2].astype(jnp.int32)
    lane = lax.broadcasted_iota(jnp.int32, (tm, MOE_SLOTS), 1)
    cmb = (jnp.where(lane == s1, mc[:, 2:3], 0.0) + jnp.where(lane == s2, mc[:, 3:4], 0.0)).astype(BF16)
    o_ref[...] = _layer_norm(ALPHA * x_ref[...] + _dot(cmb, ys), ln_g_ref[...], ln_b_ref[...])


def _combine(x2d, meta_col, ys, dst_chunk, total, ln_g, ln_b):
    N, D = x2d.shape
    nt = N // MOE_TILE
    return pl.pallas_call(
        _combine_kernel,
        grid_spec=pltpu.PrefetchScalarGridSpec(
            num_scalar_prefetch=2, grid=(nt,),
            in_specs=[pl.BlockSpec((MOE_TILE, D), lambda i, dst, tot: (i, 0)),
                      pl.BlockSpec((MOE_TILE, 8), lambda i, dst, tot: (i, 0)),
                      pl.BlockSpec(memory_space=pl.ANY),
                      pl.BlockSpec(ln_g.shape, lambda i, dst, tot: (0, 0)),
                      pl.BlockSpec(ln_b.shape, lambda i, dst, tot: (0, 0))],
            out_specs=pl.BlockSpec((MOE_TILE, D), lambda i, dst, tot: (i, 0)),
            scratch_shapes=[pltpu.VMEM((2, MOE_SLOTS, D), BF16), pltpu.SemaphoreType.DMA((2,))]),
        out_shape=jax.ShapeDtypeStruct((N, D), F32),
        compiler_params=pltpu.CompilerParams(dimension_semantics=("arbitrary",),
                                             vmem_limit_bytes=VMEM_LIMIT_BYTES),
    )(dst_chunk, total, x2d, meta_col, ys, ln_g, ln_b)


def _moe_sparse(x2d, rw, rb, wg, wu, wd, layer, ln_g, ln_b):
    N, D = x2d.shape
    E = rw.shape[1]
    nt = N // MOE_TILE
    meta, pc = _route(x2d, rw.T, rb.reshape(E, 1))
    pc = pc[:, :, 0].astype(jnp.int32)
    n_rows = nt * MOE_SLOTS + E * MOE_BLOCK
    assert n_rows % MOE_BLOCK == 0
    dst_chunk, total, block_expert, n_valid, fill, n_fill = _moe_schedule(pc, n_rows // MOE_BLOCK)
    xs = _dispatch(x2d, meta, dst_chunk, fill, n_fill, n_rows)
    ys = _ffn(xs, wg, wu, wd, layer, block_expert, n_valid)
    meta_col = jnp.swapaxes(meta, 1, 2).reshape(N, 8)
    return _combine(x2d, meta_col, ys, dst_chunk, total, ln_g, ln_b)


def _s5_weights_kernel(lam_col_ref, lam_row_ref, bt_ref, ct_ref, d_ref,
                       toe_ref, pst_ref, qst_ref, a_pow_ref):
    L, Cg = S5_CHUNK, S5_GROUP
    LC = L * Cg
    P = lam_row_ref.shape[-1]

    def discretise(a_re, a_im, log_dt):
        dt = jnp.exp(log_dt)
        lr, li = dt * a_re, dt * a_im
        er = jnp.exp(lr)
        abr, abi = er * jnp.cos(li), er * jnp.sin(li)
        nr, ni = abr - 1.0, abi
        den = a_re * a_re + a_im * a_im
        cr = (nr * a_re + ni * a_im) / den
        ci = (ni * a_re - nr * a_im) / den
        return lr, li, cr, ci

    def a_power(lr, li, tau):
        er = jnp.exp(tau * lr)
        return er * jnp.cos(tau * li), er * jnp.sin(tau * li)

    lc = lam_col_ref[...]
    lr_c, li_c, _, _ = discretise(lc[:, 0:1], lc[:, 1:2], lc[:, 2:3])
    lw = lam_row_ref[...]
    lr_r, li_r, cr_r, ci_r = discretise(lw[0:1, :], lw[1:2, :], lw[2:3, :])

    btr, bti = bt_ref[0], bt_ref[1]
    bbr = cr_r * btr - ci_r * bti
    bbi = cr_r * bti + ci_r * btr

    ctr, cti = ct_ref[0], ct_ref[1]
    tau_l = (lax.broadcasted_iota(jnp.int32, (P, LC), 1) // Cg).astype(F32)

    pr, pi = a_power(lr_c, li_c, tau_l)
    car = ctr * pr - cti * pi
    cai = ctr * pi + cti * pr
    t0 = (jnp.dot(bbr[0:Cg, :], car, preferred_element_type=F32, precision=HIGHEST)
          - jnp.dot(bbi[0:Cg, :], cai, preferred_element_type=F32, precision=HIGHEST))
    r16 = lax.broadcasted_iota(jnp.int32, (Cg, LC), 0)
    l16 = lax.broadcasted_iota(jnp.int32, (Cg, LC), 1)
    t0 = t0 + jnp.where(r16 == l16, d_ref[...], 0.0)
    toe_ref[0:Cg, :] = t0.astype(toe_ref.dtype)
    for s in range(1, L):
        toe_ref[s * Cg:(s + 1) * Cg, :] = jnp.where(
            l16 >= s * Cg, pltpu.roll(t0, s * Cg, axis=1), 0.0).astype(toe_ref.dtype)

    pr, pi = a_power(lr_c, li_c, tau_l + 1.0)
    qst_ref[0:P, :] = (ctr * pr - cti * pi).astype(qst_ref.dtype)
    qst_ref[P:2 * P, :] = (-(ctr * pi + cti * pr)).astype(qst_ref.dtype)

    tau_r = (L - 1 - lax.broadcasted_iota(jnp.int32, (LC, P), 0) // Cg).astype(F32)
    pr, pi = a_power(lr_r, li_r, tau_r)
    pst_ref[:, 0:P] = (bbr * pr - bbi * pi).astype(pst_ref.dtype)
    pst_ref[:, P:2 * P] = (bbr * pi + bbi * pr).astype(pst_ref.dtype)

    pr, pi = a_power(lr_r, li_r, float(L))
    a_pow_ref[0:1, :] = pr
    a_pow_ref[1:2, :] = pi


def _s5_weights(a_re, a_im, log_dt, b_re, b_im, c_re, c_im, d):
    G, P = a_re.shape
    Cg, L = S5_GROUP, S5_CHUNK
    LC = L * Cg
    lam = jnp.stack([a_re, a_im, jnp.broadcast_to(log_dt[:, None], (G, P)), jnp.zeros((G, P), F32)], axis=1)
    lam_row = lam
    lam_col = jnp.swapaxes(lam, 1, 2)
    bt = jnp.stack([jnp.swapaxes(b_re, 1, 2), jnp.swapaxes(b_im, 1, 2)], axis=1)
    bt = jnp.tile(bt, (1, 1, L, 1))
    ct = jnp.stack([jnp.swapaxes(c_re, 1, 2), jnp.swapaxes(c_im, 1, 2)], axis=1)
    ct = jnp.tile(ct, (1, 1, 1, L))
    d_col = d.reshape(G, Cg, 1)
    blk = lambda shp: pl.BlockSpec((None,) + shp, lambda g: (g,) + (0,) * len(shp))
    return pl.pallas_call(
        _s5_weights_kernel,
        grid=(G,),
        in_specs=[blk((P, 4)), blk((4, P)), blk((2, LC, P)), blk((2, P, LC)), blk((Cg, 1))],
        out_specs=(blk((LC, LC)), blk((LC, 2 * P)), blk((2 * P, LC)), blk((2, P))),
        out_shape=(jax.ShapeDtypeStruct((G, LC, LC), BF16),
                   jax.ShapeDtypeStruct((G, LC, 2 * P), BF16),
                   jax.ShapeDtypeStruct((G, 2 * P, LC), BF16),
                   jax.ShapeDtypeStruct((G, 2, P), F32)),
        compiler_params=pltpu.CompilerParams(dimension_semantics=("arbitrary",)),
    )(lam_col, lam_row, bt, ct, d_col)


def _granule_transpose(arrs):
    n = len(arrs)
    granule = lax.broadcasted_iota(jnp.int32, arrs[0].shape, 1) // S5_GROUP
    arrs = list(arrs)
    d = n // 2
    while d >= 1:
        upper = (granule & d) != 0
        for i in range(n):
            if i & d:
                continue
            lo, hi = arrs[i], arrs[i + d]
            arrs[i] = jnp.where(upper, pltpu.roll(hi, d * S5_GROUP, axis=1), lo)
            arrs[i + d] = jnp.where(upper, hi, pltpu.roll(lo, (n - d) * S5_GROUP, axis=1))
        d //= 2
    return arrs


def _s5_in_kernel(x_ref, w_ref, o_ref, u_scr):
    B, tt, D = x_ref.shape
    L, Cg = S5_CHUNK, S5_GROUP
    gpb = 128 // Cg
    cpb = tt // L
    u = _dot(x_ref[...].reshape(B * tt, D).astype(BF16), w_ref[...])
    for j in range(D // 128):
        u_scr[j] = u[:, 128 * j:128 * (j + 1)]
    for j in range(D // 128):
        for h in range(L // gpb):
            pieces = [jnp.concatenate([u_scr[j, pl.ds(c * L + h * gpb + q, B, stride=tt), :]
                                       for c in range(cpb)], axis=0) for q in range(gpb)]
            for gg, dest in enumerate(_granule_transpose(pieces)):
                o_ref[gpb * j + gg, :, 128 * h:128 * (h + 1)] = dest.astype(o_ref.dtype)


def _s5_in(x, w, tt):
    B, T, D = x.shape
    G = D // S5_GROUP
    LC = S5_CHUNK * S5_GROUP
    rows = (tt // S5_CHUNK) * B
    return pl.pallas_call(
        _s5_in_kernel,
        grid=(T // tt,),
        in_specs=[pl.BlockSpec((B, tt, D), lambda i: (0, i, 0)), pl.BlockSpec(w.shape, lambda i: (0, 0))],
        out_specs=pl.BlockSpec((G, rows, LC), lambda i: (0, i, 0)),
        out_shape=jax.ShapeDtypeStruct((G, (T // S5_CHUNK) * B, LC), BF16),
        scratch_shapes=[pltpu.VMEM((D // 128, B * tt, 128), F32)],
        compiler_params=pltpu.CompilerParams(dimension_semantics=("arbitrary",),
                                             vmem_limit_bytes=VMEM_LIMIT_BYTES),
    )(x, w)


def _s5_scan_kernel(x_ref, s0_ref, toe_ref, pst_ref, qst_ref, apow_ref, y_ref, sfin_ref,
                    zr_scr, zi_scr, sr_scr, si_scr, *, n_chunks, batch):
    P = apow_ref.shape[-1]
    for gi in range(x_ref.shape[0]):
        xg = x_ref[gi]
        z = _dot(xg, pst_ref[gi])
        zr_scr[...] = z[:, 0:P]
        zi_scr[...] = z[:, P:2 * P]
        ar = apow_ref[gi, 0:1, :]
        ai = apow_ref[gi, 1:2, :]

        def step(n, carry, ar=ar, ai=ai):
            sr, si = carry
            rows = pl.ds(pl.multiple_of(n * batch, batch), batch)
            sr_scr[rows, :] = sr
            si_scr[rows, :] = si
            return (ar * sr - ai * si + zr_scr[rows, :], ar * si + ai * sr + zi_scr[rows, :])

        sr, si = lax.fori_loop(0, n_chunks, step, (s0_ref[gi, 0], s0_ref[gi, 1]), unroll=4)
        sfin_ref[gi, 0] = sr
        sfin_ref[gi, 1] = si
        y = (_dot(xg, toe_ref[gi]) + _dot(sr_scr[...].astype(BF16), qst_ref[gi, 0:P, :])
             + _dot(si_scr[...].astype(BF16), qst_ref[gi, P:2 * P, :]))
        y_ref[gi] = y.astype(y_ref.dtype)


def _s5_scan(xg, s0, toe, pst, qst, apow, n_chunks, batch):
    G, R, LC = xg.shape
    P = apow.shape[-1]
    gb = max(1, min(8, S5_SCAN_ROWS_PER_STEP // R))
    blk = lambda shp: pl.BlockSpec((gb,) + shp, lambda g: (g,) + (0,) * len(shp))
    return pl.pallas_call(
        functools.partial(_s5_scan_kernel, n_chunks=n_chunks, batch=batch),
        grid=(G // gb,),
        in_specs=[blk((R, LC)), blk((2, batch, P)), blk((LC, LC)), blk((LC, 2 * P)), blk((2 * P, LC)),
                  blk((2, P))],
        out_specs=(blk((R, LC)), blk((2, batch, P))),
        out_shape=(jax.ShapeDtypeStruct((G, R, LC), BF16), jax.ShapeDtypeStruct((G, 2, batch, P), F32)),
        scratch_shapes=[pltpu.VMEM((R, P), F32)] * 4,
        compiler_params=pltpu.CompilerParams(dimension_semantics=("arbitrary",),
                                             vmem_limit_bytes=VMEM_LIMIT_BYTES),
    )(xg, s0, toe, pst, qst, apow)


def _s5_out_kernel(x_ref, yg_ref, w_ref, ln_g_ref, ln_b_ref, o_ref, y_scr):
    B, tt, D = x_ref.shape
    L, Cg = S5_CHUNK, S5_GROUP
    gpb = 128 // Cg
    cpb = tt // L
    for j in range(D // 128):
        for h in range(L // gpb):
            srcs = [yg_ref[gpb * j + gg, :, 128 * h:128 * (h + 1)].astype(F32) for gg in range(gpb)]
            for q, dest in enumerate(_granule_transpose(srcs)):
                for c in range(cpb):
                    y_scr[j, pl.ds(c * L + h * gpb + q, B, stride=tt), :] = dest[c * B:(c + 1) * B, :]
    y = jnp.concatenate([y_scr[j] for j in range(D // 128)], axis=1)
    act = _gelu_tanh(y).astype(BF16)
    zz = _dot(act, w_ref[...])
    mix = zz[:, :D] * _sigmoid(zz[:, D:])
    out = _layer_norm(ALPHA * x_ref[...].reshape(B * tt, D) + mix, ln_g_ref[...], ln_b_ref[...])
    o_ref[...] = out.reshape(B, tt, D)


def _s5_out(x, yg, w_out, ln_g, ln_b, tt):
    B, T, D = x.shape
    G, _, LC = yg.shape
    rows = (tt // S5_CHUNK) * B
    full = lambda a: pl.BlockSpec(a.shape, lambda i: (0,) * a.ndim)
    return pl.pallas_call(
        _s5_out_kernel,
        grid=(T // tt,),
        in_specs=[pl.BlockSpec((B, tt, D), lambda i: (0, i, 0)), pl.BlockSpec((G, rows, LC), lambda i: (0, i, 0)),
                  full(w_out), full(ln_g), full(ln_b)],
        out_specs=pl.BlockSpec((B, tt, D), lambda i: (0, i, 0)),
        out_shape=jax.ShapeDtypeStruct((B, T, D), F32),
        scratch_shapes=[pltpu.VMEM((D // 128, B * tt, 128), F32)],
        compiler_params=pltpu.CompilerParams(dimension_semantics=("arbitrary",),
                                             vmem_limit_bytes=VMEM_LIMIT_BYTES),
    )(x, yg, w_out, ln_g, ln_b)


def _s5_mixer(x, s0_re, s0_im, w_in, w_out, s5w, ln_g, ln_b):
    B, T, D = x.shape
    toe, pst, qst, apow = s5w
    xg = _s5_in(x, w_in, S5_STEP_TOKENS)
    s0 = jnp.stack([s0_re, s0_im], axis=0).transpose(2, 0, 1, 3)
    yg, sfin = _s5_scan(xg, s0, toe, pst, qst, apow, T // S5_CHUNK, B)
    out = _s5_out(x, yg, w_out, ln_g, ln_b, S5_STEP_TOKENS)
    sfin = sfin.transpose(1, 2, 0, 3)
    return out, sfin[0], sfin[1]


def _block_diag(w):
    H, d, _ = w.shape
    eye = jnp.eye(H, dtype=w.dtype)
    return (eye[:, None, :, None] * w[:, :, None, :]).reshape(H * d, H * d)


def kernel(x_prompt, x_sample, state_lru_h, state_lru_conv, state_hgrn2, state_s5_re, state_s5_im, ln_mix_g, ln_mix_b, ln_ffn_g, ln_ffn_b, w_in_ab, w_out_ab, lru_conv_w, lru_conv_b, lru_w_r, lru_b_r, lru_w_i, lru_b_i, lru_lam, hgrn_lb_logits, hgrn_norm_g, w_in_s5, w_out_s5, s5_a_re, s5_a_im, s5_b_re, s5_b_im, s5_c_re, s5_c_im, s5_d, s5_log_dt, router_w, router_b, moe_w_gate, moe_w_up, moe_w_down):
    Bp, Tp, D = x_prompt.shape
    Bs, Ts, _ = x_sample.shape
    W = state_lru_h.shape[-1]
    dk = W // HG_HEADS
    row = lambda v: v.reshape(1, -1)

    def moe(y, l, sparse):
        fn = _moe_sparse if sparse else functools.partial(_moe_dense, tm=512)
        out = fn(y.reshape(-1, D), router_w, row(router_b), moe_w_gate, moe_w_up, moe_w_down, l,
                 row(ln_ffn_g[l]), row(ln_ffn_b[l]))
        return out.reshape(y.shape)

    j = 0
    wts = (w_in_ab[j].astype(BF16), w_out_ab[j].astype(BF16), lru_conv_w[j], row(lru_conv_b[j]),
           _block_diag(lru_w_r[j]).astype(BF16), row(lru_b_r[j]),
           _block_diag(lru_w_i[j]).astype(BF16), row(lru_b_i[j]),
           row(lru_lam[j]), hgrn_lb_logits, row(hgrn_norm_g[j]), row(ln_mix_g[0]), row(ln_mix_b[0]))
    yp, p_h, p_conv, p_hg = _ab_mixer(
        x_prompt, jnp.zeros((Bp, 1, W), F32), jnp.zeros((Bp, 3, W), F32),
        jnp.zeros((Bp, HG_HEADS, dk, dk), F32), wts, AB_TILE, AB_SEQS)
    ys, s_h, s_conv, s_hg = _ab_mixer(
        x_sample, state_lru_h[j][:, None, :], state_lru_conv[j], state_hgrn2[j], wts, Ts, 2)
    yp = moe(yp, 0, True)
    ys = moe(ys, 0, False)

    s5w = _s5_weights(s5_a_re[j], s5_a_im[j], s5_log_dt[j], s5_b_re[j], s5_b_im[j],
                      s5_c_re[j], s5_c_im[j], s5_d[j])
    G, P = s5_a_re[j].shape
    w_in5 = w_in_s5[j].astype(BF16)
    w_out5 = w_out_s5[j].astype(BF16)
    zp = jnp.zeros((Bp, G, P), F32)
    yp, p_re, p_im = _s5_mixer(yp, zp, zp, w_in5, w_out5, s5w, row(ln_mix_g[1]), row(ln_mix_b[1]))
    ys, s_re, s_im = _s5_mixer(ys, state_s5_re[j], state_s5_im[j], w_in5, w_out5, s5w,
                               row(ln_mix_g[1]), row(ln_mix_b[1]))
    yp = moe(yp, 1, True)
    ys = moe(ys, 1, False)

    return (yp, ys,
            p_h.reshape(1, Bp, W), p_conv[None], p_hg[None], p_re[None], p_im[None],
            s_h.reshape(1, Bs, W), s_conv[None], s_hg[None], s_re[None], s_im[None])
```

```python
import functools
import math

import jax
import jax.numpy as jnp
from jax import lax
from jax.experimental import pallas as pl
from jax.experimental.pallas import tpu as pltpu

F32 = jnp.float32
BF16 = jnp.bfloat16
HIGHEST = lax.Precision.HIGHEST

DEPTH = 2
RG_C = 8.0
LRU_HEADS = 8
HG_HEADS = 4
AB_TILE = 512
AB_SEQS = 1
HG_CHUNK = 16
RMS_EPS = 1e-6
S5_GROUP = 16
S5_SCAN_ROWS_PER_STEP = 2048
S5_STEP_TOKENS = 32
S5_CHUNK = 16
N_EXPERT_GROUPS = 4
ALPHA = (2.0 * DEPTH) ** 0.25
LN_EPS = 1e-5
MOE_TILE = 256
MOE_CHUNK = 16
MOE_SLOTS = 768
MOE_BLOCK = 1024
ROUTE_TILES_PER_STEP = 4
VMEM_LIMIT_BYTES = 56 * 1024 * 1024


def _dot(a, b):
    return jnp.dot(a, b, preferred_element_type=F32)


def _dot_nt(a, b):
    return lax.dot_general(a, b, (((1,), (1,)), ((), ())), preferred_element_type=F32)


def _sigmoid(x):
    return 0.5 * jnp.tanh(0.5 * x) + 0.5


def _gelu_tanh(x):
    c = math.sqrt(2.0 / math.pi)
    return 0.5 * x * (1.0 + jnp.tanh(c * (x + 0.044715 * (x * x * x))))


def _layer_norm(s, g, b):
    mu = jnp.mean(s, axis=-1, keepdims=True)
    d = s - mu
    var = jnp.mean(d * d, axis=-1, keepdims=True)
    return d * lax.rsqrt(var + LN_EPS) * g + b


def _shift_rows(x, s, row):
    del row
    return pltpu.roll(x, s, axis=0)


def _ab_kernel(x_ref, h0_ref, conv0_ref, s0_ref, w_in_ref, w_out_ref, conv_w_ref, conv_b_ref,
               wr_ref, br_ref, wi_ref, bi_ref, lam_ref, lb_logits_ref, norm_g_ref, ln_g_ref, ln_b_ref,
               y_ref, h_out_ref, conv_out_ref, s_out_ref,
               zx_buf, h_carry, st_carry, st_all, *, tb, width, nb):
    for si in range(nb):
        _ab_sequence_tile(si, x_ref, h0_ref, conv0_ref, s0_ref, w_in_ref, w_out_ref, conv_w_ref, conv_b_ref,
                          wr_ref, br_ref, wi_ref, bi_ref, lam_ref, lb_logits_ref, norm_g_ref, ln_g_ref, ln_b_ref,
                          y_ref, h_out_ref, conv_out_ref, s_out_ref,
                          zx_buf.at[si], h_carry.at[si], st_carry.at[si], st_all.at[si], tb, width)


def _ab_sequence_tile(si, x_ref, h0_ref, conv0_ref, s0_ref, w_in_ref, w_out_ref, conv_w_ref, conv_b_ref,
                      wr_ref, br_ref, wi_ref, bi_ref, lam_ref, lb_logits_ref, norm_g_ref, ln_g_ref, ln_b_ref,
                      y_ref, h_out_ref, conv_out_ref, s_out_ref, zx_buf, h_carry, st_carry, st_all, tb, width):
    t = pl.program_id(1)
    nt = pl.num_programs(1)
    W = width
    L = HG_CHUNK
    nc = tb // L
    dk = W // HG_HEADS

    @pl.when(t == 0)
    def _init():
        zx_buf[0:8, :] = jnp.zeros((8, W), F32)
        zx_buf[5:8, :] = conv0_ref[si]
        h_carry[...] = jnp.broadcast_to(h0_ref[si], h_carry.shape)
        for hd in range(HG_HEADS):
            st_carry[hd] = s0_ref[si, hd].T

    x = x_ref[si]
    z = _dot(x.astype(BF16), w_in_ref[...])
    z_gate = z[:, 0 * W:1 * W]
    z_x = z[:, 1 * W:2 * W]
    z_q = z[:, 2 * W:3 * W]
    z_f = z[:, 3 * W:4 * W]
    z_i = z[:, 4 * W:5 * W]
    z_g = z[:, 5 * W:6 * W]

    zx_buf[8:8 + tb, :] = z_x
    cw = conv_w_ref[...]
    u = (conv_b_ref[...] + cw[3:4, :] * z_x
         + cw[2:3, :] * zx_buf[7:7 + tb, :]
         + cw[1:2, :] * zx_buf[6:6 + tb, :]
         + cw[0:1, :] * zx_buf[5:5 + tb, :])
    tail = zx_buf[tb:tb + 8, :]
    zx_buf[0:8, :] = tail

    ub = u.astype(BF16)
    r = _sigmoid(_dot(ub, wr_ref[...]) + br_ref[...])
    ig = _sigmoid(_dot(ub, wi_ref[...]) + bi_ref[...])
    nlam = -lam_ref[...]
    softplus_nlam = jnp.maximum(nlam, 0.0) + jnp.log1p(jnp.exp(-jnp.abs(nlam)))
    log_a = (-RG_C) * r * softplus_nlam
    a = jnp.exp(log_a)
    th = jnp.tanh(log_a)
    b = jnp.sqrt(-2.0 * th / (1.0 - th)) * ig * u

    row = lax.broadcasted_iota(jnp.int32, (tb, W), 0)
    rin8 = row % 8
    s = 1
    while s < 8:
        m = rin8 >= s
        a_sh = _shift_rows(a, s, row)
        b_sh = _shift_rows(b, s, row)
        b = jnp.where(m, a * b_sh + b, b)
        a = jnp.where(m, a * a_sh, a)
        s *= 2
    a3 = a.reshape(tb // 8, 8, W)
    b3 = b.reshape(tb // 8, 8, W)
    h_prev = h_carry[0:1, :]
    h_groups = []
    for g in range(tb // 8):
        hg = b3[g] + a3[g] * h_prev
        h_groups.append(hg)
        h_prev = hg[7:8, :]
    h = jnp.concatenate(h_groups, axis=0)
    h_carry[...] = jnp.broadcast_to(h_prev, h_carry.shape)
    out_a = h * _gelu_tanh(z_gate)

    lg = lb_logits_ref[...]
    e = jnp.exp(lg - jnp.max(lg, axis=0, keepdims=True))
    lb = e[0:1, :] / jnp.sum(e, axis=0, keepdims=True)
    f = lb + (1.0 - lb) * _sigmoid(z_f)
    log_f = jnp.log(f)
    kk = 1.0 - f
    rin = row % L
    G = log_f
    s = 1
    while s < L:
        G = G + jnp.where(rin >= s, _shift_rows(G, s, row), 0.0)
        s *= 2
    G3 = G.reshape(nc, L, W)
    G_last = G3[:, L - 1:L, :]
    q_dec = (z_q * jnp.exp(G)).reshape(nc, L, W).astype(BF16)
    k_inc = (kk * jnp.exp(-G)).reshape(nc, L, W).astype(BF16)
    k_tail = (kk.reshape(nc, L, W) * jnp.exp(G_last - G3)).astype(BF16)
    decay = jnp.exp(G_last)
    v3 = z_i.reshape(nc, L, W).astype(BF16)
    causal = (lax.broadcasted_iota(jnp.int32, (nc, L, L), 1)
              >= lax.broadcasted_iota(jnp.int32, (nc, L, L), 2))

    o_heads = []
    for hd in range(HG_HEADS):
        sl = slice(hd * dk, (hd + 1) * dk)
        qd, ki, kt, vv = q_dec[:, :, sl], k_inc[:, :, sl], k_tail[:, :, sl], v3[:, :, sl]
        scores = jnp.einsum('ntk,nsk->nts', qd, ki, preferred_element_type=F32)
        scores = jnp.where(causal, scores, 0.0).astype(BF16)
        o_intra = jnp.einsum('nts,nsv->ntv', scores, vv, preferred_element_type=F32)
        vt = jnp.swapaxes(vv.astype(F32), 1, 2).astype(BF16)
        upd = jnp.einsum('nvs,nsk->nvk', vt, kt, preferred_element_type=F32)
        st = st_carry[hd]
        for n in range(nc):
            st_all[hd, n] = st.astype(BF16)
            st = st * decay[n, :, sl] + upd[n]
        st_carry[hd] = st
        o_inter = jnp.einsum('ntk,nvk->ntv', qd, st_all[hd], preferred_element_type=F32)
        o = (o_intra + o_inter).reshape(tb, dk)
        o = o * lax.rsqrt(jnp.mean(o * o, axis=-1, keepdims=True) + RMS_EPS)
        o_heads.append(o)
    o_all = jnp.concatenate(o_heads, axis=-1) * norm_g_ref[...]
    out_b = o_all * (z_g * _sigmoid(z_g))

    mixed = jnp.concatenate([out_a, out_b], axis=-1).astype(BF16)
    y = _dot(mixed, w_out_ref[...])
    y_ref[si] = _layer_norm(ALPHA * x + y, ln_g_ref[...], ln_b_ref[...])

    @pl.when(t == nt - 1)
    def _fin():
        h_out_ref[si] = h[tb - 1:tb, :]
        conv_out_ref[si] = tail[5:8, :]
        for hd in range(HG_HEADS):
            s_out_ref[si, hd] = st_carry[hd].T


def _ab_mixer(x, h0, conv0, s0, wts, tb, nb):
    B, T, D = x.shape
    W = h0.shape[-1]
    dk = W // HG_HEADS
    nt = T // tb
    nc = tb // HG_CHUNK
    (w_in, w_out, conv_w, conv_b, wr, br, wi, bi, lam, lb_logits, norm_g, ln_g, ln_b) = wts
    full = lambda a: pl.BlockSpec(a.shape, lambda b, t: (0,) * a.ndim)
    per_b = lambda shp: pl.BlockSpec((nb,) + shp, lambda b, t: (b,) + (0,) * len(shp))
    out_shape = (jax.ShapeDtypeStruct((B, T, D), F32),
                 jax.ShapeDtypeStruct((B, 1, W), F32),
                 jax.ShapeDtypeStruct((B, 3, W), F32),
                 jax.ShapeDtypeStruct((B, HG_HEADS, dk, dk), F32))
    return pl.pallas_call(
        functools.partial(_ab_kernel, tb=tb, width=W, nb=nb),
        grid=(B // nb, nt),
        in_specs=[pl.BlockSpec((nb, tb, D), lambda b, t: (b, t, 0)),
                  per_b((1, W)), per_b((3, W)), per_b((HG_HEADS, dk, dk))]
                 + [full(a) for a in wts],
        out_specs=(pl.BlockSpec((nb, tb, D), lambda b, t: (b, t, 0)),
                   per_b((1, W)), per_b((3, W)), per_b((HG_HEADS, dk, dk))),
        out_shape=out_shape,
        scratch_shapes=[pltpu.VMEM((nb, tb + 8, W), F32),
                        pltpu.VMEM((nb, 8, W), F32),
                        pltpu.VMEM((nb, HG_HEADS, dk, dk), F32),
                        pltpu.VMEM((nb, HG_HEADS, nc, dk, dk), BF16)],
        compiler_params=pltpu.CompilerParams(
            dimension_semantics=("arbitrary", "arbitrary"),
            vmem_limit_bytes=VMEM_LIMIT_BYTES),
    )(x, h0, conv0, s0, *wts)


def _router(x, rw, rb, n_experts):
    per_group = n_experts // N_EXPERT_GROUPS
    logits = jnp.dot(x, rw, preferred_element_type=F32, precision=HIGHEST) + rb
    logits = logits - jnp.max(logits, axis=-1, keepdims=True)
    ex = jnp.exp(logits)
    p = ex / jnp.sum(ex, axis=-1, keepdims=True)
    lane_i = lax.broadcasted_iota(jnp.int32, p.shape, 1)
    grp = lane_i // per_group
    lane = lane_i.astype(F32)
    big = float(n_experts)

    def top2(pm):
        m1 = jnp.max(pm, axis=-1, keepdims=True)
        i1 = jnp.min(jnp.where(pm == m1, lane, big), axis=-1, keepdims=True)
        pm2 = jnp.where(lane == i1, -1.0, pm)
        m2 = jnp.max(pm2, axis=-1, keepdims=True)
        i2 = jnp.min(jnp.where(pm2 == m2, lane, big), axis=-1, keepdims=True)
        return m1, i1, m2, i2

    best = None
    g_sel = None
    for j in range(N_EXPERT_GROUPS):
        m1, _, m2, _ = top2(jnp.where(grp == j, p, -1.0))
        score = m1 + m2
        if best is None:
            best, g_sel = score, jnp.zeros_like(lane_i[:, :1])
        else:
            better = score > best
            best = jnp.where(better, score, best)
            g_sel = jnp.where(better, j, g_sel)
    m1, i1, m2, i2 = top2(jnp.where(grp == g_sel, p, -1.0))
    den = m1 + m2
    return jnp.where(lane == i1, m1 / den, 0.0) + jnp.where(lane == i2, m2 / den, 0.0)


def _moe_dense_kernel(x_ref, rw_ref, rb_ref, wg_ref, wu_ref, wd_ref, ln_g_ref, ln_b_ref,
                      y_ref, xb_scr, comb_scr, acc_scr, *, n_experts):
    e = pl.program_id(1)

    @pl.when(e == 0)
    def _init():
        x = x_ref[...]
        xb_scr[...] = x.astype(BF16)
        comb_scr[...] = _router(x, rw_ref[...], rb_ref[...], n_experts)
        acc_scr[...] = jnp.zeros_like(acc_scr)

    comb = comb_scr[...]
    lane = lax.broadcasted_iota(jnp.int32, comb.shape, 1)
    c_e = jnp.sum(jnp.where(lane == e, comb, 0.0), axis=-1, keepdims=True)
    xb = xb_scr[...]
    g = _dot(xb, wg_ref[...].astype(BF16))
    up = _dot(xb, wu_ref[...].astype(BF16))
    hmid = (g * _sigmoid(g) * up).astype(BF16)
    acc_scr[...] += c_e * _dot(hmid, wd_ref[...].astype(BF16))

    @pl.when(e == n_experts - 1)
    def _fin():
        y_ref[...] = _layer_norm(ALPHA * x_ref[...] + acc_scr[...], ln_g_ref[...], ln_b_ref[...])


def _moe_dense(x2d, rw, rb, wg, wu, wd, layer, ln_g, ln_b, tm):
    N, D = x2d.shape
    _, E, _, De = wg.shape
    full = lambda a: pl.BlockSpec(a.shape, lambda i, e: (0,) * a.ndim)
    return pl.pallas_call(
        functools.partial(_moe_dense_kernel, n_experts=E),
        grid=(N // tm, E),
        in_specs=[pl.BlockSpec((tm, D), lambda i, e: (i, 0)), full(rw), full(rb),
                  pl.BlockSpec((None, None, D, De), lambda i, e: (layer, e, 0, 0)),
                  pl.BlockSpec((None, None, D, De), lambda i, e: (layer, e, 0, 0)),
                  pl.BlockSpec((None, None, De, D), lambda i, e: (layer, e, 0, 0)),
                  full(ln_g), full(ln_b)],
        out_specs=pl.BlockSpec((tm, D), lambda i, e: (i, 0)),
        out_shape=jax.ShapeDtypeStruct((N, D), F32),
        scratch_shapes=[pltpu.VMEM((tm, D), BF16), pltpu.VMEM((tm, E), F32), pltpu.VMEM((tm, D), F32)],
        compiler_params=pltpu.CompilerParams(
            dimension_semantics=("arbitrary", "arbitrary"),
            vmem_limit_bytes=VMEM_LIMIT_BYTES),
    )(x2d, rw, rb, wg, wu, wd, ln_g, ln_b)


def _route_kernel(x_ref, rwt_ref, rbc_ref, meta_ref, pc_ref, *, n_experts):
    E = n_experts
    per_group = E // N_EXPERT_GROUPS
    x = x_ref[...]
    tm = x.shape[0]
    logits = lax.dot_general(rwt_ref[...], x, (((1,), (1,)), ((), ())),
                             precision=HIGHEST, preferred_element_type=F32) + rbc_ref[...]
    logits = logits - jnp.max(logits, axis=0, keepdims=True)
    ex = jnp.exp(logits)
    p = ex / jnp.sum(ex, axis=0, keepdims=True)
    eid_i = lax.broadcasted_iota(jnp.int32, (E, tm), 0)
    grp = eid_i // per_group
    eid = eid_i.astype(F32)
    big = float(E)

    def top2(pm):
        m1 = jnp.max(pm, axis=0, keepdims=True)
        i1 = jnp.min(jnp.where(pm == m1, eid, big), axis=0, keepdims=True)
        pm2 = jnp.where(eid == i1, -1.0, pm)
        m2 = jnp.max(pm2, axis=0, keepdims=True)
        i2 = jnp.min(jnp.where(pm2 == m2, eid, big), axis=0, keepdims=True)
        return m1, i1, m2, i2

    best = None
    g_sel = None
    for j in range(N_EXPERT_GROUPS):
        m1, _, m2, _ = top2(jnp.where(grp == j, p, -1.0))
        score = m1 + m2
        if best is None:
            best, g_sel = score, jnp.zeros_like(eid_i[:1, :])
        else:
            better = score > best
            best = jnp.where(better, score, best)
            g_sel = jnp.where(better, j, g_sel)
    m1, i1, m2, i2 = top2(jnp.where(grp == g_sel, p, -1.0))
    den = m1 + m2

    oh1_all = jnp.where(eid == i1, 1.0, 0.0)
    oh2_all = jnp.where(eid == i2, 1.0, 0.0)
    g1_all = m1 / den
    g2_all = m2 / den
    tl = MOE_TILE
    earlier = jnp.where(lax.broadcasted_iota(jnp.int32, (tl, tl), 0)
                        < lax.broadcasted_iota(jnp.int32, (tl, tl), 1), 1.0, 0.0).astype(BF16)
    below = jnp.where(lax.broadcasted_iota(jnp.int32, (E, E), 1)
                      < lax.broadcasted_iota(jnp.int32, (E, E), 0), 1.0, 0.0)
    for k in range(tm // tl):
        lanes = slice(k * tl, (k + 1) * tl)
        oh1, oh2 = oh1_all[:, lanes], oh2_all[:, lanes]
        r1 = _dot(oh1.astype(BF16), earlier)
        r2 = _dot(oh2.astype(BF16), earlier)
        cnt1 = jnp.sum(oh1, axis=1, keepdims=True)
        cnt2 = jnp.sum(oh2, axis=1, keepdims=True)
        pc = jnp.floor((cnt1 + cnt2 + (MOE_CHUNK - 1.0)) * (1.0 / MOE_CHUNK)) * MOE_CHUNK
        pc_b = jnp.broadcast_to(pc, (E, 128))
        run_off = jnp.dot(below, pc_b, precision=HIGHEST, preferred_element_type=F32)[:, 0:1]
        meta_ref[k, 0:1, :] = jnp.sum(oh1 * (run_off + r1), axis=0, keepdims=True)
        meta_ref[k, 1:2, :] = jnp.sum(oh2 * (run_off + cnt1 + r2), axis=0, keepdims=True)
        meta_ref[k, 2:3, :] = g1_all[:, lanes]
        meta_ref[k, 3:4, :] = g2_all[:, lanes]
        meta_ref[k, 4:8, :] = jnp.zeros((4, tl), F32)
        pc_ref[k] = pc_b


def _route(x2d, rwt, rbc):
    N, D = x2d.shape
    E = rwt.shape[0]
    nt = N // MOE_TILE
    tps = ROUTE_TILES_PER_STEP if nt % ROUTE_TILES_PER_STEP == 0 else 1
    full = lambda a: pl.BlockSpec(a.shape, lambda i: (0,) * a.ndim)
    return pl.pallas_call(
        functools.partial(_route_kernel, n_experts=E),
        grid=(nt // tps,),
        in_specs=[pl.BlockSpec((tps * MOE_TILE, D), lambda i: (i, 0)), full(rwt), full(rbc)],
        out_specs=(pl.BlockSpec((tps, 8, MOE_TILE), lambda i: (i, 0, 0)),
                   pl.BlockSpec((tps, E, 128), lambda i: (i, 0, 0))),
        out_shape=(jax.ShapeDtypeStruct((nt, 8, MOE_TILE), F32), jax.ShapeDtypeStruct((nt, E, 128), F32)),
        compiler_params=pltpu.CompilerParams(dimension_semantics=("arbitrary",),
                                             vmem_limit_bytes=VMEM_LIMIT_BYTES),
    )(x2d, rwt, rbc)


def _moe_schedule(pc, n_blocks_max):
    nt, E = pc.shape
    cum = jnp.cumsum(pc, axis=1)
    run_off = cum - pc
    total = cum[:, -1]
    seg_len = jnp.sum(pc, axis=0)
    seg_pad = ((seg_len + MOE_BLOCK - 1) // MOE_BLOCK) * MOE_BLOCK
    seg_end = jnp.cumsum(seg_pad)
    goff = (seg_end - seg_pad)[None, :] + (jnp.cumsum(pc, axis=0) - pc)
    r = jnp.arange(MOE_SLOTS // MOE_CHUNK, dtype=jnp.int32) * MOE_CHUNK
    e_c = jnp.minimum(jnp.sum((cum[:, None, :] <= r[None, :, None]).astype(jnp.int32), axis=-1), E - 1)
    pick = e_c[:, :, None] == jnp.arange(E, dtype=jnp.int32)[None, None, :]
    dst_row = r[None, :] + jnp.sum(jnp.where(pick, (goff - run_off)[:, None, :], 0), axis=-1)
    dst_chunk = jnp.where(r[None, :] < total[:, None], dst_row // MOE_CHUNK, -1).astype(jnp.int32)
    blk_start = jnp.arange(n_blocks_max, dtype=jnp.int32) * MOE_BLOCK
    block_expert = jnp.minimum(jnp.sum((seg_end[None, :] <= blk_start[:, None]).astype(jnp.int32), axis=-1), E - 1)
    n_valid = (seg_end[-1] // MOE_BLOCK).reshape(1).astype(jnp.int32)
    n_chunks_all = n_blocks_max * (MOE_BLOCK // MOE_CHUNK)
    n_fill_max = n_chunks_all - (nt * 2 * MOE_TILE) // MOE_CHUNK
    gap_start = jnp.concatenate([seg_end - seg_pad + seg_len, seg_end[-1:]]) // MOE_CHUNK
    gap_len = jnp.concatenate([seg_pad - seg_len, n_chunks_all * MOE_CHUNK - seg_end[-1:]]) // MOE_CHUNK
    gap_cum = jnp.cumsum(gap_len)
    k = jnp.arange(n_fill_max, dtype=jnp.int32)
    which = jnp.minimum(jnp.sum((gap_cum[None, :] <= k[:, None]).astype(jnp.int32), axis=-1), E)
    pick_gap = which[:, None] == jnp.arange(E + 1, dtype=jnp.int32)[None, :]
    fill = k + jnp.sum(jnp.where(pick_gap, (gap_start - (gap_cum - gap_len))[None, :], 0), axis=-1)
    fill = jnp.where(k < gap_cum[-1], fill, 0).astype(jnp.int32)
    n_fill = gap_cum[-1].reshape(1).astype(jnp.int32)
    return (dst_chunk.reshape(-1), total.astype(jnp.int32), block_expert.astype(jnp.int32), n_valid, fill, n_fill)


def _chunk_copy(hbm_ref, buf, sem, b, c, d, to_hbm):
    local = buf.at[b, pl.ds(pl.multiple_of(c * MOE_CHUNK, MOE_CHUNK), MOE_CHUNK)]
    remote = hbm_ref.at[pl.ds(pl.multiple_of(d * MOE_CHUNK, MOE_CHUNK), MOE_CHUNK)]
    if to_hbm:
        return pltpu.make_async_copy(local, remote, sem.at[b])
    return pltpu.make_async_copy(remote, local, sem.at[b])


def _for_tile_chunks(dst_ref, tot_ref, tile, fn):
    slots = MOE_SLOTS // MOE_CHUNK

    def body(c, carry):
        fn(c, dst_ref[tile * slots + c])
        return carry

    lax.fori_loop(0, tot_ref[tile] // MOE_CHUNK, body, 0)


def _dispatch_kernel(dst_ref, tot_ref, fill_ref, nfill_ref, x_ref, meta_ref, xs_hbm, buf, sem, zbuf, zsem):
    i = pl.program_id(0)
    n = pl.num_programs(0)
    b = i % 2
    tm = x_ref.shape[0]

    def wait_tile(tile, slot):
        _for_tile_chunks(dst_ref, tot_ref, tile,
                         lambda c, d: _chunk_copy(xs_hbm, buf, sem, slot, c, d, True).wait())

    def zero_copy(k):
        d = fill_ref[k]
        return pltpu.make_async_copy(
            zbuf, xs_hbm.at[pl.ds(pl.multiple_of(d * MOE_CHUNK, MOE_CHUNK), MOE_CHUNK)], zsem.at[0])

    @pl.when(i == 0)
    def _():
        zbuf[...] = jnp.zeros_like(zbuf)
        lax.fori_loop(0, nfill_ref[0], lambda k, c: (zero_copy(k).start(), c)[1], 0)

    @pl.when(i >= 2)
    def _():
        wait_tile(i - 2, b)

    s1 = meta_ref[0:1, :].astype(jnp.int32)
    s2 = meta_ref[1:2, :].astype(jnp.int32)
    rows = lax.broadcasted_iota(jnp.int32, (MOE_SLOTS, tm), 0)
    dsp = jnp.where((rows == s1) | (rows == s2), 1.0, 0.0).astype(BF16)
    buf[b] = _dot(dsp, x_ref[...].astype(BF16)).astype(BF16)
    _for_tile_chunks(dst_ref, tot_ref, i, lambda c, d: _chunk_copy(xs_hbm, buf, sem, b, c, d, True).start())

    @pl.when(i == n - 1)
    def _():
        wait_tile(i, b)

    @pl.when((i == n - 1) & (i >= 1))
    def _():
        wait_tile(i - 1, 1 - b)

    @pl.when(i == n - 1)
    def _():
        lax.fori_loop(0, nfill_ref[0], lambda k, c: (zero_copy(k).wait(), c)[1], 0)


def _dispatch(x2d, meta, dst_chunk, total, fill, n_fill, n_rows):
    N, D = x2d.shape
    nt = N // MOE_TILE
    return pl.pallas_call(
        _dispatch_kernel,
        grid_spec=pltpu.PrefetchScalarGridSpec(
            num_scalar_prefetch=4, grid=(nt,),
            in_specs=[pl.BlockSpec((MOE_TILE, D), lambda i, *_: (i, 0)),
                      pl.BlockSpec((None, 8, MOE_TILE), lambda i, *_: (i, 0, 0))],
            out_specs=pl.BlockSpec(memory_space=pl.ANY),
            scratch_shapes=[pltpu.VMEM((2, MOE_SLOTS, D), BF16), pltpu.SemaphoreType.DMA((2,)),
                            pltpu.VMEM((MOE_CHUNK, D), BF16), pltpu.SemaphoreType.DMA((1,))]),
        out_shape=jax.ShapeDtypeStruct((n_rows, D), BF16),
        compiler_params=pltpu.CompilerParams(dimension_semantics=("arbitrary",),
                                             vmem_limit_bytes=VMEM_LIMIT_BYTES),
    )(dst_chunk, total, fill, n_fill, x2d, meta)


def _ffn_kernel(be_ref, nv_ref, xs_ref, wg_ref, wu_ref, wd_ref, ys_ref, wg_b, wu_b, wd_b):
    j = pl.program_id(0)
    used = j < nv_ref[0]
    new_expert = (j == 0) | (be_ref[j] != be_ref[jnp.maximum(j - 1, 0)])

    @pl.when(used & new_expert)
    def _():
        wg_b[...] = wg_ref[...].astype(BF16)
        wu_b[...] = wu_ref[...].astype(BF16)
        wd_b[...] = wd_ref[...].astype(BF16)

    @pl.when(used)
    def _():
        xb = xs_ref[...]
        g = _dot(xb, wg_b[...])
        up = _dot(xb, wu_b[...])
        hmid = (g * _sigmoid(g) * up).astype(BF16)
        ys_ref[...] = _dot(hmid, wd_b[...]).astype(ys_ref.dtype)

    @pl.when(jnp.logical_not(used))
    def _():
        ys_ref[...] = jnp.zeros_like(ys_ref)


def _ffn(xs, wg, wu, wd, layer, block_expert, n_valid):
    R, D = xs.shape
    _, E, _, De = wg.shape
    last = lambda j, nv: jnp.maximum(jnp.minimum(j, nv[0] - 1), 0)
    return pl.pallas_call(
        _ffn_kernel,
        grid_spec=pltpu.PrefetchScalarGridSpec(
            num_scalar_prefetch=2, grid=(R // MOE_BLOCK,),
            in_specs=[pl.BlockSpec((MOE_BLOCK, D), lambda j, be, nv: (last(j, nv), 0)),
                      pl.BlockSpec((None, None, D, De), lambda j, be, nv: (layer, be[last(j, nv)], 0, 0)),
                      pl.BlockSpec((None, None, D, De), lambda j, be, nv: (layer, be[last(j, nv)], 0, 0)),
                      pl.BlockSpec((None, None, De, D), lambda j, be, nv: (layer, be[last(j, nv)], 0, 0))],
            out_specs=pl.BlockSpec((MOE_BLOCK, D), lambda j, be, nv: (j, 0)),
            scratch_shapes=[pltpu.VMEM((D, De), BF16), pltpu.VMEM((D, De), BF16), pltpu.VMEM((De, D), BF16)]),
        out_shape=jax.ShapeDtypeStruct((R, D), BF16),
        compiler_params=pltpu.CompilerParams(dimension_semantics=("arbitrary",),
                                             vmem_limit_bytes=VMEM_LIMIT_BYTES),
    )(block_expert, n_valid, xs, wg, wu, wd)


def _combine_kernel(dst_ref, tot_ref, x_ref, metac_ref, ys_hbm, ln_g_ref, ln_b_ref, o_ref, buf, sem):
    i = pl.program_id(0)
    n = pl.num_programs(0)
    b = i % 2
    tm = x_ref.shape[0]

    def fetch(tile, slot):
        _for_tile_chunks(dst_ref, tot_ref, tile,
                         lambda c, d: _chunk_copy(ys_hbm, buf, sem, slot, c, d, False).start())

    @pl.when(i == 0)
    def _():
        fetch(0, 0)

    @pl.when(i + 1 < n)
    def _():
        fetch(i + 1, 1 - b)

    _for_tile_chunks(dst_ref, tot_ref, i, lambda c, d: _chunk_copy(ys_hbm, buf, sem, b, c, d, False).wait())
    live = lax.broadcasted_iota(jnp.int32, (MOE_SLOTS, 1), 0) < tot_ref[i]
    ys = jnp.where(live, buf[b], jnp.zeros((), BF16))
    mc = metac_ref[...]
    s1 = mc[:, 0:1].astype(jnp.int32)
    s2 = mc[:, 1:2].astype(jnp.int32)
    lane = lax.broadcasted_iota(jnp.int32, (tm, MOE_SLOTS), 1)
    cmb = (jnp.where(lane == s1, mc[:, 2:3], 0.0) + jnp.where(lane == s2, mc[:, 3:4], 0.0)).astype(BF16)
    o_ref[...] = _layer_norm(ALPHA * x_ref[...] + _dot(cmb, ys), ln_g_ref[...], ln_b_ref[...])


def _combine(x2d, meta_col, ys, dst_chunk, total, ln_g, ln_b):
    N, D = x2d.shape
    nt = N // MOE_TILE
    return pl.pallas_call(
        _combine_kernel,
        grid_spec=pltpu.PrefetchScalarGridSpec(
            num_scalar_prefetch=2, grid=(nt,),
            in_specs=[pl.BlockSpec((MOE_TILE, D), lambda i, dst, tot: (i, 0)),
                      pl.BlockSpec((MOE_TILE, 8), lambda i, dst, tot: (i, 0)),
                      pl.BlockSpec(memory_space=pl.ANY),
                      pl.BlockSpec(ln_g.shape, lambda i, dst, tot: (0, 0)),
                      pl.BlockSpec(ln_b.shape, lambda i, dst, tot: (0, 0))],
            out_specs=pl.BlockSpec((MOE_TILE, D), lambda i, dst, tot: (i, 0)),
            scratch_shapes=[pltpu.VMEM((2, MOE_SLOTS, D), BF16), pltpu.SemaphoreType.DMA((2,))]),
        out_shape=jax.ShapeDtypeStruct((N, D), F32),
        compiler_params=pltpu.CompilerParams(dimension_semantics=("arbitrary",),
                                             vmem_limit_bytes=VMEM_LIMIT_BYTES),
    )(dst_chunk, total, x2d, meta_col, ys, ln_g, ln_b)


def _moe_sparse(x2d, rw, rb, wg, wu, wd, layer, ln_g, ln_b):
    N, D = x2d.shape
    E = rw.shape[1]
    nt = N // MOE_TILE
    meta, pc = _route(x2d, rw.T, rb.reshape(E, 1))
    pc = pc[:, :, 0].astype(jnp.int32)
    n_rows = nt * MOE_SLOTS + E * MOE_BLOCK
    assert n_rows % MOE_BLOCK == 0
    dst_chunk, total, block_expert, n_valid, fill, n_fill = _moe_schedule(pc, n_rows // MOE_BLOCK)
    xs = _dispatch(x2d, meta, dst_chunk, total, fill, n_fill, n_rows)
    ys = _ffn(xs, wg, wu, wd, layer, block_expert, n_valid)
    meta_col = jnp.swapaxes(meta, 1, 2).reshape(N, 8)
    return _combine(x2d, meta_col, ys, dst_chunk, total, ln_g, ln_b)


def _s5_weights_kernel(lam_col_ref, lam_row_ref, bt_ref, ct_ref, d_ref,
                       toe_ref, pst_ref, qst_ref, a_pow_ref):
    L, Cg = S5_CHUNK, S5_GROUP
    LC = L * Cg
    P = lam_row_ref.shape[-1]

    def discretise(a_re, a_im, log_dt):
        dt = jnp.exp(log_dt)
        lr, li = dt * a_re, dt * a_im
        er = jnp.exp(lr)
        abr, abi = er * jnp.cos(li), er * jnp.sin(li)
        nr, ni = abr - 1.0, abi
        den = a_re * a_re + a_im * a_im
        cr = (nr * a_re + ni * a_im) / den
        ci = (ni * a_re - nr * a_im) / den
        return lr, li, cr, ci

    def a_power(lr, li, tau):
        er = jnp.exp(tau * lr)
        return er * jnp.cos(tau * li), er * jnp.sin(tau * li)

    lc = lam_col_ref[...]
    lr_c, li_c, _, _ = discretise(lc[:, 0:1], lc[:, 1:2], lc[:, 2:3])
    lw = lam_row_ref[...]
    lr_r, li_r, cr_r, ci_r = discretise(lw[0:1, :], lw[1:2, :], lw[2:3, :])

    btr, bti = bt_ref[0], bt_ref[1]
    bbr = cr_r * btr - ci_r * bti
    bbi = cr_r * bti + ci_r * btr

    ctr, cti = ct_ref[0], ct_ref[1]
    tau_l = (lax.broadcasted_iota(jnp.int32, (P, LC), 1) // Cg).astype(F32)

    pr, pi = a_power(lr_c, li_c, tau_l)
    car = ctr * pr - cti * pi
    cai = ctr * pi + cti * pr
    t0 = (jnp.dot(bbr[0:Cg, :], car, preferred_element_type=F32, precision=HIGHEST)
          - jnp.dot(bbi[0:Cg, :], cai, preferred_element_type=F32, precision=HIGHEST))
    r16 = lax.broadcasted_iota(jnp.int32, (Cg, LC), 0)
    l16 = lax.broadcasted_iota(jnp.int32, (Cg, LC), 1)
    t0 = t0 + jnp.where(r16 == l16, d_ref[...], 0.0)
    toe_ref[0:Cg, :] = t0.astype(toe_ref.dtype)
    for s in range(1, L):
        toe_ref[s * Cg:(s + 1) * Cg, :] = jnp.where(
            l16 >= s * Cg, pltpu.roll(t0, s * Cg, axis=1), 0.0).astype(toe_ref.dtype)

    pr, pi = a_power(lr_c, li_c, tau_l + 1.0)
    qst_ref[0:P, :] = (ctr * pr - cti * pi).astype(qst_ref.dtype)
    qst_ref[P:2 * P, :] = (-(ctr * pi + cti * pr)).astype(qst_ref.dtype)

    tau_r = (L - 1 - lax.broadcasted_iota(jnp.int32, (LC, P), 0) // Cg).astype(F32)
    pr, pi = a_power(lr_r, li_r, tau_r)
    pst_ref[:, 0:P] = (bbr * pr - bbi * pi).astype(pst_ref.dtype)
    pst_ref[:, P:2 * P] = (bbr * pi + bbi * pr).astype(pst_ref.dtype)

    pr, pi = a_power(lr_r, li_r, float(L))
    a_pow_ref[0:1, :] = pr
    a_pow_ref[1:2, :] = pi


def _s5_weights(a_re, a_im, log_dt, b_re, b_im, c_re, c_im, d):
    G, P = a_re.shape
    Cg, L = S5_GROUP, S5_CHUNK
    LC = L * Cg
    lam = jnp.stack([a_re, a_im, jnp.broadcast_to(log_dt[:, None], (G, P)), jnp.zeros((G, P), F32)], axis=1)
    lam_row = lam
    lam_col = jnp.swapaxes(lam, 1, 2)
    bt = jnp.stack([jnp.swapaxes(b_re, 1, 2), jnp.swapaxes(b_im, 1, 2)], axis=1)
    bt = jnp.tile(bt, (1, 1, L, 1))
    ct = jnp.stack([jnp.swapaxes(c_re, 1, 2), jnp.swapaxes(c_im, 1, 2)], axis=1)
    ct = jnp.tile(ct, (1, 1, 1, L))
    d_col = d.reshape(G, Cg, 1)
    blk = lambda shp: pl.BlockSpec((None,) + shp, lambda g: (g,) + (0,) * len(shp))
    return pl.pallas_call(
        _s5_weights_kernel,
        grid=(G,),
        in_specs=[blk((P, 4)), blk((4, P)), blk((2, LC, P)), blk((2, P, LC)), blk((Cg, 1))],
        out_specs=(blk((LC, LC)), blk((LC, 2 * P)), blk((2 * P, LC)), blk((2, P))),
        out_shape=(jax.ShapeDtypeStruct((G, LC, LC), BF16),
                   jax.ShapeDtypeStruct((G, LC, 2 * P), BF16),
                   jax.ShapeDtypeStruct((G, 2 * P, LC), BF16),
                   jax.ShapeDtypeStruct((G, 2, P), F32)),
        compiler_params=pltpu.CompilerParams(dimension_semantics=("arbitrary",)),
    )(lam_col, lam_row, bt, ct, d_col)


def _granule_transpose(arrs):
    n = len(arrs)
    granule = lax.broadcasted_iota(jnp.int32, arrs[0].shape, 1) // S5_GROUP
    arrs = list(arrs)
    d = n // 2
    while d >= 1:
        upper = (granule & d) != 0
        for i in range(n):
            if i & d:
                continue
            lo, hi = arrs[i], arrs[i + d]
            arrs[i] = jnp.where(upper, pltpu.roll(hi, d * S5_GROUP, axis=1), lo)
            arrs[i + d] = jnp.where(upper, hi, pltpu.roll(lo, (n - d) * S5_GROUP, axis=1))
        d //= 2
    return arrs


def _s5_in_kernel(x_ref, w_ref, o_ref, u_scr):
    B, tt, D = x_ref.shape
    L, Cg = S5_CHUNK, S5_GROUP
    gpb = 128 // Cg
    cpb = tt // L
    u = _dot(x_ref[...].reshape(B * tt, D).astype(BF16), w_ref[...])
    for j in range(D // 128):
        u_scr[j] = u[:, 128 * j:128 * (j + 1)]
    for j in range(D // 128):
        for h in range(L // gpb):
            pieces = [jnp.concatenate([u_scr[j, pl.ds(c * L + h * gpb + q, B, stride=tt), :]
                                       for c in range(cpb)], axis=0) for q in range(gpb)]
            for gg, dest in enumerate(_granule_transpose(pieces)):
                o_ref[gpb * j + gg, :, 128 * h:128 * (h + 1)] = dest.astype(o_ref.dtype)


def _s5_in(x, w, tt):
    B, T, D = x.shape
    G = D // S5_GROUP
    LC = S5_CHUNK * S5_GROUP
    rows = (tt // S5_CHUNK) * B
    return pl.pallas_call(
        _s5_in_kernel,
        grid=(T // tt,),
        in_specs=[pl.BlockSpec((B, tt, D), lambda i: (0, i, 0)), pl.BlockSpec(w.shape, lambda i: (0, 0))],
        out_specs=pl.BlockSpec((G, rows, LC), lambda i: (0, i, 0)),
        out_shape=jax.ShapeDtypeStruct((G, (T // S5_CHUNK) * B, LC), BF16),
        scratch_shapes=[pltpu.VMEM((D // 128, B * tt, 128), F32)],
        compiler_params=pltpu.CompilerParams(dimension_semantics=("arbitrary",),
                                             vmem_limit_bytes=VMEM_LIMIT_BYTES),
    )(x, w)


def _s5_scan_kernel(x_ref, s0_ref, toe_ref, pst_ref, qst_ref, apow_ref, y_ref, sfin_ref,
                    zr_scr, zi_scr, sr_scr, si_scr, *, n_chunks, batch):
    P = apow_ref.shape[-1]
    for gi in range(x_ref.shape[0]):
        xg = x_ref[gi]
        z = _dot(xg, pst_ref[gi])
        zr_scr[...] = z[:, 0:P]
        zi_scr[...] = z[:, P:2 * P]
        ar = apow_ref[gi, 0:1, :]
        ai = apow_ref[gi, 1:2, :]

        def step(n, carry, ar=ar, ai=ai):
            sr, si = carry
            rows = pl.ds(pl.multiple_of(n * batch, batch), batch)
            sr_scr[rows, :] = sr
            si_scr[rows, :] = si
            return (ar * sr - ai * si + zr_scr[rows, :], ar * si + ai * sr + zi_scr[rows, :])

        sr, si = lax.fori_loop(0, n_chunks, step, (s0_ref[gi, 0], s0_ref[gi, 1]), unroll=4)
        sfin_ref[gi, 0] = sr
        sfin_ref[gi, 1] = si
        y = (_dot(xg, toe_ref[gi]) + _dot(sr_scr[...].astype(BF16), qst_ref[gi, 0:P, :])
             + _dot(si_scr[...].astype(BF16), qst_ref[gi, P:2 * P, :]))
        y_ref[gi] = y.astype(y_ref.dtype)


def _s5_scan(xg, s0, toe, pst, qst, apow, n_chunks, batch):
    G, R, LC = xg.shape
    P = apow.shape[-1]
    gb = max(1, min(8, S5_SCAN_ROWS_PER_STEP // R))
    blk = lambda shp: pl.BlockSpec((gb,) + shp, lambda g: (g,) + (0,) * len(shp))
    return pl.pallas_call(
        functools.partial(_s5_scan_kernel, n_chunks=n_chunks, batch=batch),
        grid=(G // gb,),
        in_specs=[blk((R, LC)), blk((2, batch, P)), blk((LC, LC)), blk((LC, 2 * P)), blk((2 * P, LC)),
                  blk((2, P))],
        out_specs=(blk((R, LC)), blk((2, batch, P))),
        out_shape=(jax.ShapeDtypeStruct((G, R, LC), BF16), jax.ShapeDtypeStruct((G, 2, batch, P), F32)),
        scratch_shapes=[pltpu.VMEM((R, P), F32)] * 4,
        compiler_params=pltpu.CompilerParams(dimension_semantics=("arbitrary",),
                                             vmem_limit_bytes=VMEM_LIMIT_BYTES),
    )(xg, s0, toe, pst, qst, apow)


def _s5_out_kernel(x_ref, yg_ref, w_ref, ln_g_ref, ln_b_ref, o_ref, y_scr):
    B, tt, D = x_ref.shape
    L, Cg = S5_CHUNK, S5_GROUP
    gpb = 128 // Cg
    cpb = tt // L
    for j in range(D // 128):
        for h in range(L // gpb):
            srcs = [yg_ref[gpb * j + gg, :, 128 * h:128 * (h + 1)].astype(F32) for gg in range(gpb)]
            for q, dest in enumerate(_granule_transpose(srcs)):
                for c in range(cpb):
                    y_scr[j, pl.ds(c * L + h * gpb + q, B, stride=tt), :] = dest[c * B:(c + 1) * B, :]
    y = jnp.concatenate([y_scr[j] for j in range(D // 128)], axis=1)
    act = _gelu_tanh(y).astype(BF16)
    zz = _dot(act, w_ref[...])
    mix = zz[:, :D] * _sigmoid(zz[:, D:])
    out = _layer_norm(ALPHA * x_ref[...].reshape(B * tt, D) + mix, ln_g_ref[...], ln_b_ref[...])
    o_ref[...] = out.reshape(B, tt, D)


def _s5_out(x, yg, w_out, ln_g, ln_b, tt):
    B, T, D = x.shape
    G, _, LC = yg.shape
    rows = (tt // S5_CHUNK) * B
    full = lambda a: pl.BlockSpec(a.shape, lambda i: (0,) * a.ndim)
    return pl.pallas_call(
        _s5_out_kernel,
        grid=(T // tt,),
        in_specs=[pl.BlockSpec((B, tt, D), lambda i: (0, i, 0)), pl.BlockSpec((G, rows, LC), lambda i: (0, i, 0)),
                  full(w_out), full(ln_g), full(ln_b)],
        out_specs=pl.BlockSpec((B, tt, D), lambda i: (0, i, 0)),
        out_shape=jax.ShapeDtypeStruct((B, T, D), F32),
        scratch_shapes=[pltpu.VMEM((D // 128, B * tt, 128), F32)],
        compiler_params=pltpu.CompilerParams(dimension_semantics=("arbitrary",),
                                             vmem_limit_bytes=VMEM_LIMIT_BYTES),
    )(x, yg, w_out, ln_g, ln_b)


def _s5_mixer(x, s0_re, s0_im, w_in, w_out, s5w, ln_g, ln_b):
    B, T, D = x.shape
    toe, pst, qst, apow = s5w
    xg = _s5_in(x, w_in, S5_STEP_TOKENS)
    s0 = jnp.stack([s0_re, s0_im], axis=0).transpose(2, 0, 1, 3)
    yg, sfin = _s5_scan(xg, s0, toe, pst, qst, apow, T // S5_CHUNK, B)
    out = _s5_out(x, yg, w_out, ln_g, ln_b, S5_STEP_TOKENS)
    sfin = sfin.transpose(1, 2, 0, 3)
    return out, sfin[0], sfin[1]


def _block_diag(w):
    H, d, _ = w.shape
    eye = jnp.eye(H, dtype=w.dtype)
    return (eye[:, None, :, None] * w[:, :, None, :]).reshape(H * d, H * d)


def kernel(x_prompt, x_sample, state_lru_h, state_lru_conv, state_hgrn2, state_s5_re, state_s5_im, ln_mix_g, ln_mix_b, ln_ffn_g, ln_ffn_b, w_in_ab, w_out_ab, lru_conv_w, lru_conv_b, lru_w_r, lru_b_r, lru_w_i, lru_b_i, lru_lam, hgrn_lb_logits, hgrn_norm_g, w_in_s5, w_out_s5, s5_a_re, s5_a_im, s5_b_re, s5_b_im, s5_c_re, s5_c_im, s5_d, s5_log_dt, router_w, router_b, moe_w_gate, moe_w_up, moe_w_down):
    Bp, Tp, D = x_prompt.shape
    Bs, Ts, _ = x_sample.shape
    W = state_lru_h.shape[-1]
    dk = W // HG_HEADS
    row = lambda v: v.reshape(1, -1)

    def moe(y, l, sparse):
        fn = _moe_sparse if sparse else functools.partial(_moe_dense, tm=512)
        out = fn(y.reshape(-1, D), router_w, row(router_b), moe_w_gate, moe_w_up, moe_w_down, l,
                 row(ln_ffn_g[l]), row(ln_ffn_b[l]))
        return out.reshape(y.shape)

    j = 0
    wts = (w_in_ab[j].astype(BF16), w_out_ab[j].astype(BF16), lru_conv_w[j], row(lru_conv_b[j]),
           _block_diag(lru_w_r[j]).astype(BF16), row(lru_b_r[j]),
           _block_diag(lru_w_i[j]).astype(BF16), row(lru_b_i[j]),
           row(lru_lam[j]), hgrn_lb_logits, row(hgrn_norm_g[j]), row(ln_mix_g[0]), row(ln_mix_b[0]))
    yp, p_h, p_conv, p_hg = _ab_mixer(
        x_prompt, jnp.zeros((Bp, 1, W), F32), jnp.zeros((Bp, 3, W), F32),
        jnp.zeros((Bp, HG_HEADS, dk, dk), F32), wts, AB_TILE, AB_SEQS)
    ys, s_h, s_conv, s_hg = _ab_mixer(
        x_sample, state_lru_h[j][:, None, :], state_lru_conv[j], state_hgrn2[j], wts, Ts, 2)
    yp = moe(yp, 0, True)
    ys = moe(ys, 0, False)

    s5w = _s5_weights(s5_a_re[j], s5_a_im[j], s5_log_dt[j], s5_b_re[j], s5_b_im[j],
                      s5_c_re[j], s5_c_im[j], s5_d[j])
    G, P = s5_a_re[j].shape
    w_in5 = w_in_s5[j].astype(BF16)
    w_out5 = w_out_s5[j].astype(BF16)
    zp = jnp.zeros((Bp, G, P), F32)
    yp, p_re, p_im = _s5_mixer(yp, zp, zp, w_in5, w_out5, s5w, row(ln_mix_g[1]), row(ln_mix_b[1]))
    ys, s_re, s_im = _s5_mixer(ys, state_s5_re[j], state_s5_im[j], w_in5, w_out5, s5w,
                               row(ln_mix_g[1]), row(ln_mix_b[1]))
    yp = moe(yp, 1, True)
    ys = moe(ys, 1, False)

    return (yp, ys,
            p_h.reshape(1, Bp, W), p_conv[None], p_hg[None], p_re[None], p_im[None],
            s_h.reshape(1, Bs, W), s_conv[None], s_hg[None], s_re[None], s_im[None])
```

```python
import functools
import math

import jax
import jax.numpy as jnp
from jax import lax
from jax.experimental import pallas as pl
from jax.experimental.pallas import tpu as pltpu

F32 = jnp.float32
BF16 = jnp.bfloat16
HIGHEST = lax.Precision.HIGHEST

DEPTH = 2
RG_C = 8.0
LRU_HEADS = 8
HG_HEADS = 4
AB_TILE = 512
AB_SEQS = 1
HG_CHUNK = 16
RMS_EPS = 1e-6
S5_GROUP = 16
S5_SCAN_ROWS_PER_STEP = 2048
S5_STEP_TOKENS = 32
S5_CHUNK = 16
N_EXPERT_GROUPS = 4
ALPHA = (2.0 * DEPTH) ** 0.25
LN_EPS = 1e-5
MOE_TILE = 256
MOE_CHUNK = 16
MOE_SLOTS = 768
MOE_BLOCK = 1024
COMBINE_DEPTH = 3
ROUTE_TILES_PER_STEP = 4
VMEM_LIMIT_BYTES = 56 * 1024 * 1024


def _dot(a, b):
    return jnp.dot(a, b, preferred_element_type=F32)


def _dot_nt(a, b):
    return lax.dot_general(a, b, (((1,), (1,)), ((), ())), preferred_element_type=F32)


def _sigmoid(x):
    return 0.5 * jnp.tanh(0.5 * x) + 0.5


def _gelu_tanh(x):
    c = math.sqrt(2.0 / math.pi)
    return 0.5 * x * (1.0 + jnp.tanh(c * (x + 0.044715 * (x * x * x))))


def _layer_norm(s, g, b):
    mu = jnp.mean(s, axis=-1, keepdims=True)
    d = s - mu
    var = jnp.mean(d * d, axis=-1, keepdims=True)
    return d * lax.rsqrt(var + LN_EPS) * g + b


def _shift_rows(x, s, row):
    del row
    return pltpu.roll(x, s, axis=0)


def _ab_kernel(x_ref, h0_ref, conv0_ref, s0_ref, w_in_ref, w_out_ref, conv_w_ref, conv_b_ref,
               wr_ref, br_ref, wi_ref, bi_ref, lam_ref, lb_logits_ref, norm_g_ref, ln_g_ref, ln_b_ref,
               y_ref, h_out_ref, conv_out_ref, s_out_ref,
               zx_buf, h_carry, st_carry, st_all, *, tb, width, nb):
    for si in range(nb):
        _ab_sequence_tile(si, x_ref, h0_ref, conv0_ref, s0_ref, w_in_ref, w_out_ref, conv_w_ref, conv_b_ref,
                          wr_ref, br_ref, wi_ref, bi_ref, lam_ref, lb_logits_ref, norm_g_ref, ln_g_ref, ln_b_ref,
                          y_ref, h_out_ref, conv_out_ref, s_out_ref,
                          zx_buf.at[si], h_carry.at[si], st_carry.at[si], st_all.at[si], tb, width)


def _ab_sequence_tile(si, x_ref, h0_ref, conv0_ref, s0_ref, w_in_ref, w_out_ref, conv_w_ref, conv_b_ref,
                      wr_ref, br_ref, wi_ref, bi_ref, lam_ref, lb_logits_ref, norm_g_ref, ln_g_ref, ln_b_ref,
                      y_ref, h_out_ref, conv_out_ref, s_out_ref, zx_buf, h_carry, st_carry, st_all, tb, width):
    t = pl.program_id(1)
    nt = pl.num_programs(1)
    W = width
    L = HG_CHUNK
    nc = tb // L
    dk = W // HG_HEADS

    @pl.when(t == 0)
    def _init():
        zx_buf[0:8, :] = jnp.zeros((8, W), F32)
        zx_buf[5:8, :] = conv0_ref[si]
        h_carry[...] = jnp.broadcast_to(h0_ref[si], h_carry.shape)
        for hd in range(HG_HEADS):
            st_carry[hd] = s0_ref[si, hd].T

    x = x_ref[si]
    z = _dot(x.astype(BF16), w_in_ref[...])
    z_gate = z[:, 0 * W:1 * W]
    z_x = z[:, 1 * W:2 * W]
    z_q = z[:, 2 * W:3 * W]
    z_f = z[:, 3 * W:4 * W]
    z_i = z[:, 4 * W:5 * W]
    z_g = z[:, 5 * W:6 * W]

    zx_buf[8:8 + tb, :] = z_x
    cw = conv_w_ref[...]
    u = (conv_b_ref[...] + cw[3:4, :] * z_x
         + cw[2:3, :] * zx_buf[7:7 + tb, :]
         + cw[1:2, :] * zx_buf[6:6 + tb, :]
         + cw[0:1, :] * zx_buf[5:5 + tb, :])
    tail = zx_buf[tb:tb + 8, :]
    zx_buf[0:8, :] = tail

    ub = u.astype(BF16)
    r = _sigmoid(_dot(ub, wr_ref[...]) + br_ref[...])
    ig = _sigmoid(_dot(ub, wi_ref[...]) + bi_ref[...])
    nlam = -lam_ref[...]
    softplus_nlam = jnp.maximum(nlam, 0.0) + jnp.log1p(jnp.exp(-jnp.abs(nlam)))
    log_a = (-RG_C) * r * softplus_nlam
    a = jnp.exp(log_a)
    th = jnp.tanh(log_a)
    b = jnp.sqrt(-2.0 * th / (1.0 - th)) * ig * u

    row = lax.broadcasted_iota(jnp.int32, (tb, W), 0)
    rin8 = row % 8
    s = 1
    while s < 8:
        m = rin8 >= s
        a_sh = _shift_rows(a, s, row)
        b_sh = _shift_rows(b, s, row)
        b = jnp.where(m, a * b_sh + b, b)
        a = jnp.where(m, a * a_sh, a)
        s *= 2
    a3 = a.reshape(tb // 8, 8, W)
    b3 = b.reshape(tb // 8, 8, W)
    h_prev = h_carry[0:1, :]
    h_groups = []
    for g in range(tb // 8):
        hg = b3[g] + a3[g] * h_prev
        h_groups.append(hg)
        h_prev = hg[7:8, :]
    h = jnp.concatenate(h_groups, axis=0)
    h_carry[...] = jnp.broadcast_to(h_prev, h_carry.shape)
    out_a = h * _gelu_tanh(z_gate)

    lg = lb_logits_ref[...]
    e = jnp.exp(lg - jnp.max(lg, axis=0, keepdims=True))
    lb = e[0:1, :] / jnp.sum(e, axis=0, keepdims=True)
    f = lb + (1.0 - lb) * _sigmoid(z_f)
    log_f = jnp.log(f)
    kk = 1.0 - f
    rin = row % L
    G = log_f
    s = 1
    while s < L:
        G = G + jnp.where(rin >= s, _shift_rows(G, s, row), 0.0)
        s *= 2
    G3 = G.reshape(nc, L, W)
    G_last = G3[:, L - 1:L, :]
    q_dec = (z_q * jnp.exp(G)).reshape(nc, L, W).astype(BF16)
    k_inc = (kk * jnp.exp(-G)).reshape(nc, L, W).astype(BF16)
    k_tail = (kk.reshape(nc, L, W) * jnp.exp(G_last - G3)).astype(BF16)
    decay = jnp.exp(G_last)
    v3 = z_i.reshape(nc, L, W).astype(BF16)
    causal = (lax.broadcasted_iota(jnp.int32, (nc, L, L), 1)
              >= lax.broadcasted_iota(jnp.int32, (nc, L, L), 2))

    o_heads = []
    for hd in range(HG_HEADS):
        sl = slice(hd * dk, (hd + 1) * dk)
        qd, ki, kt, vv = q_dec[:, :, sl], k_inc[:, :, sl], k_tail[:, :, sl], v3[:, :, sl]
        scores = jnp.einsum('ntk,nsk->nts', qd, ki, preferred_element_type=F32)
        scores = jnp.where(causal, scores, 0.0).astype(BF16)
        o_intra = jnp.einsum('nts,nsv->ntv', scores, vv, preferred_element_type=F32)
        vt = jnp.swapaxes(vv.astype(F32), 1, 2).astype(BF16)
        upd = jnp.einsum('nvs,nsk->nvk', vt, kt, preferred_element_type=F32)
        st = st_carry[hd]
        for n in range(nc):
            st_all[hd, n] = st.astype(BF16)
            st = st * decay[n, :, sl] + upd[n]
        st_carry[hd] = st
        o_inter = jnp.einsum('ntk,nvk->ntv', qd, st_all[hd], preferred_element_type=F32)
        o = (o_intra + o_inter).reshape(tb, dk)
        o = o * lax.rsqrt(jnp.mean(o * o, axis=-1, keepdims=True) + RMS_EPS)
        o_heads.append(o)
    o_all = jnp.concatenate(o_heads, axis=-1) * norm_g_ref[...]
    out_b = o_all * (z_g * _sigmoid(z_g))

    mixed = jnp.concatenate([out_a, out_b], axis=-1).astype(BF16)
    y = _dot(mixed, w_out_ref[...])
    y_ref[si] = _layer_norm(ALPHA * x + y, ln_g_ref[...], ln_b_ref[...])

    @pl.when(t == nt - 1)
    def _fin():
        h_out_ref[si] = h[tb - 1:tb, :]
        conv_out_ref[si] = tail[5:8, :]
        for hd in range(HG_HEADS):
            s_out_ref[si, hd] = st_carry[hd].T


def _ab_mixer(x, h0, conv0, s0, wts, tb, nb):
    B, T, D = x.shape
    W = h0.shape[-1]
    dk = W // HG_HEADS
    nt = T // tb
    nc = tb // HG_CHUNK
    (w_in, w_out, conv_w, conv_b, wr, br, wi, bi, lam, lb_logits, norm_g, ln_g, ln_b) = wts
    full = lambda a: pl.BlockSpec(a.shape, lambda b, t: (0,) * a.ndim)
    per_b = lambda shp: pl.BlockSpec((nb,) + shp, lambda b, t: (b,) + (0,) * len(shp))
    out_shape = (jax.ShapeDtypeStruct((B, T, D), F32),
                 jax.ShapeDtypeStruct((B, 1, W), F32),
                 jax.ShapeDtypeStruct((B, 3, W), F32),
                 jax.ShapeDtypeStruct((B, HG_HEADS, dk, dk), F32))
    return pl.pallas_call(
        functools.partial(_ab_kernel, tb=tb, width=W, nb=nb),
        grid=(B // nb, nt),
        in_specs=[pl.BlockSpec((nb, tb, D), lambda b, t: (b, t, 0)),
                  per_b((1, W)), per_b((3, W)), per_b((HG_HEADS, dk, dk))]
                 + [full(a) for a in wts],
        out_specs=(pl.BlockSpec((nb, tb, D), lambda b, t: (b, t, 0)),
                   per_b((1, W)), per_b((3, W)), per_b((HG_HEADS, dk, dk))),
        out_shape=out_shape,
        scratch_shapes=[pltpu.VMEM((nb, tb + 8, W), F32),
                        pltpu.VMEM((nb, 8, W), F32),
                        pltpu.VMEM((nb, HG_HEADS, dk, dk), F32),
                        pltpu.VMEM((nb, HG_HEADS, nc, dk, dk), BF16)],
        compiler_params=pltpu.CompilerParams(
            dimension_semantics=("arbitrary", "arbitrary"),
            vmem_limit_bytes=VMEM_LIMIT_BYTES),
    )(x, h0, conv0, s0, *wts)


def _router(x, rw, rb, n_experts):
    per_group = n_experts // N_EXPERT_GROUPS
    logits = jnp.dot(x, rw, preferred_element_type=F32, precision=HIGHEST) + rb
    logits = logits - jnp.max(logits, axis=-1, keepdims=True)
    ex = jnp.exp(logits)
    p = ex / jnp.sum(ex, axis=-1, keepdims=True)
    lane_i = lax.broadcasted_iota(jnp.int32, p.shape, 1)
    grp = lane_i // per_group
    lane = lane_i.astype(F32)
    big = float(n_experts)

    def top2(pm):
        m1 = jnp.max(pm, axis=-1, keepdims=True)
        i1 = jnp.min(jnp.where(pm == m1, lane, big), axis=-1, keepdims=True)
        pm2 = jnp.where(lane == i1, -1.0, pm)
        m2 = jnp.max(pm2, axis=-1, keepdims=True)
        i2 = jnp.min(jnp.where(pm2 == m2, lane, big), axis=-1, keepdims=True)
        return m1, i1, m2, i2

    best = None
    g_sel = None
    for j in range(N_EXPERT_GROUPS):
        m1, _, m2, _ = top2(jnp.where(grp == j, p, -1.0))
        score = m1 + m2
        if best is None:
            best, g_sel = score, jnp.zeros_like(lane_i[:, :1])
        else:
            better = score > best
            best = jnp.where(better, score, best)
            g_sel = jnp.where(better, j, g_sel)
    m1, i1, m2, i2 = top2(jnp.where(grp == g_sel, p, -1.0))
    den = m1 + m2
    return jnp.where(lane == i1, m1 / den, 0.0) + jnp.where(lane == i2, m2 / den, 0.0)


def _moe_dense_kernel(x_ref, rw_ref, rb_ref, wg_ref, wu_ref, wd_ref, ln_g_ref, ln_b_ref,
                      y_ref, xb_scr, comb_scr, acc_scr, *, n_experts):
    e = pl.program_id(1)

    @pl.when(e == 0)
    def _init():
        x = x_ref[...]
        xb_scr[...] = x.astype(BF16)
        comb_scr[...] = _router(x, rw_ref[...], rb_ref[...], n_experts)
        acc_scr[...] = jnp.zeros_like(acc_scr)

    comb = comb_scr[...]
    lane = lax.broadcasted_iota(jnp.int32, comb.shape, 1)
    c_e = jnp.sum(jnp.where(lane == e, comb, 0.0), axis=-1, keepdims=True)
    xb = xb_scr[...]
    g = _dot(xb, wg_ref[...].astype(BF16))
    up = _dot(xb, wu_ref[...].astype(BF16))
    hmid = (g * _sigmoid(g) * up).astype(BF16)
    acc_scr[...] += c_e * _dot(hmid, wd_ref[...].astype(BF16))

    @pl.when(e == n_experts - 1)
    def _fin():
        y_ref[...] = _layer_norm(ALPHA * x_ref[...] + acc_scr[...], ln_g_ref[...], ln_b_ref[...])


def _moe_dense(x2d, rw, rb, wg, wu, wd, layer, ln_g, ln_b, tm):
    N, D = x2d.shape
    _, E, _, De = wg.shape
    full = lambda a: pl.BlockSpec(a.shape, lambda i, e: (0,) * a.ndim)
    return pl.pallas_call(
        functools.partial(_moe_dense_kernel, n_experts=E),
        grid=(N // tm, E),
        in_specs=[pl.BlockSpec((tm, D), lambda i, e: (i, 0)), full(rw), full(rb),
                  pl.BlockSpec((None, None, D, De), lambda i, e: (layer, e, 0, 0)),
                  pl.BlockSpec((None, None, D, De), lambda i, e: (layer, e, 0, 0)),
                  pl.BlockSpec((None, None, De, D), lambda i, e: (layer, e, 0, 0)),
                  full(ln_g), full(ln_b)],
        out_specs=pl.BlockSpec((tm, D), lambda i, e: (i, 0)),
        out_shape=jax.ShapeDtypeStruct((N, D), F32),
        scratch_shapes=[pltpu.VMEM((tm, D), BF16), pltpu.VMEM((tm, E), F32), pltpu.VMEM((tm, D), F32)],
        compiler_params=pltpu.CompilerParams(
            dimension_semantics=("arbitrary", "arbitrary"),
            vmem_limit_bytes=VMEM_LIMIT_BYTES),
    )(x2d, rw, rb, wg, wu, wd, ln_g, ln_b)


def _route_kernel(x_ref, rwt_ref, rbc_ref, meta_ref, pc_ref, *, n_experts):
    E = n_experts
    per_group = E // N_EXPERT_GROUPS
    x = x_ref[...]
    tm = x.shape[0]
    logits = lax.dot_general(rwt_ref[...], x, (((1,), (1,)), ((), ())),
                             precision=HIGHEST, preferred_element_type=F32) + rbc_ref[...]
    logits = logits - jnp.max(logits, axis=0, keepdims=True)
    ex = jnp.exp(logits)
    p = ex / jnp.sum(ex, axis=0, keepdims=True)
    eid_i = lax.broadcasted_iota(jnp.int32, (E, tm), 0)
    grp = eid_i // per_group
    eid = eid_i.astype(F32)
    big = float(E)

    def top2(pm):
        m1 = jnp.max(pm, axis=0, keepdims=True)
        i1 = jnp.min(jnp.where(pm == m1, eid, big), axis=0, keepdims=True)
        pm2 = jnp.where(eid == i1, -1.0, pm)
        m2 = jnp.max(pm2, axis=0, keepdims=True)
        i2 = jnp.min(jnp.where(pm2 == m2, eid, big), axis=0, keepdims=True)
        return m1, i1, m2, i2

    best = None
    g_sel = None
    for j in range(N_EXPERT_GROUPS):
        m1, _, m2, _ = top2(jnp.where(grp == j, p, -1.0))
        score = m1 + m2
        if best is None:
            best, g_sel = score, jnp.zeros_like(eid_i[:1, :])
        else:
            better = score > best
            best = jnp.where(better, score, best)
            g_sel = jnp.where(better, j, g_sel)
    m1, i1, m2, i2 = top2(jnp.where(grp == g_sel, p, -1.0))
    den = m1 + m2

    oh1_all = jnp.where(eid == i1, 1.0, 0.0)
    oh2_all = jnp.where(eid == i2, 1.0, 0.0)
    g1_all = m1 / den
    g2_all = m2 / den
    tl = MOE_TILE
    earlier = jnp.where(lax.broadcasted_iota(jnp.int32, (tl, tl), 0)
                        < lax.broadcasted_iota(jnp.int32, (tl, tl), 1), 1.0, 0.0).astype(BF16)
    below = jnp.where(lax.broadcasted_iota(jnp.int32, (E, E), 1)
                      < lax.broadcasted_iota(jnp.int32, (E, E), 0), 1.0, 0.0)
    for k in range(tm // tl):
        lanes = slice(k * tl, (k + 1) * tl)
        oh1, oh2 = oh1_all[:, lanes], oh2_all[:, lanes]
        r1 = _dot(oh1.astype(BF16), earlier)
        r2 = _dot(oh2.astype(BF16), earlier)
        cnt1 = jnp.sum(oh1, axis=1, keepdims=True)
        cnt2 = jnp.sum(oh2, axis=1, keepdims=True)
        pc = jnp.floor((cnt1 + cnt2 + (MOE_CHUNK - 1.0)) * (1.0 / MOE_CHUNK)) * MOE_CHUNK
        pc_b = jnp.broadcast_to(pc, (E, 128))
        run_off = jnp.dot(below, pc_b, precision=HIGHEST, preferred_element_type=F32)[:, 0:1]
        meta_ref[k, 0:1, :] = jnp.sum(oh1 * (run_off + r1), axis=0, keepdims=True)
        meta_ref[k, 1:2, :] = jnp.sum(oh2 * (run_off + cnt1 + r2), axis=0, keepdims=True)
        meta_ref[k, 2:3, :] = g1_all[:, lanes]
        meta_ref[k, 3:4, :] = g2_all[:, lanes]
        meta_ref[k, 4:8, :] = jnp.zeros((4, tl), F32)
        pc_ref[k] = pc_b


def _route(x2d, rwt, rbc):
    N, D = x2d.shape
    E = rwt.shape[0]
    nt = N // MOE_TILE
    tps = ROUTE_TILES_PER_STEP if nt % ROUTE_TILES_PER_STEP == 0 else 1
    full = lambda a: pl.BlockSpec(a.shape, lambda i: (0,) * a.ndim)
    return pl.pallas_call(
        functools.partial(_route_kernel, n_experts=E),
        grid=(nt // tps,),
        in_specs=[pl.BlockSpec((tps * MOE_TILE, D), lambda i: (i, 0)), full(rwt), full(rbc)],
        out_specs=(pl.BlockSpec((tps, 8, MOE_TILE), lambda i: (i, 0, 0)),
                   pl.BlockSpec((tps, E, 128), lambda i: (i, 0, 0))),
        out_shape=(jax.ShapeDtypeStruct((nt, 8, MOE_TILE), F32), jax.ShapeDtypeStruct((nt, E, 128), F32)),
        compiler_params=pltpu.CompilerParams(dimension_semantics=("arbitrary",),
                                             vmem_limit_bytes=VMEM_LIMIT_BYTES),
    )(x2d, rwt, rbc)


def _moe_schedule(pc, n_blocks_max):
    nt, E = pc.shape
    cum = jnp.cumsum(pc, axis=1)
    run_off = cum - pc
    total = cum[:, -1]
    seg_len = jnp.sum(pc, axis=0)
    seg_pad = ((seg_len + MOE_BLOCK - 1) // MOE_BLOCK) * MOE_BLOCK
    seg_end = jnp.cumsum(seg_pad)
    goff = (seg_end - seg_pad)[None, :] + (jnp.cumsum(pc, axis=0) - pc)
    r = jnp.arange(MOE_SLOTS // MOE_CHUNK, dtype=jnp.int32) * MOE_CHUNK
    e_c = jnp.minimum(jnp.sum((cum[:, None, :] <= r[None, :, None]).astype(jnp.int32), axis=-1), E - 1)
    pick = e_c[:, :, None] == jnp.arange(E, dtype=jnp.int32)[None, None, :]
    dst_row = r[None, :] + jnp.sum(jnp.where(pick, (goff - run_off)[:, None, :], 0), axis=-1)
    dst_chunk = jnp.where(r[None, :] < total[:, None], dst_row // MOE_CHUNK, -1).astype(jnp.int32)
    blk_start = jnp.arange(n_blocks_max, dtype=jnp.int32) * MOE_BLOCK
    block_expert = jnp.minimum(jnp.sum((seg_end[None, :] <= blk_start[:, None]).astype(jnp.int32), axis=-1), E - 1)
    n_valid = (seg_end[-1] // MOE_BLOCK).reshape(1).astype(jnp.int32)
    n_chunks_all = n_blocks_max * (MOE_BLOCK // MOE_CHUNK)
    n_fill_max = n_chunks_all - (nt * 2 * MOE_TILE) // MOE_CHUNK
    gap_start = jnp.concatenate([seg_end - seg_pad + seg_len, seg_end[-1:]]) // MOE_CHUNK
    gap_len = jnp.concatenate([seg_pad - seg_len, n_chunks_all * MOE_CHUNK - seg_end[-1:]]) // MOE_CHUNK
    gap_cum = jnp.cumsum(gap_len)
    k = jnp.arange(n_fill_max, dtype=jnp.int32)
    which = jnp.minimum(jnp.sum((gap_cum[None, :] <= k[:, None]).astype(jnp.int32), axis=-1), E)
    pick_gap = which[:, None] == jnp.arange(E + 1, dtype=jnp.int32)[None, :]
    fill = k + jnp.sum(jnp.where(pick_gap, (gap_start - (gap_cum - gap_len))[None, :], 0), axis=-1)
    fill = jnp.where(k < gap_cum[-1], fill, 0).astype(jnp.int32)
    n_fill = gap_cum[-1].reshape(1).astype(jnp.int32)
    return (dst_chunk.reshape(-1), total.astype(jnp.int32), block_expert.astype(jnp.int32), n_valid, fill, n_fill)


def _chunk_copy(hbm_ref, buf, sem, b, c, d, to_hbm):
    local = buf.at[b, pl.ds(pl.multiple_of(c * MOE_CHUNK, MOE_CHUNK), MOE_CHUNK)]
    remote = hbm_ref.at[pl.ds(pl.multiple_of(d * MOE_CHUNK, MOE_CHUNK), MOE_CHUNK)]
    if to_hbm:
        return pltpu.make_async_copy(local, remote, sem.at[b])
    return pltpu.make_async_copy(remote, local, sem.at[b])


def _for_tile_chunks(dst_ref, tot_ref, tile, fn):
    slots = MOE_SLOTS // MOE_CHUNK

    def body(c, carry):
        fn(c, dst_ref[tile * slots + c])
        return carry

    lax.fori_loop(0, tot_ref[tile] // MOE_CHUNK, body, 0)


def _dispatch_kernel(dst_ref, tot_ref, fill_ref, nfill_ref, x_ref, meta_ref, xs_hbm, buf, sem, zbuf, zsem):
    i = pl.program_id(0)
    n = pl.num_programs(0)
    b = i % 2
    tm = x_ref.shape[0]

    def wait_tile(tile, slot):
        _for_tile_chunks(dst_ref, tot_ref, tile,
                         lambda c, d: _chunk_copy(xs_hbm, buf, sem, slot, c, d, True).wait())

    def zero_copy(k):
        d = fill_ref[k]
        return pltpu.make_async_copy(
            zbuf, xs_hbm.at[pl.ds(pl.multiple_of(d * MOE_CHUNK, MOE_CHUNK), MOE_CHUNK)], zsem.at[0])

    @pl.when(i == 0)
    def _():
        zbuf[...] = jnp.zeros_like(zbuf)
        lax.fori_loop(0, nfill_ref[0], lambda k, c: (zero_copy(k).start(), c)[1], 0)

    @pl.when(i >= 2)
    def _():
        wait_tile(i - 2, b)

    s1 = meta_ref[0:1, :].astype(jnp.int32)
    s2 = meta_ref[1:2, :].astype(jnp.int32)
    rows = lax.broadcasted_iota(jnp.int32, (MOE_SLOTS, tm), 0)
    dsp = jnp.where((rows == s1) | (rows == s2), 1.0, 0.0).astype(BF16)
    buf[b] = _dot(dsp, x_ref[...].astype(BF16)).astype(BF16)
    _for_tile_chunks(dst_ref, tot_ref, i, lambda c, d: _chunk_copy(xs_hbm, buf, sem, b, c, d, True).start())

    @pl.when(i == n - 1)
    def _():
        wait_tile(i, b)

    @pl.when((i == n - 1) & (i >= 1))
    def _():
        wait_tile(i - 1, 1 - b)

    @pl.when(i == n - 1)
    def _():
        lax.fori_loop(0, nfill_ref[0], lambda k, c: (zero_copy(k).wait(), c)[1], 0)


def _dispatch(x2d, meta, dst_chunk, total, fill, n_fill, n_rows):
    N, D = x2d.shape
    nt = N // MOE_TILE
    return pl.pallas_call(
        _dispatch_kernel,
        grid_spec=pltpu.PrefetchScalarGridSpec(
            num_scalar_prefetch=4, grid=(nt,),
            in_specs=[pl.BlockSpec((MOE_TILE, D), lambda i, *_: (i, 0)),
                      pl.BlockSpec((None, 8, MOE_TILE), lambda i, *_: (i, 0, 0))],
            out_specs=pl.BlockSpec(memory_space=pl.ANY),
            scratch_shapes=[pltpu.VMEM((2, MOE_SLOTS, D), BF16), pltpu.SemaphoreType.DMA((2,)),
                            pltpu.VMEM((MOE_CHUNK, D), BF16), pltpu.SemaphoreType.DMA((1,))]),
        out_shape=jax.ShapeDtypeStruct((n_rows, D), BF16),
        compiler_params=pltpu.CompilerParams(dimension_semantics=("arbitrary",),
                                             vmem_limit_bytes=VMEM_LIMIT_BYTES),
    )(dst_chunk, total, fill, n_fill, x2d, meta)


def _ffn_kernel(be_ref, nv_ref, xs_ref, wg_ref, wu_ref, wd_ref, ys_ref, wg_b, wu_b, wd_b):
    j = pl.program_id(0)
    used = j < nv_ref[0]
    new_expert = (j == 0) | (be_ref[j] != be_ref[jnp.maximum(j - 1, 0)])

    @pl.when(used & new_expert)
    def _():
        wg_b[...] = wg_ref[...].astype(BF16)
        wu_b[...] = wu_ref[...].astype(BF16)
        wd_b[...] = wd_ref[...].astype(BF16)

    @pl.when(used)
    def _():
        xb = xs_ref[...]
        g = _dot(xb, wg_b[...])
        up = _dot(xb, wu_b[...])
        hmid = (g * _sigmoid(g) * up).astype(BF16)
        ys_ref[...] = _dot(hmid, wd_b[...]).astype(ys_ref.dtype)

    @pl.when(jnp.logical_not(used))
    def _():
        ys_ref[...] = jnp.zeros_like(ys_ref)


def _ffn(xs, wg, wu, wd, layer, block_expert, n_valid):
    R, D = xs.shape
    _, E, _, De = wg.shape
    last = lambda j, nv: jnp.maximum(jnp.minimum(j, nv[0] - 1), 0)
    return pl.pallas_call(
        _ffn_kernel,
        grid_spec=pltpu.PrefetchScalarGridSpec(
            num_scalar_prefetch=2, grid=(R // MOE_BLOCK,),
            in_specs=[pl.BlockSpec((MOE_BLOCK, D), lambda j, be, nv: (last(j, nv), 0)),
                      pl.BlockSpec((None, None, D, De), lambda j, be, nv: (layer, be[last(j, nv)], 0, 0)),
                      pl.BlockSpec((None, None, D, De), lambda j, be, nv: (layer, be[last(j, nv)], 0, 0)),
                      pl.BlockSpec((None, None, De, D), lambda j, be, nv: (layer, be[last(j, nv)], 0, 0))],
            out_specs=pl.BlockSpec((MOE_BLOCK, D), lambda j, be, nv: (j, 0)),
            scratch_shapes=[pltpu.VMEM((D, De), BF16), pltpu.VMEM((D, De), BF16), pltpu.VMEM((De, D), BF16)]),
        out_shape=jax.ShapeDtypeStruct((R, D), BF16),
        compiler_params=pltpu.CompilerParams(dimension_semantics=("arbitrary",),
                                             vmem_limit_bytes=VMEM_LIMIT_BYTES),
    )(block_expert, n_valid, xs, wg, wu, wd)


def _combine_kernel(dst_ref, tot_ref, x_ref, metac_ref, ys_hbm, ln_g_ref, ln_b_ref, o_ref, buf, sem):
    i = pl.program_id(0)
    n = pl.num_programs(0)
    depth = buf.shape[0]
    b = i % depth
    tm = x_ref.shape[0]

    def fetch(tile, slot):
        _for_tile_chunks(dst_ref, tot_ref, tile,
                         lambda c, d: _chunk_copy(ys_hbm, buf, sem, slot, c, d, False).start())

    @pl.when(i == 0)
    def _():
        for k in range(depth - 1):
            @pl.when(k < n)
            def _():
                fetch(k, k)

    @pl.when(i + (depth - 1) < n)
    def _():
        fetch(i + (depth - 1), (i + (depth - 1)) % depth)

    _for_tile_chunks(dst_ref, tot_ref, i, lambda c, d: _chunk_copy(ys_hbm, buf, sem, b, c, d, False).wait())
    live = lax.broadcasted_iota(jnp.int32, (MOE_SLOTS, 1), 0) < tot_ref[i]
    ys = jnp.where(live, buf[b], jnp.zeros((), BF16))
    mc = metac_ref[...]
    s1 = mc[:, 0:1].astype(jnp.int32)
    s2 = mc[:, 1:2].astype(jnp.int32)
    lane = lax.broadcasted_iota(jnp.int32, (tm, MOE_SLOTS), 1)
    cmb = (jnp.where(lane == s1, mc[:, 2:3], 0.0) + jnp.where(lane == s2, mc[:, 3:4], 0.0)).astype(BF16)
    o_ref[...] = _layer_norm(ALPHA * x_ref[...] + _dot(cmb, ys), ln_g_ref[...], ln_b_ref[...])


def _combine(x2d, meta_col, ys, dst_chunk, total, ln_g, ln_b):
    N, D = x2d.shape
    nt = N // MOE_TILE
    return pl.pallas_call(
        _combine_kernel,
        grid_spec=pltpu.PrefetchScalarGridSpec(
            num_scalar_prefetch=2, grid=(nt,),
            in_specs=[pl.BlockSpec((MOE_TILE, D), lambda i, dst, tot: (i, 0)),
                      pl.BlockSpec((MOE_TILE, 8), lambda i, dst, tot: (i, 0)),
                      pl.BlockSpec(memory_space=pl.ANY),
                      pl.BlockSpec(ln_g.shape, lambda i, dst, tot: (0, 0)),
                      pl.BlockSpec(ln_b.shape, lambda i, dst, tot: (0, 0))],
            out_specs=pl.BlockSpec((MOE_TILE, D), lambda i, dst, tot: (i, 0)),
            scratch_shapes=[pltpu.VMEM((COMBINE_DEPTH, MOE_SLOTS, D), BF16),
                            pltpu.SemaphoreType.DMA((COMBINE_DEPTH,))]),
        out_shape=jax.ShapeDtypeStruct((N, D), F32),
        compiler_params=pltpu.CompilerParams(dimension_semantics=("arbitrary",),
                                             vmem_limit_bytes=VMEM_LIMIT_BYTES),
    )(dst_chunk, total, x2d, meta_col, ys, ln_g, ln_b)


def _moe_sparse(x2d, rw, rb, wg, wu, wd, layer, ln_g, ln_b):
    N, D = x2d.shape
    E = rw.shape[1]
    nt = N // MOE_TILE
    meta, pc = _route(x2d, rw.T, rb.reshape(E, 1))
    pc = pc[:, :, 0].astype(jnp.int32)
    n_rows = nt * MOE_SLOTS + E * MOE_BLOCK
    assert n_rows % MOE_BLOCK == 0
    dst_chunk, total, block_expert, n_valid, fill, n_fill = _moe_schedule(pc, n_rows // MOE_BLOCK)
    xs = _dispatch(x2d, meta, dst_chunk, total, fill, n_fill, n_rows)
    ys = _ffn(xs, wg, wu, wd, layer, block_expert, n_valid)
    meta_col = jnp.swapaxes(meta, 1, 2).reshape(N, 8)
    return _combine(x2d, meta_col, ys, dst_chunk, total, ln_g, ln_b)


def _s5_weights_kernel(lam_col_ref, lam_row_ref, bt_ref, ct_ref, d_ref,
                       toe_ref, pst_ref, qst_ref, a_pow_ref):
    L, Cg = S5_CHUNK, S5_GROUP
    LC = L * Cg
    P = lam_row_ref.shape[-1]

    def discretise(a_re, a_im, log_dt):
        dt = jnp.exp(log_dt)
        lr, li = dt * a_re, dt * a_im
        er = jnp.exp(lr)
        abr, abi = er * jnp.cos(li), er * jnp.sin(li)
        nr, ni = abr - 1.0, abi
        den = a_re * a_re + a_im * a_im
        cr = (nr * a_re + ni * a_im) / den
        ci = (ni * a_re - nr * a_im) / den
        return lr, li, cr, ci

    def a_power(lr, li, tau):
        er = jnp.exp(tau * lr)
        return er * jnp.cos(tau * li), er * jnp.sin(tau * li)

    lc = lam_col_ref[...]
    lr_c, li_c, _, _ = discretise(lc[:, 0:1], lc[:, 1:2], lc[:, 2:3])
    lw = lam_row_ref[...]
    lr_r, li_r, cr_r, ci_r = discretise(lw[0:1, :], lw[1:2, :], lw[2:3, :])

    btr, bti = bt_ref[0], bt_ref[1]
    bbr = cr_r * btr - ci_r * bti
    bbi = cr_r * bti + ci_r * btr

    ctr, cti = ct_ref[0], ct_ref[1]
    tau_l = (lax.broadcasted_iota(jnp.int32, (P, LC), 1) // Cg).astype(F32)

    pr, pi = a_power(lr_c, li_c, tau_l)
    car = ctr * pr - cti * pi
    cai = ctr * pi + cti * pr
    t0 = (jnp.dot(bbr[0:Cg, :], car, preferred_element_type=F32, precision=HIGHEST)
          - jnp.dot(bbi[0:Cg, :], cai, preferred_element_type=F32, precision=HIGHEST))
    r16 = lax.broadcasted_iota(jnp.int32, (Cg, LC), 0)
    l16 = lax.broadcasted_iota(jnp.int32, (Cg, LC), 1)
    t0 = t0 + jnp.where(r16 == l16, d_ref[...], 0.0)
    toe_ref[0:Cg, :] = t0.astype(toe_ref.dtype)
    for s in range(1, L):
        toe_ref[s * Cg:(s + 1) * Cg, :] = jnp.where(
            l16 >= s * Cg, pltpu.roll(t0, s * Cg, axis=1), 0.0).astype(toe_ref.dtype)

    pr, pi = a_power(lr_c, li_c, tau_l + 1.0)
    qst_ref[0:P, :] = (ctr * pr - cti * pi).astype(qst_ref.dtype)
    qst_ref[P:2 * P, :] = (-(ctr * pi + cti * pr)).astype(qst_ref.dtype)

    tau_r = (L - 1 - lax.broadcasted_iota(jnp.int32, (LC, P), 0) // Cg).astype(F32)
    pr, pi = a_power(lr_r, li_r, tau_r)
    pst_ref[:, 0:P] = (bbr * pr - bbi * pi).astype(pst_ref.dtype)
    pst_ref[:, P:2 * P] = (bbr * pi + bbi * pr).astype(pst_ref.dtype)

    pr, pi = a_power(lr_r, li_r, float(L))
    a_pow_ref[0:1, :] = pr
    a_pow_ref[1:2, :] = pi


def _s5_weights(a_re, a_im, log_dt, b_re, b_im, c_re, c_im, d):
    G, P = a_re.shape
    Cg, L = S5_GROUP, S5_CHUNK
    LC = L * Cg
    lam = jnp.stack([a_re, a_im, jnp.broadcast_to(log_dt[:, None], (G, P)), jnp.zeros((G, P), F32)], axis=1)
    lam_row = lam
    lam_col = jnp.swapaxes(lam, 1, 2)
    bt = jnp.stack([jnp.swapaxes(b_re, 1, 2), jnp.swapaxes(b_im, 1, 2)], axis=1)
    bt = jnp.tile(bt, (1, 1, L, 1))
    ct = jnp.stack([jnp.swapaxes(c_re, 1, 2), jnp.swapaxes(c_im, 1, 2)], axis=1)
    ct = jnp.tile(ct, (1, 1, 1, L))
    d_col = d.reshape(G, Cg, 1)
    blk = lambda shp: pl.BlockSpec((None,) + shp, lambda g: (g,) + (0,) * len(shp))
    return pl.pallas_call(
        _s5_weights_kernel,
        grid=(G,),
        in_specs=[blk((P, 4)), blk((4, P)), blk((2, LC, P)), blk((2, P, LC)), blk((Cg, 1))],
        out_specs=(blk((LC, LC)), blk((LC, 2 * P)), blk((2 * P, LC)), blk((2, P))),
        out_shape=(jax.ShapeDtypeStruct((G, LC, LC), BF16),
                   jax.ShapeDtypeStruct((G, LC, 2 * P), BF16),
                   jax.ShapeDtypeStruct((G, 2 * P, LC), BF16),
                   jax.ShapeDtypeStruct((G, 2, P), F32)),
        compiler_params=pltpu.CompilerParams(dimension_semantics=("arbitrary",)),
    )(lam_col, lam_row, bt, ct, d_col)


def _granule_transpose(arrs):
    n = len(arrs)
    granule = lax.broadcasted_iota(jnp.int32, arrs[0].shape, 1) // S5_GROUP
    arrs = list(arrs)
    d = n // 2
    while d >= 1:
        upper = (granule & d) != 0
        for i in range(n):
            if i & d:
                continue
            lo, hi = arrs[i], arrs[i + d]
            arrs[i] = jnp.where(upper, pltpu.roll(hi, d * S5_GROUP, axis=1), lo)
            arrs[i + d] = jnp.where(upper, hi, pltpu.roll(lo, (n - d) * S5_GROUP, axis=1))
        d //= 2
    return arrs


def _s5_in_kernel(x_ref, w_ref, o_ref, u_scr):
    B, tt, D = x_ref.shape
    L, Cg = S5_CHUNK, S5_GROUP
    gpb = 128 // Cg
    cpb = tt // L
    u = _dot(x_ref[...].reshape(B * tt, D).astype(BF16), w_ref[...])
    for j in range(D // 128):
        u_scr[j] = u[:, 128 * j:128 * (j + 1)]
    for j in range(D // 128):
        for h in range(L // gpb):
            pieces = [jnp.concatenate([u_scr[j, pl.ds(c * L + h * gpb + q, B, stride=tt), :]
                                       for c in range(cpb)], axis=0) for q in range(gpb)]
            for gg, dest in enumerate(_granule_transpose(pieces)):
                o_ref[gpb * j + gg, :, 128 * h:128 * (h + 1)] = dest.astype(o_ref.dtype)


def _s5_in(x, w, tt):
    B, T, D = x.shape
    G = D // S5_GROUP
    LC = S5_CHUNK * S5_GROUP
    rows = (tt // S5_CHUNK) * B
    return pl.pallas_call(
        _s5_in_kernel,
        grid=(T // tt,),
        in_specs=[pl.BlockSpec((B, tt, D), lambda i: (0, i, 0)), pl.BlockSpec(w.shape, lambda i: (0, 0))],
        out_specs=pl.BlockSpec((G, rows, LC), lambda i: (0, i, 0)),
        out_shape=jax.ShapeDtypeStruct((G, (T // S5_CHUNK) * B, LC), BF16),
        scratch_shapes=[pltpu.VMEM((D // 128, B * tt, 128), F32)],
        compiler_params=pltpu.CompilerParams(dimension_semantics=("arbitrary",),
                                             vmem_limit_bytes=VMEM_LIMIT_BYTES),
    )(x, w)


def _s5_scan_kernel(x_ref, s0_ref, toe_ref, pst_ref, qst_ref, apow_ref, y_ref, sfin_ref,
                    zr_scr, zi_scr, sr_scr, si_scr, *, n_chunks, batch):
    P = apow_ref.shape[-1]
    for gi in range(x_ref.shape[0]):
        xg = x_ref[gi]
        z = _dot(xg, pst_ref[gi])
        zr_scr[...] = z[:, 0:P]
        zi_scr[...] = z[:, P:2 * P]
        ar = apow_ref[gi, 0:1, :]
        ai = apow_ref[gi, 1:2, :]

        def step(n, carry, ar=ar, ai=ai):
            sr, si = carry
            rows = pl.ds(pl.multiple_of(n * batch, batch), batch)
            sr_scr[rows, :] = sr
            si_scr[rows, :] = si
            return (ar * sr - ai * si + zr_scr[rows, :], ar * si + ai * sr + zi_scr[rows, :])

        sr, si = lax.fori_loop(0, n_chunks, step, (s0_ref[gi, 0], s0_ref[gi, 1]), unroll=4)
        sfin_ref[gi, 0] = sr
        sfin_ref[gi, 1] = si
        y = (_dot(xg, toe_ref[gi]) + _dot(sr_scr[...].astype(BF16), qst_ref[gi, 0:P, :])
             + _dot(si_scr[...].astype(BF16), qst_ref[gi, P:2 * P, :]))
        y_ref[gi] = y.astype(y_ref.dtype)


def _s5_scan(xg, s0, toe, pst, qst, apow, n_chunks, batch):
    G, R, LC = xg.shape
    P = apow.shape[-1]
    gb = max(1, min(8, S5_SCAN_ROWS_PER_STEP // R))
    blk = lambda shp: pl.BlockSpec((gb,) + shp, lambda g: (g,) + (0,) * len(shp))
    return pl.pallas_call(
        functools.partial(_s5_scan_kernel, n_chunks=n_chunks, batch=batch),
        grid=(G // gb,),
        in_specs=[blk((R, LC)), blk((2, batch, P)), blk((LC, LC)), blk((LC, 2 * P)), blk((2 * P, LC)),
                  blk((2, P))],
        out_specs=(blk((R, LC)), blk((2, batch, P))),
        out_shape=(jax.ShapeDtypeStruct((G, R, LC), BF16), jax.ShapeDtypeStruct((G, 2, batch, P), F32)),
        scratch_shapes=[pltpu.VMEM((R, P), F32)] * 4,
        compiler_params=pltpu.CompilerParams(dimension_semantics=("arbitrary",),
                                             vmem_limit_bytes=VMEM_LIMIT_BYTES),
    )(xg, s0, toe, pst, qst, apow)


def _s5_out_kernel(x_ref, yg_ref, w_ref, ln_g_ref, ln_b_ref, o_ref, y_scr):
    B, tt, D = x_ref.shape
    L, Cg = S5_CHUNK, S5_GROUP
    gpb = 128 // Cg
    cpb = tt // L
    for j in range(D // 128):
        for h in range(L // gpb):
            srcs = [yg_ref[gpb * j + gg, :, 128 * h:128 * (h + 1)].astype(F32) for gg in range(gpb)]
            for q, dest in enumerate(_granule_transpose(srcs)):
                for c in range(cpb):
                    y_scr[j, pl.ds(c * L + h * gpb + q, B, stride=tt), :] = dest[c * B:(c + 1) * B, :]
    y = jnp.concatenate([y_scr[j] for j in range(D // 128)], axis=1)
    act = _gelu_tanh(y).astype(BF16)
    zz = _dot(act, w_ref[...])
    mix = zz[:, :D] * _sigmoid(zz[:, D:])
    out = _layer_norm(ALPHA * x_ref[...].reshape(B * tt, D) + mix, ln_g_ref[...], ln_b_ref[...])
    o_ref[...] = out.reshape(B, tt, D)


def _s5_out(x, yg, w_out, ln_g, ln_b, tt):
    B, T, D = x.shape
    G, _, LC = yg.shape
    rows = (tt // S5_CHUNK) * B
    full = lambda a: pl.BlockSpec(a.shape, lambda i: (0,) * a.ndim)
    return pl.pallas_call(
        _s5_out_kernel,
        grid=(T // tt,),
        in_specs=[pl.BlockSpec((B, tt, D), lambda i: (0, i, 0)), pl.BlockSpec((G, rows, LC), lambda i: (0, i, 0)),
                  full(w_out), full(ln_g), full(ln_b)],
        out_specs=pl.BlockSpec((B, tt, D), lambda i: (0, i, 0)),
        out_shape=jax.ShapeDtypeStruct((B, T, D), F32),
        scratch_shapes=[pltpu.VMEM((D // 128, B * tt, 128), F32)],
        compiler_params=pltpu.CompilerParams(dimension_semantics=("arbitrary",),
                                             vmem_limit_bytes=VMEM_LIMIT_BYTES),
    )(x, yg, w_out, ln_g, ln_b)


def _s5_mixer(x, s0_re, s0_im, w_in, w_out, s5w, ln_g, ln_b):
    B, T, D = x.shape
    toe, pst, qst, apow = s5w
    xg = _s5_in(x, w_in, S5_STEP_TOKENS)
    s0 = jnp.stack([s0_re, s0_im], axis=0).transpose(2, 0, 1, 3)
    yg, sfin = _s5_scan(xg, s0, toe, pst, qst, apow, T // S5_CHUNK, B)
    out = _s5_out(x, yg, w_out, ln_g, ln_b, S5_STEP_TOKENS)
    sfin = sfin.transpose(1, 2, 0, 3)
    return out, sfin[0], sfin[1]


def _block_diag(w):
    H, d, _ = w.shape
    eye = jnp.eye(H, dtype=w.dtype)
    return (eye[:, None, :, None] * w[:, :, None, :]).reshape(H * d, H * d)


def kernel(x_prompt, x_sample, state_lru_h, state_lru_conv, state_hgrn2, state_s5_re, state_s5_im, ln_mix_g, ln_mix_b, ln_ffn_g, ln_ffn_b, w_in_ab, w_out_ab, lru_conv_w, lru_conv_b, lru_w_r, lru_b_r, lru_w_i, lru_b_i, lru_lam, hgrn_lb_logits, hgrn_norm_g, w_in_s5, w_out_s5, s5_a_re, s5_a_im, s5_b_re, s5_b_im, s5_c_re, s5_c_im, s5_d, s5_log_dt, router_w, router_b, moe_w_gate, moe_w_up, moe_w_down):
    Bp, Tp, D = x_prompt.shape
    Bs, Ts, _ = x_sample.shape
    W = state_lru_h.shape[-1]
    dk = W // HG_HEADS
    row = lambda v: v.reshape(1, -1)

    def moe(y, l, sparse):
        fn = _moe_sparse if sparse else functools.partial(_moe_dense, tm=512)
        out = fn(y.reshape(-1, D), router_w, row(router_b), moe_w_gate, moe_w_up, moe_w_down, l,
                 row(ln_ffn_g[l]), row(ln_ffn_b[l]))
        return out.reshape(y.shape)

    j = 0
    wts = (w_in_ab[j].astype(BF16), w_out_ab[j].astype(BF16), lru_conv_w[j], row(lru_conv_b[j]),
           _block_diag(lru_w_r[j]).astype(BF16), row(lru_b_r[j]),
           _block_diag(lru_w_i[j]).astype(BF16), row(lru_b_i[j]),
           row(lru_lam[j]), hgrn_lb_logits, row(hgrn_norm_g[j]), row(ln_mix_g[0]), row(ln_mix_b[0]))
    yp, p_h, p_conv, p_hg = _ab_mixer(
        x_prompt, jnp.zeros((Bp, 1, W), F32), jnp.zeros((Bp, 3, W), F32),
        jnp.zeros((Bp, HG_HEADS, dk, dk), F32), wts, AB_TILE, AB_SEQS)
    ys, s_h, s_conv, s_hg = _ab_mixer(
        x_sample, state_lru_h[j][:, None, :], state_lru_conv[j], state_hgrn2[j], wts, Ts, 2)
    yp = moe(yp, 0, True)
    ys = moe(ys, 0, False)

    s5w = _s5_weights(s5_a_re[j], s5_a_im[j], s5_log_dt[j], s5_b_re[j], s5_b_im[j],
                      s5_c_re[j], s5_c_im[j], s5_d[j])
    G, P = s5_a_re[j].shape
    w_in5 = w_in_s5[j].astype(BF16)
    w_out5 = w_out_s5[j].astype(BF16)
    zp = jnp.zeros((Bp, G, P), F32)
    yp, p_re, p_im = _s5_mixer(yp, zp, zp, w_in5, w_out5, s5w, row(ln_mix_g[1]), row(ln_mix_b[1]))
    ys, s_re, s_im = _s5_mixer(ys, state_s5_re[j], state_s5_im[j], w_in5, w_out5, s5w,
                               row(ln_mix_g[1]), row(ln_mix_b[1]))
    yp = moe(yp, 1, True)
    ys = moe(ys, 1, False)

    return (yp, ys,
            p_h.reshape(1, Bp, W), p_conv[None], p_hg[None], p_re[None], p_im[None],
            s_h.reshape(1, Bs, W), s_conv[None], s_hg[None], s_re[None], s_im[None])
```

```python
import functools
import math

import jax
import jax.numpy as jnp
from jax import lax
from jax.experimental import pallas as pl
from jax.experimental.pallas import tpu as pltpu

F32 = jnp.float32
BF16 = jnp.bfloat16
HIGHEST = lax.Precision.HIGHEST

DEPTH = 2
RG_C = 8.0
LRU_HEADS = 8
HG_HEADS = 4
AB_TILE = 512
AB_SEQS = 1
HG_CHUNK = 16
RMS_EPS = 1e-6
S5_GROUP = 16
S5_SCAN_ROWS_PER_STEP = 2048
S5_STEP_TOKENS = 32
S5_CHUNK = 16
N_EXPERT_GROUPS = 4
ALPHA = (2.0 * DEPTH) ** 0.25
LN_EPS = 1e-5
MOE_TILE = 512
MOE_CHUNK = 16
MOE_SLOTS = 1280
MOE_BLOCK = 1024
COMBINE_DEPTH = 3
ROUTE_TILES_PER_STEP = 2
VMEM_LIMIT_BYTES = 56 * 1024 * 1024


def _dot(a, b):
    return jnp.dot(a, b, preferred_element_type=F32)


def _dot_nt(a, b):
    return lax.dot_general(a, b, (((1,), (1,)), ((), ())), preferred_element_type=F32)


def _sigmoid(x):
    return 0.5 * jnp.tanh(0.5 * x) + 0.5


def _gelu_tanh(x):
    c = math.sqrt(2.0 / math.pi)
    return 0.5 * x * (1.0 + jnp.tanh(c * (x + 0.044715 * (x * x * x))))


def _layer_norm(s, g, b):
    mu = jnp.mean(s, axis=-1, keepdims=True)
    d = s - mu
    var = jnp.mean(d * d, axis=-1, keepdims=True)
    return d * lax.rsqrt(var + LN_EPS) * g + b


def _shift_rows(x, s, row):
    del row
    return pltpu.roll(x, s, axis=0)


def _ab_kernel(x_ref, h0_ref, conv0_ref, s0_ref, w_in_ref, w_out_ref, conv_w_ref, conv_b_ref,
               wr_ref, br_ref, wi_ref, bi_ref, lam_ref, lb_logits_ref, norm_g_ref, ln_g_ref, ln_b_ref,
               y_ref, h_out_ref, conv_out_ref, s_out_ref,
               zx_buf, h_carry, st_carry, st_all, *, tb, width, nb):
    for si in range(nb):
        _ab_sequence_tile(si, x_ref, h0_ref, conv0_ref, s0_ref, w_in_ref, w_out_ref, conv_w_ref, conv_b_ref,
                          wr_ref, br_ref, wi_ref, bi_ref, lam_ref, lb_logits_ref, norm_g_ref, ln_g_ref, ln_b_ref,
                          y_ref, h_out_ref, conv_out_ref, s_out_ref,
                          zx_buf.at[si], h_carry.at[si], st_carry.at[si], st_all.at[si], tb, width)


def _ab_sequence_tile(si, x_ref, h0_ref, conv0_ref, s0_ref, w_in_ref, w_out_ref, conv_w_ref, conv_b_ref,
                      wr_ref, br_ref, wi_ref, bi_ref, lam_ref, lb_logits_ref, norm_g_ref, ln_g_ref, ln_b_ref,
                      y_ref, h_out_ref, conv_out_ref, s_out_ref, zx_buf, h_carry, st_carry, st_all, tb, width):
    t = pl.program_id(1)
    nt = pl.num_programs(1)
    W = width
    L = HG_CHUNK
    nc = tb // L
    dk = W // HG_HEADS

    @pl.when(t == 0)
    def _init():
        zx_buf[0:8, :] = jnp.zeros((8, W), F32)
        zx_buf[5:8, :] = conv0_ref[si]
        h_carry[...] = jnp.broadcast_to(h0_ref[si], h_carry.shape)
        for hd in range(HG_HEADS):
            st_carry[hd] = s0_ref[si, hd].T

    x = x_ref[si]
    z = _dot(x.astype(BF16), w_in_ref[...])
    z_gate = z[:, 0 * W:1 * W]
    z_x = z[:, 1 * W:2 * W]
    z_q = z[:, 2 * W:3 * W]
    z_f = z[:, 3 * W:4 * W]
    z_i = z[:, 4 * W:5 * W]
    z_g = z[:, 5 * W:6 * W]

    zx_buf[8:8 + tb, :] = z_x
    cw = conv_w_ref[...]
    u = (conv_b_ref[...] + cw[3:4, :] * z_x
         + cw[2:3, :] * zx_buf[7:7 + tb, :]
         + cw[1:2, :] * zx_buf[6:6 + tb, :]
         + cw[0:1, :] * zx_buf[5:5 + tb, :])
    tail = zx_buf[tb:tb + 8, :]
    zx_buf[0:8, :] = tail

    ub = u.astype(BF16)
    r = _sigmoid(_dot(ub, wr_ref[...]) + br_ref[...])
    ig = _sigmoid(_dot(ub, wi_ref[...]) + bi_ref[...])
    nlam = -lam_ref[...]
    softplus_nlam = jnp.maximum(nlam, 0.0) + jnp.log1p(jnp.exp(-jnp.abs(nlam)))
    log_a = (-RG_C) * r * softplus_nlam
    a = jnp.exp(log_a)
    th = jnp.tanh(log_a)
    b = jnp.sqrt(-2.0 * th / (1.0 - th)) * ig * u

    row = lax.broadcasted_iota(jnp.int32, (tb, W), 0)
    rin8 = row % 8
    s = 1
    while s < 8:
        m = rin8 >= s
        a_sh = _shift_rows(a, s, row)
        b_sh = _shift_rows(b, s, row)
        b = jnp.where(m, a * b_sh + b, b)
        a = jnp.where(m, a * a_sh, a)
        s *= 2
    a3 = a.reshape(tb // 8, 8, W)
    b3 = b.reshape(tb // 8, 8, W)
    h_prev = h_carry[0:1, :]
    h_groups = []
    for g in range(tb // 8):
        hg = b3[g] + a3[g] * h_prev
        h_groups.append(hg)
        h_prev = hg[7:8, :]
    h = jnp.concatenate(h_groups, axis=0)
    h_carry[...] = jnp.broadcast_to(h_prev, h_carry.shape)
    out_a = h * _gelu_tanh(z_gate)

    lg = lb_logits_ref[...]
    e = jnp.exp(lg - jnp.max(lg, axis=0, keepdims=True))
    lb = e[0:1, :] / jnp.sum(e, axis=0, keepdims=True)
    f = lb + (1.0 - lb) * _sigmoid(z_f)
    log_f = jnp.log(f)
    kk = 1.0 - f
    rin = row % L
    G = log_f
    s = 1
    while s < L:
        G = G + jnp.where(rin >= s, _shift_rows(G, s, row), 0.0)
        s *= 2
    G3 = G.reshape(nc, L, W)
    G_last = G3[:, L - 1:L, :]
    q_dec = (z_q * jnp.exp(G)).reshape(nc, L, W).astype(BF16)
    k_inc = (kk * jnp.exp(-G)).reshape(nc, L, W).astype(BF16)
    k_tail = (kk.reshape(nc, L, W) * jnp.exp(G_last - G3)).astype(BF16)
    decay = jnp.exp(G_last)
    v3 = z_i.reshape(nc, L, W).astype(BF16)
    causal = (lax.broadcasted_iota(jnp.int32, (nc, L, L), 1)
              >= lax.broadcasted_iota(jnp.int32, (nc, L, L), 2))

    o_heads = []
    for hd in range(HG_HEADS):
        sl = slice(hd * dk, (hd + 1) * dk)
        qd, ki, kt, vv = q_dec[:, :, sl], k_inc[:, :, sl], k_tail[:, :, sl], v3[:, :, sl]
        scores = jnp.einsum('ntk,nsk->nts', qd, ki, preferred_element_type=F32)
        scores = jnp.where(causal, scores, 0.0).astype(BF16)
        o_intra = jnp.einsum('nts,nsv->ntv', scores, vv, preferred_element_type=F32)
        vt = jnp.swapaxes(vv.astype(F32), 1, 2).astype(BF16)
        upd = jnp.einsum('nvs,nsk->nvk', vt, kt, preferred_element_type=F32)
        st = st_carry[hd]
        for n in range(nc):
            st_all[hd, n] = st.astype(BF16)
            st = st * decay[n, :, sl] + upd[n]
        st_carry[hd] = st
        o_inter = jnp.einsum('ntk,nvk->ntv', qd, st_all[hd], preferred_element_type=F32)
        o = (o_intra + o_inter).reshape(tb, dk)
        o = o * lax.rsqrt(jnp.mean(o * o, axis=-1, keepdims=True) + RMS_EPS)
        o_heads.append(o)
    o_all = jnp.concatenate(o_heads, axis=-1) * norm_g_ref[...]
    out_b = o_all * (z_g * _sigmoid(z_g))

    mixed = jnp.concatenate([out_a, out_b], axis=-1).astype(BF16)
    y = _dot(mixed, w_out_ref[...])
    y_ref[si] = _layer_norm(ALPHA * x + y, ln_g_ref[...], ln_b_ref[...])

    @pl.when(t == nt - 1)
    def _fin():
        h_out_ref[si] = h[tb - 1:tb, :]
        conv_out_ref[si] = tail[5:8, :]
        for hd in range(HG_HEADS):
            s_out_ref[si, hd] = st_carry[hd].T


def _ab_mixer(x, h0, conv0, s0, wts, tb, nb):
    B, T, D = x.shape
    W = h0.shape[-1]
    dk = W // HG_HEADS
    nt = T // tb
    nc = tb // HG_CHUNK
    (w_in, w_out, conv_w, conv_b, wr, br, wi, bi, lam, lb_logits, norm_g, ln_g, ln_b) = wts
    full = lambda a: pl.BlockSpec(a.shape, lambda b, t: (0,) * a.ndim)
    per_b = lambda shp: pl.BlockSpec((nb,) + shp, lambda b, t: (b,) + (0,) * len(shp))
    out_shape = (jax.ShapeDtypeStruct((B, T, D), F32),
                 jax.ShapeDtypeStruct((B, 1, W), F32),
                 jax.ShapeDtypeStruct((B, 3, W), F32),
                 jax.ShapeDtypeStruct((B, HG_HEADS, dk, dk), F32))
    return pl.pallas_call(
        functools.partial(_ab_kernel, tb=tb, width=W, nb=nb),
        grid=(B // nb, nt),
        in_specs=[pl.BlockSpec((nb, tb, D), lambda b, t: (b, t, 0)),
                  per_b((1, W)), per_b((3, W)), per_b((HG_HEADS, dk, dk))]
                 + [full(a) for a in wts],
        out_specs=(pl.BlockSpec((nb, tb, D), lambda b, t: (b, t, 0)),
                   per_b((1, W)), per_b((3, W)), per_b((HG_HEADS, dk, dk))),
        out_shape=out_shape,
        scratch_shapes=[pltpu.VMEM((nb, tb + 8, W), F32),
                        pltpu.VMEM((nb, 8, W), F32),
                        pltpu.VMEM((nb, HG_HEADS, dk, dk), F32),
                        pltpu.VMEM((nb, HG_HEADS, nc, dk, dk), BF16)],
        compiler_params=pltpu.CompilerParams(
            dimension_semantics=("arbitrary", "arbitrary"),
            vmem_limit_bytes=VMEM_LIMIT_BYTES),
    )(x, h0, conv0, s0, *wts)


def _router(x, rw, rb, n_experts):
    per_group = n_experts // N_EXPERT_GROUPS
    logits = jnp.dot(x, rw, preferred_element_type=F32, precision=HIGHEST) + rb
    logits = logits - jnp.max(logits, axis=-1, keepdims=True)
    ex = jnp.exp(logits)
    p = ex / jnp.sum(ex, axis=-1, keepdims=True)
    lane_i = lax.broadcasted_iota(jnp.int32, p.shape, 1)
    grp = lane_i // per_group
    lane = lane_i.astype(F32)
    big = float(n_experts)

    def top2(pm):
        m1 = jnp.max(pm, axis=-1, keepdims=True)
        i1 = jnp.min(jnp.where(pm == m1, lane, big), axis=-1, keepdims=True)
        pm2 = jnp.where(lane == i1, -1.0, pm)
        m2 = jnp.max(pm2, axis=-1, keepdims=True)
        i2 = jnp.min(jnp.where(pm2 == m2, lane, big), axis=-1, keepdims=True)
        return m1, i1, m2, i2

    best = None
    g_sel = None
    for j in range(N_EXPERT_GROUPS):
        m1, _, m2, _ = top2(jnp.where(grp == j, p, -1.0))
        score = m1 + m2
        if best is None:
            best, g_sel = score, jnp.zeros_like(lane_i[:, :1])
        else:
            better = score > best
            best = jnp.where(better, score, best)
            g_sel = jnp.where(better, j, g_sel)
    m1, i1, m2, i2 = top2(jnp.where(grp == g_sel, p, -1.0))
    den = m1 + m2
    return jnp.where(lane == i1, m1 / den, 0.0) + jnp.where(lane == i2, m2 / den, 0.0)


def _moe_dense_kernel(x_ref, rw_ref, rb_ref, wg_ref, wu_ref, wd_ref, ln_g_ref, ln_b_ref,
                      y_ref, xb_scr, comb_scr, acc_scr, *, n_experts):
    e = pl.program_id(1)

    @pl.when(e == 0)
    def _init():
        x = x_ref[...]
        xb_scr[...] = x.astype(BF16)
        comb_scr[...] = _router(x, rw_ref[...], rb_ref[...], n_experts)
        acc_scr[...] = jnp.zeros_like(acc_scr)

    comb = comb_scr[...]
    lane = lax.broadcasted_iota(jnp.int32, comb.shape, 1)
    c_e = jnp.sum(jnp.where(lane == e, comb, 0.0), axis=-1, keepdims=True)
    xb = xb_scr[...]
    g = _dot(xb, wg_ref[...].astype(BF16))
    up = _dot(xb, wu_ref[...].astype(BF16))
    hmid = (g * _sigmoid(g) * up).astype(BF16)
    acc_scr[...] += c_e * _dot(hmid, wd_ref[...].astype(BF16))

    @pl.when(e == n_experts - 1)
    def _fin():
        y_ref[...] = _layer_norm(ALPHA * x_ref[...] + acc_scr[...], ln_g_ref[...], ln_b_ref[...])


def _moe_dense(x2d, rw, rb, wg, wu, wd, layer, ln_g, ln_b, tm):
    N, D = x2d.shape
    _, E, _, De = wg.shape
    full = lambda a: pl.BlockSpec(a.shape, lambda i, e: (0,) * a.ndim)
    return pl.pallas_call(
        functools.partial(_moe_dense_kernel, n_experts=E),
        grid=(N // tm, E),
        in_specs=[pl.BlockSpec((tm, D), lambda i, e: (i, 0)), full(rw), full(rb),
                  pl.BlockSpec((None, None, D, De), lambda i, e: (layer, e, 0, 0)),
                  pl.BlockSpec((None, None, D, De), lambda i, e: (layer, e, 0, 0)),
                  pl.BlockSpec((None, None, De, D), lambda i, e: (layer, e, 0, 0)),
                  full(ln_g), full(ln_b)],
        out_specs=pl.BlockSpec((tm, D), lambda i, e: (i, 0)),
        out_shape=jax.ShapeDtypeStruct((N, D), F32),
        scratch_shapes=[pltpu.VMEM((tm, D), BF16), pltpu.VMEM((tm, E), F32), pltpu.VMEM((tm, D), F32)],
        compiler_params=pltpu.CompilerParams(
            dimension_semantics=("arbitrary", "arbitrary"),
            vmem_limit_bytes=VMEM_LIMIT_BYTES),
    )(x2d, rw, rb, wg, wu, wd, ln_g, ln_b)


def _route_kernel(x_ref, rwt_ref, rbc_ref, meta_ref, pc_ref, *, n_experts):
    E = n_experts
    per_group = E // N_EXPERT_GROUPS
    x = x_ref[...]
    tm = x.shape[0]
    logits = lax.dot_general(rwt_ref[...], x, (((1,), (1,)), ((), ())),
                             precision=HIGHEST, preferred_element_type=F32) + rbc_ref[...]
    logits = logits - jnp.max(logits, axis=0, keepdims=True)
    ex = jnp.exp(logits)
    p = ex / jnp.sum(ex, axis=0, keepdims=True)
    eid_i = lax.broadcasted_iota(jnp.int32, (E, tm), 0)
    grp = eid_i // per_group
    eid = eid_i.astype(F32)
    big = float(E)

    def top2(pm):
        m1 = jnp.max(pm, axis=0, keepdims=True)
        i1 = jnp.min(jnp.where(pm == m1, eid, big), axis=0, keepdims=True)
        pm2 = jnp.where(eid == i1, -1.0, pm)
        m2 = jnp.max(pm2, axis=0, keepdims=True)
        i2 = jnp.min(jnp.where(pm2 == m2, eid, big), axis=0, keepdims=True)
        return m1, i1, m2, i2

    best = None
    g_sel = None
    for j in range(N_EXPERT_GROUPS):
        m1, _, m2, _ = top2(jnp.where(grp == j, p, -1.0))
        score = m1 + m2
        if best is None:
            best, g_sel = score, jnp.zeros_like(eid_i[:1, :])
        else:
            better = score > best
            best = jnp.where(better, score, best)
            g_sel = jnp.where(better, j, g_sel)
    m1, i1, m2, i2 = top2(jnp.where(grp == g_sel, p, -1.0))
    den = m1 + m2

    oh1_all = jnp.where(eid == i1, 1.0, 0.0)
    oh2_all = jnp.where(eid == i2, 1.0, 0.0)
    g1_all = m1 / den
    g2_all = m2 / den
    tl = MOE_TILE
    earlier = jnp.where(lax.broadcasted_iota(jnp.int32, (tl, tl), 0)
                        < lax.broadcasted_iota(jnp.int32, (tl, tl), 1), 1.0, 0.0).astype(BF16)
    below = jnp.where(lax.broadcasted_iota(jnp.int32, (E, E), 1)
                      < lax.broadcasted_iota(jnp.int32, (E, E), 0), 1.0, 0.0)
    for k in range(tm // tl):
        lanes = slice(k * tl, (k + 1) * tl)
        oh1, oh2 = oh1_all[:, lanes], oh2_all[:, lanes]
        r1 = _dot(oh1.astype(BF16), earlier)
        r2 = _dot(oh2.astype(BF16), earlier)
        cnt1 = jnp.sum(oh1, axis=1, keepdims=True)
        cnt2 = jnp.sum(oh2, axis=1, keepdims=True)
        pc = jnp.floor((cnt1 + cnt2 + (MOE_CHUNK - 1.0)) * (1.0 / MOE_CHUNK)) * MOE_CHUNK
        pc_b = jnp.broadcast_to(pc, (E, 128))
        run_off = jnp.dot(below, pc_b, precision=HIGHEST, preferred_element_type=F32)[:, 0:1]
        meta_ref[k, 0:1, :] = jnp.sum(oh1 * (run_off + r1), axis=0, keepdims=True)
        meta_ref[k, 1:2, :] = jnp.sum(oh2 * (run_off + cnt1 + r2), axis=0, keepdims=True)
        meta_ref[k, 2:3, :] = g1_all[:, lanes]
        meta_ref[k, 3:4, :] = g2_all[:, lanes]
        meta_ref[k, 4:8, :] = jnp.zeros((4, tl), F32)
        pc_ref[k] = pc_b


def _route(x2d, rwt, rbc):
    N, D = x2d.shape
    E = rwt.shape[0]
    nt = N // MOE_TILE
    tps = ROUTE_TILES_PER_STEP if nt % ROUTE_TILES_PER_STEP == 0 else 1
    full = lambda a: pl.BlockSpec(a.shape, lambda i: (0,) * a.ndim)
    return pl.pallas_call(
        functools.partial(_route_kernel, n_experts=E),
        grid=(nt // tps,),
        in_specs=[pl.BlockSpec((tps * MOE_TILE, D), lambda i: (i, 0)), full(rwt), full(rbc)],
        out_specs=(pl.BlockSpec((tps, 8, MOE_TILE), lambda i: (i, 0, 0)),
                   pl.BlockSpec((tps, E, 128), lambda i: (i, 0, 0))),
        out_shape=(jax.ShapeDtypeStruct((nt, 8, MOE_TILE), F32), jax.ShapeDtypeStruct((nt, E, 128), F32)),
        compiler_params=pltpu.CompilerParams(dimension_semantics=("arbitrary",),
                                             vmem_limit_bytes=VMEM_LIMIT_BYTES),
    )(x2d, rwt, rbc)


def _moe_schedule(pc, n_blocks_max):
    nt, E = pc.shape
    cum = jnp.cumsum(pc, axis=1)
    run_off = cum - pc
    total = cum[:, -1]
    seg_len = jnp.sum(pc, axis=0)
    seg_pad = ((seg_len + MOE_BLOCK - 1) // MOE_BLOCK) * MOE_BLOCK
    seg_end = jnp.cumsum(seg_pad)
    goff = (seg_end - seg_pad)[None, :] + (jnp.cumsum(pc, axis=0) - pc)
    r = jnp.arange(MOE_SLOTS // MOE_CHUNK, dtype=jnp.int32) * MOE_CHUNK
    e_c = jnp.minimum(jnp.sum((cum[:, None, :] <= r[None, :, None]).astype(jnp.int32), axis=-1), E - 1)
    pick = e_c[:, :, None] == jnp.arange(E, dtype=jnp.int32)[None, None, :]
    dst_row = r[None, :] + jnp.sum(jnp.where(pick, (goff - run_off)[:, None, :], 0), axis=-1)
    dst_chunk = jnp.where(r[None, :] < total[:, None], dst_row // MOE_CHUNK, -1).astype(jnp.int32)
    blk_start = jnp.arange(n_blocks_max, dtype=jnp.int32) * MOE_BLOCK
    block_expert = jnp.minimum(jnp.sum((seg_end[None, :] <= blk_start[:, None]).astype(jnp.int32), axis=-1), E - 1)
    n_valid = (seg_end[-1] // MOE_BLOCK).reshape(1).astype(jnp.int32)
    n_chunks_all = n_blocks_max * (MOE_BLOCK // MOE_CHUNK)
    n_fill_max = n_chunks_all - (nt * 2 * MOE_TILE) // MOE_CHUNK
    gap_start = jnp.concatenate([seg_end - seg_pad + seg_len, seg_end[-1:]]) // MOE_CHUNK
    gap_len = jnp.concatenate([seg_pad - seg_len, n_chunks_all * MOE_CHUNK - seg_end[-1:]]) // MOE_CHUNK
    gap_cum = jnp.cumsum(gap_len)
    k = jnp.arange(n_fill_max, dtype=jnp.int32)
    which = jnp.minimum(jnp.sum((gap_cum[None, :] <= k[:, None]).astype(jnp.int32), axis=-1), E)
    pick_gap = which[:, None] == jnp.arange(E + 1, dtype=jnp.int32)[None, :]
    fill = k + jnp.sum(jnp.where(pick_gap, (gap_start - (gap_cum - gap_len))[None, :], 0), axis=-1)
    fill = jnp.where(k < gap_cum[-1], fill, 0).astype(jnp.int32)
    n_fill = gap_cum[-1].reshape(1).astype(jnp.int32)
    return (dst_chunk.reshape(-1), total.astype(jnp.int32), block_expert.astype(jnp.int32), n_valid, fill, n_fill)


def _chunk_copy(hbm_ref, buf, sem, b, c, d, to_hbm):
    local = buf.at[b, pl.ds(pl.multiple_of(c * MOE_CHUNK, MOE_CHUNK), MOE_CHUNK)]
    remote = hbm_ref.at[pl.ds(pl.multiple_of(d * MOE_CHUNK, MOE_CHUNK), MOE_CHUNK)]
    if to_hbm:
        return pltpu.make_async_copy(local, remote, sem.at[b])
    return pltpu.make_async_copy(remote, local, sem.at[b])


def _for_tile_chunks(dst_ref, tot_ref, tile, fn):
    slots = MOE_SLOTS // MOE_CHUNK

    def body(c, carry):
        fn(c, dst_ref[tile * slots + c])
        return carry

    lax.fori_loop(0, tot_ref[tile] // MOE_CHUNK, body, 0)


def _dispatch_kernel(dst_ref, tot_ref, fill_ref, nfill_ref, x_ref, meta_ref, xs_hbm, buf, sem, zbuf, zsem):
    i = pl.program_id(0)
    n = pl.num_programs(0)
    b = i % 2
    tm = x_ref.shape[0]

    def wait_tile(tile, slot):
        _for_tile_chunks(dst_ref, tot_ref, tile,
                         lambda c, d: _chunk_copy(xs_hbm, buf, sem, slot, c, d, True).wait())

    def zero_copy(k):
        d = fill_ref[k]
        return pltpu.make_async_copy(
            zbuf, xs_hbm.at[pl.ds(pl.multiple_of(d * MOE_CHUNK, MOE_CHUNK), MOE_CHUNK)], zsem.at[0])

    @pl.when(i == 0)
    def _():
        zbuf[...] = jnp.zeros_like(zbuf)
        lax.fori_loop(0, nfill_ref[0], lambda k, c: (zero_copy(k).start(), c)[1], 0)

    @pl.when(i >= 2)
    def _():
        wait_tile(i - 2, b)

    s1 = meta_ref[0:1, :].astype(jnp.int32)
    s2 = meta_ref[1:2, :].astype(jnp.int32)
    rows = lax.broadcasted_iota(jnp.int32, (MOE_SLOTS, tm), 0)
    dsp = jnp.where((rows == s1) | (rows == s2), 1.0, 0.0).astype(BF16)
    buf[b] = _dot(dsp, x_ref[...].astype(BF16)).astype(BF16)
    _for_tile_chunks(dst_ref, tot_ref, i, lambda c, d: _chunk_copy(xs_hbm, buf, sem, b, c, d, True).start())

    @pl.when(i == n - 1)
    def _():
        wait_tile(i, b)

    @pl.when((i == n - 1) & (i >= 1))
    def _():
        wait_tile(i - 1, 1 - b)

    @pl.when(i == n - 1)
    def _():
        lax.fori_loop(0, nfill_ref[0], lambda k, c: (zero_copy(k).wait(), c)[1], 0)


def _dispatch(x2d, meta, dst_chunk, total, fill, n_fill, n_rows):
    N, D = x2d.shape
    nt = N // MOE_TILE
    return pl.pallas_call(
        _dispatch_kernel,
        grid_spec=pltpu.PrefetchScalarGridSpec(
            num_scalar_prefetch=4, grid=(nt,),
            in_specs=[pl.BlockSpec((MOE_TILE, D), lambda i, *_: (i, 0)),
                      pl.BlockSpec((None, 8, MOE_TILE), lambda i, *_: (i, 0, 0))],
            out_specs=pl.BlockSpec(memory_space=pl.ANY),
            scratch_shapes=[pltpu.VMEM((2, MOE_SLOTS, D), BF16), pltpu.SemaphoreType.DMA((2,)),
                            pltpu.VMEM((MOE_CHUNK, D), BF16), pltpu.SemaphoreType.DMA((1,))]),
        out_shape=jax.ShapeDtypeStruct((n_rows, D), BF16),
        compiler_params=pltpu.CompilerParams(dimension_semantics=("arbitrary",),
                                             vmem_limit_bytes=VMEM_LIMIT_BYTES),
    )(dst_chunk, total, fill, n_fill, x2d, meta)


def _ffn_kernel(be_ref, nv_ref, xs_ref, wg_ref, wu_ref, wd_ref, ys_ref, wg_b, wu_b, wd_b):
    j = pl.program_id(0)
    used = j < nv_ref[0]
    new_expert = (j == 0) | (be_ref[j] != be_ref[jnp.maximum(j - 1, 0)])

    @pl.when(used & new_expert)
    def _():
        wg_b[...] = wg_ref[...].astype(BF16)
        wu_b[...] = wu_ref[...].astype(BF16)
        wd_b[...] = wd_ref[...].astype(BF16)

    @pl.when(used)
    def _():
        xb = xs_ref[...]
        g = _dot(xb, wg_b[...])
        up = _dot(xb, wu_b[...])
        hmid = (g * _sigmoid(g) * up).astype(BF16)
        ys_ref[...] = _dot(hmid, wd_b[...]).astype(ys_ref.dtype)

    @pl.when(jnp.logical_not(used))
    def _():
        ys_ref[...] = jnp.zeros_like(ys_ref)


def _ffn(xs, wg, wu, wd, layer, block_expert, n_valid):
    R, D = xs.shape
    _, E, _, De = wg.shape
    last = lambda j, nv: jnp.maximum(jnp.minimum(j, nv[0] - 1), 0)
    return pl.pallas_call(
        _ffn_kernel,
        grid_spec=pltpu.PrefetchScalarGridSpec(
            num_scalar_prefetch=2, grid=(R // MOE_BLOCK,),
            in_specs=[pl.BlockSpec((MOE_BLOCK, D), lambda j, be, nv: (last(j, nv), 0)),
                      pl.BlockSpec((None, None, D, De), lambda j, be, nv: (layer, be[last(j, nv)], 0, 0)),
                      pl.BlockSpec((None, None, D, De), lambda j, be, nv: (layer, be[last(j, nv)], 0, 0)),
                      pl.BlockSpec((None, None, De, D), lambda j, be, nv: (layer, be[last(j, nv)], 0, 0))],
            out_specs=pl.BlockSpec((MOE_BLOCK, D), lambda j, be, nv: (j, 0)),
            scratch_shapes=[pltpu.VMEM((D, De), BF16), pltpu.VMEM((D, De), BF16), pltpu.VMEM((De, D), BF16)]),
        out_shape=jax.ShapeDtypeStruct((R, D), BF16),
        compiler_params=pltpu.CompilerParams(dimension_semantics=("arbitrary",),
                                             vmem_limit_bytes=VMEM_LIMIT_BYTES),
    )(block_expert, n_valid, xs, wg, wu, wd)


def _combine_kernel(dst_ref, tot_ref, x_ref, metac_ref, ys_hbm, ln_g_ref, ln_b_ref, o_ref, buf, sem):
    i = pl.program_id(0)
    n = pl.num_programs(0)
    depth = buf.shape[0]
    b = i % depth
    tm = x_ref.shape[0]

    def fetch(tile, slot):
        _for_tile_chunks(dst_ref, tot_ref, tile,
                         lambda c, d: _chunk_copy(ys_hbm, buf, sem, slot, c, d, False).start())

    @pl.when(i == 0)
    def _():
        for k in range(depth - 1):
            @pl.when(k < n)
            def _():
                fetch(k, k)

    @pl.when(i + (depth - 1) < n)
    def _():
        fetch(i + (depth - 1), (i + (depth - 1)) % depth)

    _for_tile_chunks(dst_ref, tot_ref, i, lambda c, d: _chunk_copy(ys_hbm, buf, sem, b, c, d, False).wait())
    live = lax.broadcasted_iota(jnp.int32, (MOE_SLOTS, 1), 0) < tot_ref[i]
    ys = jnp.where(live, buf[b], jnp.zeros((), BF16))
    mc = metac_ref[...]
    s1 = mc[:, 0:1].astype(jnp.int32)
    s2 = mc[:, 1:2].astype(jnp.int32)
    lane = lax.broadcasted_iota(jnp.int32, (tm, MOE_SLOTS), 1)
    cmb = (jnp.where(lane == s1, mc[:, 2:3], 0.0) + jnp.where(lane == s2, mc[:, 3:4], 0.0)).astype(BF16)
    o_ref[...] = _layer_norm(ALPHA * x_ref[...] + _dot(cmb, ys), ln_g_ref[...], ln_b_ref[...])


def _combine(x2d, meta_col, ys, dst_chunk, total, ln_g, ln_b):
    N, D = x2d.shape
    nt = N // MOE_TILE
    return pl.pallas_call(
        _combine_kernel,
        grid_spec=pltpu.PrefetchScalarGridSpec(
            num_scalar_prefetch=2, grid=(nt,),
            in_specs=[pl.BlockSpec((MOE_TILE, D), lambda i, dst, tot: (i, 0)),
                      pl.BlockSpec((MOE_TILE, 8), lambda i, dst, tot: (i, 0)),
                      pl.BlockSpec(memory_space=pl.ANY),
                      pl.BlockSpec(ln_g.shape, lambda i, dst, tot: (0, 0)),
                      pl.BlockSpec(ln_b.shape, lambda i, dst, tot: (0, 0))],
            out_specs=pl.BlockSpec((MOE_TILE, D), lambda i, dst, tot: (i, 0)),
            scratch_shapes=[pltpu.VMEM((COMBINE_DEPTH, MOE_SLOTS, D), BF16),
                            pltpu.SemaphoreType.DMA((COMBINE_DEPTH,))]),
        out_shape=jax.ShapeDtypeStruct((N, D), F32),
        compiler_params=pltpu.CompilerParams(dimension_semantics=("arbitrary",),
                                             vmem_limit_bytes=VMEM_LIMIT_BYTES),
    )(dst_chunk, total, x2d, meta_col, ys, ln_g, ln_b)


def _moe_sparse(x2d, rw, rb, wg, wu, wd, layer, ln_g, ln_b):
    N, D = x2d.shape
    E = rw.shape[1]
    nt = N // MOE_TILE
    meta, pc = _route(x2d, rw.T, rb.reshape(E, 1))
    pc = pc[:, :, 0].astype(jnp.int32)
    n_rows = nt * MOE_SLOTS + E * MOE_BLOCK
    assert n_rows % MOE_BLOCK == 0
    dst_chunk, total, block_expert, n_valid, fill, n_fill = _moe_schedule(pc, n_rows // MOE_BLOCK)
    xs = _dispatch(x2d, meta, dst_chunk, total, fill, n_fill, n_rows)
    ys = _ffn(xs, wg, wu, wd, layer, block_expert, n_valid)
    meta_col = jnp.swapaxes(meta, 1, 2).reshape(N, 8)
    return _combine(x2d, meta_col, ys, dst_chunk, total, ln_g, ln_b)


def _s5_weights_kernel(lam_col_ref, lam_row_ref, bt_ref, ct_ref, d_ref,
                       toe_ref, pst_ref, qst_ref, a_pow_ref):
    L, Cg = S5_CHUNK, S5_GROUP
    LC = L * Cg
    P = lam_row_ref.shape[-1]

    def discretise(a_re, a_im, log_dt):
        dt = jnp.exp(log_dt)
        lr, li = dt * a_re, dt * a_im
        er = jnp.exp(lr)
        abr, abi = er * jnp.cos(li), er * jnp.sin(li)
        nr, ni = abr - 1.0, abi
        den = a_re * a_re + a_im * a_im
        cr = (nr * a_re + ni * a_im) / den
        ci = (ni * a_re - nr * a_im) / den
        return lr, li, cr, ci

    def a_power(lr, li, tau):
        er = jnp.exp(tau * lr)
        return er * jnp.cos(tau * li), er * jnp.sin(tau * li)

    lc = lam_col_ref[...]
    lr_c, li_c, _, _ = discretise(lc[:, 0:1], lc[:, 1:2], lc[:, 2:3])
    lw = lam_row_ref[...]
    lr_r, li_r, cr_r, ci_r = discretise(lw[0:1, :], lw[1:2, :], lw[2:3, :])

    btr, bti = bt_ref[0], bt_ref[1]
    bbr = cr_r * btr - ci_r * bti
    bbi = cr_r * bti + ci_r * btr

    ctr, cti = ct_ref[0], ct_ref[1]
    tau_l = (lax.broadcasted_iota(jnp.int32, (P, LC), 1) // Cg).astype(F32)

    pr, pi = a_power(lr_c, li_c, tau_l)
    car = ctr * pr - cti * pi
    cai = ctr * pi + cti * pr
    t0 = (jnp.dot(bbr[0:Cg, :], car, preferred_element_type=F32, precision=HIGHEST)
          - jnp.dot(bbi[0:Cg, :], cai, preferred_element_type=F32, precision=HIGHEST))
    r16 = lax.broadcasted_iota(jnp.int32, (Cg, LC), 0)
    l16 = lax.broadcasted_iota(jnp.int32, (Cg, LC), 1)
    t0 = t0 + jnp.where(r16 == l16, d_ref[...], 0.0)
    toe_ref[0:Cg, :] = t0.astype(toe_ref.dtype)
    for s in range(1, L):
        toe_ref[s * Cg:(s + 1) * Cg, :] = jnp.where(
            l16 >= s * Cg, pltpu.roll(t0, s * Cg, axis=1), 0.0).astype(toe_ref.dtype)

    pr, pi = a_power(lr_c, li_c, tau_l + 1.0)
    qst_ref[0:P, :] = (ctr * pr - cti * pi).astype(qst_ref.dtype)
    qst_ref[P:2 * P, :] = (-(ctr * pi + cti * pr)).astype(qst_ref.dtype)

    tau_r = (L - 1 - lax.broadcasted_iota(jnp.int32, (LC, P), 0) // Cg).astype(F32)
    pr, pi = a_power(lr_r, li_r, tau_r)
    pst_ref[:, 0:P] = (bbr * pr - bbi * pi).astype(pst_ref.dtype)
    pst_ref[:, P:2 * P] = (bbr * pi + bbi * pr).astype(pst_ref.dtype)

    pr, pi = a_power(lr_r, li_r, float(L))
    a_pow_ref[0:1, :] = pr
    a_pow_ref[1:2, :] = pi


def _s5_weights(a_re, a_im, log_dt, b_re, b_im, c_re, c_im, d):
    G, P = a_re.shape
    Cg, L = S5_GROUP, S5_CHUNK
    LC = L * Cg
    lam = jnp.stack([a_re, a_im, jnp.broadcast_to(log_dt[:, None], (G, P)), jnp.zeros((G, P), F32)], axis=1)
    lam_row = lam
    lam_col = jnp.swapaxes(lam, 1, 2)
    bt = jnp.stack([jnp.swapaxes(b_re, 1, 2), jnp.swapaxes(b_im, 1, 2)], axis=1)
    bt = jnp.tile(bt, (1, 1, L, 1))
    ct = jnp.stack([jnp.swapaxes(c_re, 1, 2), jnp.swapaxes(c_im, 1, 2)], axis=1)
    ct = jnp.tile(ct, (1, 1, 1, L))
    d_col = d.reshape(G, Cg, 1)
    blk = lambda shp: pl.BlockSpec((None,) + shp, lambda g: (g,) + (0,) * len(shp))
    return pl.pallas_call(
        _s5_weights_kernel,
        grid=(G,),
        in_specs=[blk((P, 4)), blk((4, P)), blk((2, LC, P)), blk((2, P, LC)), blk((Cg, 1))],
        out_specs=(blk((LC, LC)), blk((LC, 2 * P)), blk((2 * P, LC)), blk((2, P))),
        out_shape=(jax.ShapeDtypeStruct((G, LC, LC), BF16),
                   jax.ShapeDtypeStruct((G, LC, 2 * P), BF16),
                   jax.ShapeDtypeStruct((G, 2 * P, LC), BF16),
                   jax.ShapeDtypeStruct((G, 2, P), F32)),
        compiler_params=pltpu.CompilerParams(dimension_semantics=("arbitrary",)),
    )(lam_col, lam_row, bt, ct, d_col)


def _granule_transpose(arrs):
    n = len(arrs)
    granule = lax.broadcasted_iota(jnp.int32, arrs[0].shape, 1) // S5_GROUP
    arrs = list(arrs)
    d = n // 2
    while d >= 1:
        upper = (granule & d) != 0
        for i in range(n):
            if i & d:
                continue
            lo, hi = arrs[i], arrs[i + d]
            arrs[i] = jnp.where(upper, pltpu.roll(hi, d * S5_GROUP, axis=1), lo)
            arrs[i + d] = jnp.where(upper, hi, pltpu.roll(lo, (n - d) * S5_GROUP, axis=1))
        d //= 2
    return arrs


def _s5_in_kernel(x_ref, w_ref, o_ref, u_scr):
    B, tt, D = x_ref.shape
    L, Cg = S5_CHUNK, S5_GROUP
    gpb = 128 // Cg
    cpb = tt // L
    u = _dot(x_ref[...].reshape(B * tt, D).astype(BF16), w_ref[...])
    for j in range(D // 128):
        u_scr[j] = u[:, 128 * j:128 * (j + 1)]
    for j in range(D // 128):
        for h in range(L // gpb):
            pieces = [jnp.concatenate([u_scr[j, pl.ds(c * L + h * gpb + q, B, stride=tt), :]
                                       for c in range(cpb)], axis=0) for q in range(gpb)]
            for gg, dest in enumerate(_granule_transpose(pieces)):
                o_ref[gpb * j + gg, :, 128 * h:128 * (h + 1)] = dest.astype(o_ref.dtype)


def _s5_in(x, w, tt):
    B, T, D = x.shape
    G = D // S5_GROUP
    LC = S5_CHUNK * S5_GROUP
    rows = (tt // S5_CHUNK) * B
    return pl.pallas_call(
        _s5_in_kernel,
        grid=(T // tt,),
        in_specs=[pl.BlockSpec((B, tt, D), lambda i: (0, i, 0)), pl.BlockSpec(w.shape, lambda i: (0, 0))],
        out_specs=pl.BlockSpec((G, rows, LC), lambda i: (0, i, 0)),
        out_shape=jax.ShapeDtypeStruct((G, (T // S5_CHUNK) * B, LC), BF16),
        scratch_shapes=[pltpu.VMEM((D // 128, B * tt, 128), F32)],
        compiler_params=pltpu.CompilerParams(dimension_semantics=("arbitrary",),
                                             vmem_limit_bytes=VMEM_LIMIT_BYTES),
    )(x, w)


def _s5_scan_kernel(x_ref, s0_ref, toe_ref, pst_ref, qst_ref, apow_ref, y_ref, sfin_ref,
                    zr_scr, zi_scr, sr_scr, si_scr, *, n_chunks, batch):
    P = apow_ref.shape[-1]
    for gi in range(x_ref.shape[0]):
        xg = x_ref[gi]
        z = _dot(xg, pst_ref[gi])
        zr_scr[...] = z[:, 0:P]
        zi_scr[...] = z[:, P:2 * P]
        ar = apow_ref[gi, 0:1, :]
        ai = apow_ref[gi, 1:2, :]

        def step(n, carry, ar=ar, ai=ai):
            sr, si = carry
            rows = pl.ds(pl.multiple_of(n * batch, batch), batch)
            sr_scr[rows, :] = sr
            si_scr[rows, :] = si
            return (ar * sr - ai * si + zr_scr[rows, :], ar * si + ai * sr + zi_scr[rows, :])

        sr, si = lax.fori_loop(0, n_chunks, step, (s0_ref[gi, 0], s0_ref[gi, 1]), unroll=4)
        sfin_ref[gi, 0] = sr
        sfin_ref[gi, 1] = si
        y = (_dot(xg, toe_ref[gi]) + _dot(sr_scr[...].astype(BF16), qst_ref[gi, 0:P, :])
             + _dot(si_scr[...].astype(BF16), qst_ref[gi, P:2 * P, :]))
        y_ref[gi] = y.astype(y_ref.dtype)


def _s5_scan(xg, s0, toe, pst, qst, apow, n_chunks, batch):
    G, R, LC = xg.shape
    P = apow.shape[-1]
    gb = max(1, min(8, S5_SCAN_ROWS_PER_STEP // R))
    blk = lambda shp: pl.BlockSpec((gb,) + shp, lambda g: (g,) + (0,) * len(shp))
    return pl.pallas_call(
        functools.partial(_s5_scan_kernel, n_chunks=n_chunks, batch=batch),
        grid=(G // gb,),
        in_specs=[blk((R, LC)), blk((2, batch, P)), blk((LC, LC)), blk((LC, 2 * P)), blk((2 * P, LC)),
                  blk((2, P))],
        out_specs=(blk((R, LC)), blk((2, batch, P))),
        out_shape=(jax.ShapeDtypeStruct((G, R, LC), BF16), jax.ShapeDtypeStruct((G, 2, batch, P), F32)),
        scratch_shapes=[pltpu.VMEM((R, P), F32)] * 4,
        compiler_params=pltpu.CompilerParams(dimension_semantics=("arbitrary",),
                                             vmem_limit_bytes=VMEM_LIMIT_BYTES),
    )(xg, s0, toe, pst, qst, apow)


def _s5_out_kernel(x_ref, yg_ref, w_ref, ln_g_ref, ln_b_ref, o_ref, y_scr):
    B, tt, D = x_ref.shape
    L, Cg = S5_CHUNK, S5_GROUP
    gpb = 128 // Cg
    cpb = tt // L
    for j in range(D // 128):
        for h in range(L // gpb):
            srcs = [yg_ref[gpb * j + gg, :, 128 * h:128 * (h + 1)].astype(F32) for gg in range(gpb)]
            for q, dest in enumerate(_granule_transpose(srcs)):
                for c in range(cpb):
                    y_scr[j, pl.ds(c * L + h * gpb + q, B, stride=tt), :] = dest[c * B:(c + 1) * B, :]
    y = jnp.concatenate([y_scr[j] for j in range(D // 128)], axis=1)
    act = _gelu_tanh(y).astype(BF16)
    zz = _dot(act, w_ref[...])
    mix = zz[:, :D] * _sigmoid(zz[:, D:])
    out = _layer_norm(ALPHA * x_ref[...].reshape(B * tt, D) + mix, ln_g_ref[...], ln_b_ref[...])
    o_ref[...] = out.reshape(B, tt, D)


def _s5_out(x, yg, w_out, ln_g, ln_b, tt):
    B, T, D = x.shape
    G, _, LC = yg.shape
    rows = (tt // S5_CHUNK) * B
    full = lambda a: pl.BlockSpec(a.shape, lambda i: (0,) * a.ndim)
    return pl.pallas_call(
        _s5_out_kernel,
        grid=(T // tt,),
        in_specs=[pl.BlockSpec((B, tt, D), lambda i: (0, i, 0)), pl.BlockSpec((G, rows, LC), lambda i: (0, i, 0)),
                  full(w_out), full(ln_g), full(ln_b)],
        out_specs=pl.BlockSpec((B, tt, D), lambda i: (0, i, 0)),
        out_shape=jax.ShapeDtypeStruct((B, T, D), F32),
        scratch_shapes=[pltpu.VMEM((D // 128, B * tt, 128), F32)],
        compiler_params=pltpu.CompilerParams(dimension_semantics=("arbitrary",),
                                             vmem_limit_bytes=VMEM_LIMIT_BYTES),
    )(x, yg, w_out, ln_g, ln_b)


def _s5_mixer(x, s0_re, s0_im, w_in, w_out, s5w, ln_g, ln_b):
    B, T, D = x.shape
    toe, pst, qst, apow = s5w
    xg = _s5_in(x, w_in, S5_STEP_TOKENS)
    s0 = jnp.stack([s0_re, s0_im], axis=0).transpose(2, 0, 1, 3)
    yg, sfin = _s5_scan(xg, s0, toe, pst, qst, apow, T // S5_CHUNK, B)
    out = _s5_out(x, yg, w_out, ln_g, ln_b, S5_STEP_TOKENS)
    sfin = sfin.transpose(1, 2, 0, 3)
    return out, sfin[0], sfin[1]


def _block_diag(w):
    H, d, _ = w.shape
    eye = jnp.eye(H, dtype=w.dtype)
    return (eye[:, None, :, None] * w[:, :, None, :]).reshape(H * d, H * d)


def kernel(x_prompt, x_sample, state_lru_h, state_lru_conv, state_hgrn2, state_s5_re, state_s5_im, ln_mix_g, ln_mix_b, ln_ffn_g, ln_ffn_b, w_in_ab, w_out_ab, lru_conv_w, lru_conv_b, lru_w_r, lru_b_r, lru_w_i, lru_b_i, lru_lam, hgrn_lb_logits, hgrn_norm_g, w_in_s5, w_out_s5, s5_a_re, s5_a_im, s5_b_re, s5_b_im, s5_c_re, s5_c_im, s5_d, s5_log_dt, router_w, router_b, moe_w_gate, moe_w_up, moe_w_down):
    Bp, Tp, D = x_prompt.shape
    Bs, Ts, _ = x_sample.shape
    W = state_lru_h.shape[-1]
    dk = W // HG_HEADS
    row = lambda v: v.reshape(1, -1)

    def moe(y, l, sparse):
        fn = _moe_sparse if sparse else functools.partial(_moe_dense, tm=512)
        out = fn(y.reshape(-1, D), router_w, row(router_b), moe_w_gate, moe_w_up, moe_w_down, l,
                 row(ln_ffn_g[l]), row(ln_ffn_b[l]))
        return out.reshape(y.shape)

    j = 0
    wts = (w_in_ab[j].astype(BF16), w_out_ab[j].astype(BF16), lru_conv_w[j], row(lru_conv_b[j]),
           _block_diag(lru_w_r[j]).astype(BF16), row(lru_b_r[j]),
           _block_diag(lru_w_i[j]).astype(BF16), row(lru_b_i[j]),
           row(lru_lam[j]), hgrn_lb_logits, row(hgrn_norm_g[j]), row(ln_mix_g[0]), row(ln_mix_b[0]))
    yp, p_h, p_conv, p_hg = _ab_mixer(
        x_prompt, jnp.zeros((Bp, 1, W), F32), jnp.zeros((Bp, 3, W), F32),
        jnp.zeros((Bp, HG_HEADS, dk, dk), F32), wts, AB_TILE, AB_SEQS)
    ys, s_h, s_conv, s_hg = _ab_mixer(
        x_sample, state_lru_h[j][:, None, :], state_lru_conv[j], state_hgrn2[j], wts, Ts, 2)
    yp = moe(yp, 0, True)
    ys = moe(ys, 0, False)

    s5w = _s5_weights(s5_a_re[j], s5_a_im[j], s5_log_dt[j], s5_b_re[j], s5_b_im[j],
                      s5_c_re[j], s5_c_im[j], s5_d[j])
    G, P = s5_a_re[j].shape
    w_in5 = w_in_s5[j].astype(BF16)
    w_out5 = w_out_s5[j].astype(BF16)
    zp = jnp.zeros((Bp, G, P), F32)
    yp, p_re, p_im = _s5_mixer(yp, zp, zp, w_in5, w_out5, s5w, row(ln_mix_g[1]), row(ln_mix_b[1]))
    ys, s_re, s_im = _s5_mixer(ys, state_s5_re[j], state_s5_im[j], w_in5, w_out5, s5w,
                               row(ln_mix_g[1]), row(ln_mix_b[1]))
    yp = moe(yp, 1, True)
    ys = moe(ys, 1, False)

    return (yp, ys,
            p_h.reshape(1, Bp, W), p_conv[None], p_hg[None], p_re[None], p_im[None],
            s_h.reshape(1, Bs, W), s_conv[None], s_hg[None], s_re[None], s_im[None])
```

```python
import functools
import math

import jax
import jax.numpy as jnp
from jax import lax
from jax.experimental import pallas as pl
from jax.experimental.pallas import tpu as pltpu

F32 = jnp.float32
BF16 = jnp.bfloat16
HIGHEST = lax.Precision.HIGHEST

DEPTH = 2
RG_C = 8.0
LRU_HEADS = 8
HG_HEADS = 4
AB_TILE = 512
AB_SEQS = 1
HG_CHUNK = 16
RMS_EPS = 1e-6
S5_GROUP = 16
S5_SCAN_ROWS_PER_STEP = 2048
S5_STEP_TOKENS = 32
S5_CHUNK = 16
N_EXPERT_GROUPS = 4
ALPHA = (2.0 * DEPTH) ** 0.25
LN_EPS = 1e-5
MOE_TILE = 512
MOE_CHUNK = 16
MOE_SLOTS = 1280
MOE_BLOCK = 1024
COMBINE_DEPTH = 2
ROUTE_TILES_PER_STEP = 4
VMEM_LIMIT_BYTES = 56 * 1024 * 1024


def _dot(a, b):
    return jnp.dot(a, b, preferred_element_type=F32)


def _dot_nt(a, b):
    return lax.dot_general(a, b, (((1,), (1,)), ((), ())), preferred_element_type=F32)


def _sigmoid(x):
    return 0.5 * jnp.tanh(0.5 * x) + 0.5


def _gelu_tanh(x):
    c = math.sqrt(2.0 / math.pi)
    return 0.5 * x * (1.0 + jnp.tanh(c * (x + 0.044715 * (x * x * x))))


def _layer_norm(s, g, b):
    mu = jnp.mean(s, axis=-1, keepdims=True)
    d = s - mu
    var = jnp.mean(d * d, axis=-1, keepdims=True)
    return d * lax.rsqrt(var + LN_EPS) * g + b


def _shift_rows(x, s, row):
    del row
    return pltpu.roll(x, s, axis=0)


def _ab_kernel(x_ref, h0_ref, conv0_ref, s0_ref, w_in_ref, w_out_ref, conv_w_ref, conv_b_ref,
               wr_ref, br_ref, wi_ref, bi_ref, lam_ref, lb_logits_ref, norm_g_ref, ln_g_ref, ln_b_ref,
               y_ref, h_out_ref, conv_out_ref, s_out_ref,
               zx_buf, h_carry, st_carry, st_all, *, tb, width, nb):
    for si in range(nb):
        _ab_sequence_tile(si, x_ref, h0_ref, conv0_ref, s0_ref, w_in_ref, w_out_ref, conv_w_ref, conv_b_ref,
                          wr_ref, br_ref, wi_ref, bi_ref, lam_ref, lb_logits_ref, norm_g_ref, ln_g_ref, ln_b_ref,
                          y_ref, h_out_ref, conv_out_ref, s_out_ref,
                          zx_buf.at[si], h_carry.at[si], st_carry.at[si], st_all.at[si], tb, width)


def _ab_sequence_tile(si, x_ref, h0_ref, conv0_ref, s0_ref, w_in_ref, w_out_ref, conv_w_ref, conv_b_ref,
                      wr_ref, br_ref, wi_ref, bi_ref, lam_ref, lb_logits_ref, norm_g_ref, ln_g_ref, ln_b_ref,
                      y_ref, h_out_ref, conv_out_ref, s_out_ref, zx_buf, h_carry, st_carry, st_all, tb, width):
    t = pl.program_id(1)
    nt = pl.num_programs(1)
    W = width
    L = HG_CHUNK
    nc = tb // L
    dk = W // HG_HEADS

    @pl.when(t == 0)
    def _init():
        zx_buf[0:8, :] = jnp.zeros((8, W), F32)
        zx_buf[5:8, :] = conv0_ref[si]
        h_carry[...] = jnp.broadcast_to(h0_ref[si], h_carry.shape)
        for hd in range(HG_HEADS):
            st_carry[hd] = s0_ref[si, hd].T

    x = x_ref[si]
    z = _dot(x.astype(BF16), w_in_ref[...])
    z_gate = z[:, 0 * W:1 * W]
    z_x = z[:, 1 * W:2 * W]
    z_q = z[:, 2 * W:3 * W]
    z_f = z[:, 3 * W:4 * W]
    z_i = z[:, 4 * W:5 * W]
    z_g = z[:, 5 * W:6 * W]

    zx_buf[8:8 + tb, :] = z_x
    cw = conv_w_ref[...]
    u = (conv_b_ref[...] + cw[3:4, :] * z_x
         + cw[2:3, :] * zx_buf[7:7 + tb, :]
         + cw[1:2, :] * zx_buf[6:6 + tb, :]
         + cw[0:1, :] * zx_buf[5:5 + tb, :])
    tail = zx_buf[tb:tb + 8, :]
    zx_buf[0:8, :] = tail

    ub = u.astype(BF16)
    r = _sigmoid(_dot(ub, wr_ref[...]) + br_ref[...])
    ig = _sigmoid(_dot(ub, wi_ref[...]) + bi_ref[...])
    nlam = -lam_ref[...]
    softplus_nlam = jnp.maximum(nlam, 0.0) + jnp.log1p(jnp.exp(-jnp.abs(nlam)))
    log_a = (-RG_C) * r * softplus_nlam
    a = jnp.exp(log_a)
    th = jnp.tanh(log_a)
    b = jnp.sqrt(-2.0 * th / (1.0 - th)) * ig * u

    row = lax.broadcasted_iota(jnp.int32, (tb, W), 0)
    rin8 = row % 8
    s = 1
    while s < 8:
        m = rin8 >= s
        a_sh = _shift_rows(a, s, row)
        b_sh = _shift_rows(b, s, row)
        b = jnp.where(m, a * b_sh + b, b)
        a = jnp.where(m, a * a_sh, a)
        s *= 2
    a3 = a.reshape(tb // 8, 8, W)
    b3 = b.reshape(tb // 8, 8, W)
    h_prev = h_carry[0:1, :]
    h_groups = []
    for g in range(tb // 8):
        hg = b3[g] + a3[g] * h_prev
        h_groups.append(hg)
        h_prev = hg[7:8, :]
    h = jnp.concatenate(h_groups, axis=0)
    h_carry[...] = jnp.broadcast_to(h_prev, h_carry.shape)
    out_a = h * _gelu_tanh(z_gate)

    lg = lb_logits_ref[...]
    e = jnp.exp(lg - jnp.max(lg, axis=0, keepdims=True))
    lb = e[0:1, :] / jnp.sum(e, axis=0, keepdims=True)
    f = lb + (1.0 - lb) * _sigmoid(z_f)
    log_f = jnp.log(f)
    kk = 1.0 - f
    rin = row % L
    G = log_f
    s = 1
    while s < L:
        G = G + jnp.where(rin >= s, _shift_rows(G, s, row), 0.0)
        s *= 2
    G3 = G.reshape(nc, L, W)
    G_last = G3[:, L - 1:L, :]
    q_dec = (z_q * jnp.exp(G)).reshape(nc, L, W).astype(BF16)
    k_inc = (kk * jnp.exp(-G)).reshape(nc, L, W).astype(BF16)
    k_tail = (kk.reshape(nc, L, W) * jnp.exp(G_last - G3)).astype(BF16)
    decay = jnp.exp(G_last)
    v3 = z_i.reshape(nc, L, W).astype(BF16)
    causal = (lax.broadcasted_iota(jnp.int32, (nc, L, L), 1)
              >= lax.broadcasted_iota(jnp.int32, (nc, L, L), 2))

    o_heads = []
    for hd in range(HG_HEADS):
        sl = slice(hd * dk, (hd + 1) * dk)
        qd, ki, kt, vv = q_dec[:, :, sl], k_inc[:, :, sl], k_tail[:, :, sl], v3[:, :, sl]
        scores = jnp.einsum('ntk,nsk->nts', qd, ki, preferred_element_type=F32)
        scores = jnp.where(causal, scores, 0.0).astype(BF16)
        o_intra = jnp.einsum('nts,nsv->ntv', scores, vv, preferred_element_type=F32)
        vt = jnp.swapaxes(vv.astype(F32), 1, 2).astype(BF16)
        upd = jnp.einsum('nvs,nsk->nvk', vt, kt, preferred_element_type=F32)
        st = st_carry[hd]
        for n in range(nc):
            st_all[hd, n] = st.astype(BF16)
            st = st * decay[n, :, sl] + upd[n]
        st_carry[hd] = st
        o_inter = jnp.einsum('ntk,nvk->ntv', qd, st_all[hd], preferred_element_type=F32)
        o = (o_intra + o_inter).reshape(tb, dk)
        o = o * lax.rsqrt(jnp.mean(o * o, axis=-1, keepdims=True) + RMS_EPS)
        o_heads.append(o)
    o_all = jnp.concatenate(o_heads, axis=-1) * norm_g_ref[...]
    out_b = o_all * (z_g * _sigmoid(z_g))

    mixed = jnp.concatenate([out_a, out_b], axis=-1).astype(BF16)
    y = _dot(mixed, w_out_ref[...])
    y_ref[si] = _layer_norm(ALPHA * x + y, ln_g_ref[...], ln_b_ref[...])

    @pl.when(t == nt - 1)
    def _fin():
        h_out_ref[si] = h[tb - 1:tb, :]
        conv_out_ref[si] = tail[5:8, :]
        for hd in range(HG_HEADS):
            s_out_ref[si, hd] = st_carry[hd].T


def _ab_mixer(x, h0, conv0, s0, wts, tb, nb):
    B, T, D = x.shape
    W = h0.shape[-1]
    dk = W // HG_HEADS
    nt = T // tb
    nc = tb // HG_CHUNK
    (w_in, w_out, conv_w, conv_b, wr, br, wi, bi, lam, lb_logits, norm_g, ln_g, ln_b) = wts
    full = lambda a: pl.BlockSpec(a.shape, lambda b, t: (0,) * a.ndim)
    per_b = lambda shp: pl.BlockSpec((nb,) + shp, lambda b, t: (b,) + (0,) * len(shp))
    out_shape = (jax.ShapeDtypeStruct((B, T, D), F32),
                 jax.ShapeDtypeStruct((B, 1, W), F32),
                 jax.ShapeDtypeStruct((B, 3, W), F32),
                 jax.ShapeDtypeStruct((B, HG_HEADS, dk, dk), F32))
    return pl.pallas_call(
        functools.partial(_ab_kernel, tb=tb, width=W, nb=nb),
        grid=(B // nb, nt),
        in_specs=[pl.BlockSpec((nb, tb, D), lambda b, t: (b, t, 0)),
                  per_b((1, W)), per_b((3, W)), per_b((HG_HEADS, dk, dk))]
                 + [full(a) for a in wts],
        out_specs=(pl.BlockSpec((nb, tb, D), lambda b, t: (b, t, 0)),
                   per_b((1, W)), per_b((3, W)), per_b((HG_HEADS, dk, dk))),
        out_shape=out_shape,
        scratch_shapes=[pltpu.VMEM((nb, tb + 8, W), F32),
                        pltpu.VMEM((nb, 8, W), F32),
                        pltpu.VMEM((nb, HG_HEADS, dk, dk), F32),
                        pltpu.VMEM((nb, HG_HEADS, nc, dk, dk), BF16)],
        compiler_params=pltpu.CompilerParams(
            dimension_semantics=("arbitrary", "arbitrary"),
            vmem_limit_bytes=VMEM_LIMIT_BYTES),
    )(x, h0, conv0, s0, *wts)


def _router(x, rw, rb, n_experts):
    per_group = n_experts // N_EXPERT_GROUPS
    logits = jnp.dot(x, rw, preferred_element_type=F32, precision=HIGHEST) + rb
    logits = logits - jnp.max(logits, axis=-1, keepdims=True)
    ex = jnp.exp(logits)
    p = ex / jnp.sum(ex, axis=-1, keepdims=True)
    lane_i = lax.broadcasted_iota(jnp.int32, p.shape, 1)
    grp = lane_i // per_group
    lane = lane_i.astype(F32)
    big = float(n_experts)

    def top2(pm):
        m1 = jnp.max(pm, axis=-1, keepdims=True)
        i1 = jnp.min(jnp.where(pm == m1, lane, big), axis=-1, keepdims=True)
        pm2 = jnp.where(lane == i1, -1.0, pm)
        m2 = jnp.max(pm2, axis=-1, keepdims=True)
        i2 = jnp.min(jnp.where(pm2 == m2, lane, big), axis=-1, keepdims=True)
        return m1, i1, m2, i2

    best = None
    g_sel = None
    for j in range(N_EXPERT_GROUPS):
        m1, _, m2, _ = top2(jnp.where(grp == j, p, -1.0))
        score = m1 + m2
        if best is None:
            best, g_sel = score, jnp.zeros_like(lane_i[:, :1])
        else:
            better = score > best
            best = jnp.where(better, score, best)
            g_sel = jnp.where(better, j, g_sel)
    m1, i1, m2, i2 = top2(jnp.where(grp == g_sel, p, -1.0))
    den = m1 + m2
    return jnp.where(lane == i1, m1 / den, 0.0) + jnp.where(lane == i2, m2 / den, 0.0)


def _moe_dense_kernel(x_ref, rw_ref, rb_ref, wg_ref, wu_ref, wd_ref, ln_g_ref, ln_b_ref,
                      y_ref, xb_scr, comb_scr, acc_scr, *, n_experts):
    e = pl.program_id(1)

    @pl.when(e == 0)
    def _init():
        x = x_ref[...]
        xb_scr[...] = x.astype(BF16)
        comb_scr[...] = _router(x, rw_ref[...], rb_ref[...], n_experts)
        acc_scr[...] = jnp.zeros_like(acc_scr)

    comb = comb_scr[...]
    lane = lax.broadcasted_iota(jnp.int32, comb.shape, 1)
    c_e = jnp.sum(jnp.where(lane == e, comb, 0.0), axis=-1, keepdims=True)
    xb = xb_scr[...]
    g = _dot(xb, wg_ref[...].astype(BF16))
    up = _dot(xb, wu_ref[...].astype(BF16))
    hmid = (g * _sigmoid(g) * up).astype(BF16)
    acc_scr[...] += c_e * _dot(hmid, wd_ref[...].astype(BF16))

    @pl.when(e == n_experts - 1)
    def _fin():
        y_ref[...] = _layer_norm(ALPHA * x_ref[...] + acc_scr[...], ln_g_ref[...], ln_b_ref[...])


def _moe_dense(x2d, rw, rb, wg, wu, wd, layer, ln_g, ln_b, tm):
    N, D = x2d.shape
    _, E, _, De = wg.shape
    full = lambda a: pl.BlockSpec(a.shape, lambda i, e: (0,) * a.ndim)
    return pl.pallas_call(
        functools.partial(_moe_dense_kernel, n_experts=E),
        grid=(N // tm, E),
        in_specs=[pl.BlockSpec((tm, D), lambda i, e: (i, 0)), full(rw), full(rb),
                  pl.BlockSpec((None, None, D, De), lambda i, e: (layer, e, 0, 0)),
                  pl.BlockSpec((None, None, D, De), lambda i, e: (layer, e, 0, 0)),
                  pl.BlockSpec((None, None, De, D), lambda i, e: (layer, e, 0, 0)),
                  full(ln_g), full(ln_b)],
        out_specs=pl.BlockSpec((tm, D), lambda i, e: (i, 0)),
        out_shape=jax.ShapeDtypeStruct((N, D), F32),
        scratch_shapes=[pltpu.VMEM((tm, D), BF16), pltpu.VMEM((tm, E), F32), pltpu.VMEM((tm, D), F32)],
        compiler_params=pltpu.CompilerParams(
            dimension_semantics=("arbitrary", "arbitrary"),
            vmem_limit_bytes=VMEM_LIMIT_BYTES),
    )(x2d, rw, rb, wg, wu, wd, ln_g, ln_b)


def _route_kernel(x_ref, rwt_ref, rbc_ref, meta_ref, pc_ref, *, n_experts):
    E = n_experts
    per_group = E // N_EXPERT_GROUPS
    x = x_ref[...]
    tm = x.shape[0]
    logits = lax.dot_general(rwt_ref[...], x, (((1,), (1,)), ((), ())),
                             precision=HIGHEST, preferred_element_type=F32) + rbc_ref[...]
    logits = logits - jnp.max(logits, axis=0, keepdims=True)
    ex = jnp.exp(logits)
    p = ex / jnp.sum(ex, axis=0, keepdims=True)
    eid_i = lax.broadcasted_iota(jnp.int32, (E, tm), 0)
    grp = eid_i // per_group
    eid = eid_i.astype(F32)
    big = float(E)

    def top2(pm):
        m1 = jnp.max(pm, axis=0, keepdims=True)
        i1 = jnp.min(jnp.where(pm == m1, eid, big), axis=0, keepdims=True)
        pm2 = jnp.where(eid == i1, -1.0, pm)
        m2 = jnp.max(pm2, axis=0, keepdims=True)
        i2 = jnp.min(jnp.where(pm2 == m2, eid, big), axis=0, keepdims=True)
        return m1, i1, m2, i2

    best = None
    g_sel = None
    for j in range(N_EXPERT_GROUPS):
        m1, _, m2, _ = top2(jnp.where(grp == j, p, -1.0))
        score = m1 + m2
        if best is None:
            best, g_sel = score, jnp.zeros_like(eid_i[:1, :])
        else:
            better = score > best
            best = jnp.where(better, score, best)
            g_sel = jnp.where(better, j, g_sel)
    m1, i1, m2, i2 = top2(jnp.where(grp == g_sel, p, -1.0))
    den = m1 + m2

    oh1_all = jnp.where(eid == i1, 1.0, 0.0)
    oh2_all = jnp.where(eid == i2, 1.0, 0.0)
    g1_all = m1 / den
    g2_all = m2 / den
    tl = MOE_TILE
    earlier = jnp.where(lax.broadcasted_iota(jnp.int32, (tl, tl), 0)
                        < lax.broadcasted_iota(jnp.int32, (tl, tl), 1), 1.0, 0.0).astype(BF16)
    below = jnp.where(lax.broadcasted_iota(jnp.int32, (E, E), 1)
                      < lax.broadcasted_iota(jnp.int32, (E, E), 0), 1.0, 0.0)
    for k in range(tm // tl):
        lanes = slice(k * tl, (k + 1) * tl)
        oh1, oh2 = oh1_all[:, lanes], oh2_all[:, lanes]
        r1 = _dot(oh1.astype(BF16), earlier)
        r2 = _dot(oh2.astype(BF16), earlier)
        cnt1 = jnp.sum(oh1, axis=1, keepdims=True)
        cnt2 = jnp.sum(oh2, axis=1, keepdims=True)
        pc = jnp.floor((cnt1 + cnt2 + (MOE_CHUNK - 1.0)) * (1.0 / MOE_CHUNK)) * MOE_CHUNK
        pc_b = jnp.broadcast_to(pc, (E, 128))
        run_off = jnp.dot(below, pc_b, precision=HIGHEST, preferred_element_type=F32)[:, 0:1]
        meta_ref[k, 0:1, :] = jnp.sum(oh1 * (run_off + r1), axis=0, keepdims=True)
        meta_ref[k, 1:2, :] = jnp.sum(oh2 * (run_off + cnt1 + r2), axis=0, keepdims=True)
        meta_ref[k, 2:3, :] = g1_all[:, lanes]
        meta_ref[k, 3:4, :] = g2_all[:, lanes]
        meta_ref[k, 4:8, :] = jnp.zeros((4, tl), F32)
        pc_ref[k] = pc_b


def _route(x2d, rwt, rbc):
    N, D = x2d.shape
    E = rwt.shape[0]
    nt = N // MOE_TILE
    tps = ROUTE_TILES_PER_STEP if nt % ROUTE_TILES_PER_STEP == 0 else 1
    full = lambda a: pl.BlockSpec(a.shape, lambda i: (0,) * a.ndim)
    return pl.pallas_call(
        functools.partial(_route_kernel, n_experts=E),
        grid=(nt // tps,),
        in_specs=[pl.BlockSpec((tps * MOE_TILE, D), lambda i: (i, 0)), full(rwt), full(rbc)],
        out_specs=(pl.BlockSpec((tps, 8, MOE_TILE), lambda i: (i, 0, 0)),
                   pl.BlockSpec((tps, E, 128), lambda i: (i, 0, 0))),
        out_shape=(jax.ShapeDtypeStruct((nt, 8, MOE_TILE), F32), jax.ShapeDtypeStruct((nt, E, 128), F32)),
        compiler_params=pltpu.CompilerParams(dimension_semantics=("arbitrary",),
                                             vmem_limit_bytes=VMEM_LIMIT_BYTES),
    )(x2d, rwt, rbc)


def _moe_schedule(pc, n_blocks_max):
    nt, E = pc.shape
    cum = jnp.cumsum(pc, axis=1)
    run_off = cum - pc
    total = cum[:, -1]
    seg_len = jnp.sum(pc, axis=0)
    seg_pad = ((seg_len + MOE_BLOCK - 1) // MOE_BLOCK) * MOE_BLOCK
    seg_end = jnp.cumsum(seg_pad)
    goff = (seg_end - seg_pad)[None, :] + (jnp.cumsum(pc, axis=0) - pc)
    r = jnp.arange(MOE_SLOTS // MOE_CHUNK, dtype=jnp.int32) * MOE_CHUNK
    e_c = jnp.minimum(jnp.sum((cum[:, None, :] <= r[None, :, None]).astype(jnp.int32), axis=-1), E - 1)
    pick = e_c[:, :, None] == jnp.arange(E, dtype=jnp.int32)[None, None, :]
    dst_row = r[None, :] + jnp.sum(jnp.where(pick, (goff - run_off)[:, None, :], 0), axis=-1)
    dst_chunk = jnp.where(r[None, :] < total[:, None], dst_row // MOE_CHUNK, -1).astype(jnp.int32)
    blk_start = jnp.arange(n_blocks_max, dtype=jnp.int32) * MOE_BLOCK
    block_expert = jnp.minimum(jnp.sum((seg_end[None, :] <= blk_start[:, None]).astype(jnp.int32), axis=-1), E - 1)
    n_valid = (seg_end[-1] // MOE_BLOCK).reshape(1).astype(jnp.int32)
    n_chunks_all = n_blocks_max * (MOE_BLOCK // MOE_CHUNK)
    n_fill_max = n_chunks_all - (nt * 2 * MOE_TILE) // MOE_CHUNK
    gap_start = jnp.concatenate([seg_end - seg_pad + seg_len, seg_end[-1:]]) // MOE_CHUNK
    gap_len = jnp.concatenate([seg_pad - seg_len, n_chunks_all * MOE_CHUNK - seg_end[-1:]]) // MOE_CHUNK
    gap_cum = jnp.cumsum(gap_len)
    k = jnp.arange(n_fill_max, dtype=jnp.int32)
    which = jnp.minimum(jnp.sum((gap_cum[None, :] <= k[:, None]).astype(jnp.int32), axis=-1), E)
    pick_gap = which[:, None] == jnp.arange(E + 1, dtype=jnp.int32)[None, :]
    fill = k + jnp.sum(jnp.where(pick_gap, (gap_start - (gap_cum - gap_len))[None, :], 0), axis=-1)
    fill = jnp.where(k < gap_cum[-1], fill, 0).astype(jnp.int32)
    n_fill = gap_cum[-1].reshape(1).astype(jnp.int32)
    return (dst_chunk.reshape(-1), total.astype(jnp.int32), block_expert.astype(jnp.int32), n_valid, fill, n_fill)


def _chunk_copy(hbm_ref, buf, sem, b, c, d, to_hbm):
    local = buf.at[b, pl.ds(pl.multiple_of(c * MOE_CHUNK, MOE_CHUNK), MOE_CHUNK)]
    remote = hbm_ref.at[pl.ds(pl.multiple_of(d * MOE_CHUNK, MOE_CHUNK), MOE_CHUNK)]
    if to_hbm:
        return pltpu.make_async_copy(local, remote, sem.at[b])
    return pltpu.make_async_copy(remote, local, sem.at[b])


def _for_tile_chunks(dst_ref, tot_ref, tile, fn):
    slots = MOE_SLOTS // MOE_CHUNK

    def body(c, carry):
        fn(c, dst_ref[tile * slots + c])
        return carry

    lax.fori_loop(0, tot_ref[tile] // MOE_CHUNK, body, 0)


def _dispatch_kernel(dst_ref, tot_ref, fill_ref, nfill_ref, x_ref, meta_ref, xs_hbm, buf, sem, zbuf, zsem):
    i = pl.program_id(0)
    n = pl.num_programs(0)
    b = i % 2
    tm = x_ref.shape[0]

    def wait_tile(tile, slot):
        _for_tile_chunks(dst_ref, tot_ref, tile,
                         lambda c, d: _chunk_copy(xs_hbm, buf, sem, slot, c, d, True).wait())

    def zero_copy(k):
        d = fill_ref[k]
        return pltpu.make_async_copy(
            zbuf, xs_hbm.at[pl.ds(pl.multiple_of(d * MOE_CHUNK, MOE_CHUNK), MOE_CHUNK)], zsem.at[0])

    @pl.when(i == 0)
    def _():
        zbuf[...] = jnp.zeros_like(zbuf)
        lax.fori_loop(0, nfill_ref[0], lambda k, c: (zero_copy(k).start(), c)[1], 0)

    @pl.when(i >= 2)
    def _():
        wait_tile(i - 2, b)

    s1 = meta_ref[0:1, :].astype(jnp.int32)
    s2 = meta_ref[1:2, :].astype(jnp.int32)
    rows = lax.broadcasted_iota(jnp.int32, (MOE_SLOTS, tm), 0)
    dsp = jnp.where((rows == s1) | (rows == s2), 1.0, 0.0).astype(BF16)
    buf[b] = _dot(dsp, x_ref[...].astype(BF16)).astype(BF16)
    _for_tile_chunks(dst_ref, tot_ref, i, lambda c, d: _chunk_copy(xs_hbm, buf, sem, b, c, d, True).start())

    @pl.when(i == n - 1)
    def _():
        wait_tile(i, b)

    @pl.when((i == n - 1) & (i >= 1))
    def _():
        wait_tile(i - 1, 1 - b)

    @pl.when(i == n - 1)
    def _():
        lax.fori_loop(0, nfill_ref[0], lambda k, c: (zero_copy(k).wait(), c)[1], 0)


def _dispatch(x2d, meta, dst_chunk, total, fill, n_fill, n_rows):
    N, D = x2d.shape
    nt = N // MOE_TILE
    return pl.pallas_call(
        _dispatch_kernel,
        grid_spec=pltpu.PrefetchScalarGridSpec(
            num_scalar_prefetch=4, grid=(nt,),
            in_specs=[pl.BlockSpec((MOE_TILE, D), lambda i, *_: (i, 0)),
                      pl.BlockSpec((None, 8, MOE_TILE), lambda i, *_: (i, 0, 0))],
            out_specs=pl.BlockSpec(memory_space=pl.ANY),
            scratch_shapes=[pltpu.VMEM((2, MOE_SLOTS, D), BF16), pltpu.SemaphoreType.DMA((2,)),
                            pltpu.VMEM((MOE_CHUNK, D), BF16), pltpu.SemaphoreType.DMA((1,))]),
        out_shape=jax.ShapeDtypeStruct((n_rows, D), BF16),
        compiler_params=pltpu.CompilerParams(dimension_semantics=("arbitrary",),
                                             vmem_limit_bytes=VMEM_LIMIT_BYTES),
    )(dst_chunk, total, fill, n_fill, x2d, meta)


def _ffn_kernel(be_ref, nv_ref, xs_ref, wg_ref, wu_ref, wd_ref, ys_ref, wg_b, wu_b, wd_b):
    j = pl.program_id(0)
    used = j < nv_ref[0]
    new_expert = (j == 0) | (be_ref[j] != be_ref[jnp.maximum(j - 1, 0)])

    @pl.when(used & new_expert)
    def _():
        wg_b[...] = wg_ref[...].astype(BF16)
        wu_b[...] = wu_ref[...].astype(BF16)
        wd_b[...] = wd_ref[...].astype(BF16)

    @pl.when(used)
    def _():
        xb = xs_ref[...]
        g = _dot(xb, wg_b[...])
        up = _dot(xb, wu_b[...])
        hmid = (g * _sigmoid(g) * up).astype(BF16)
        ys_ref[...] = _dot(hmid, wd_b[...]).astype(ys_ref.dtype)

    @pl.when(jnp.logical_not(used))
    def _():
        ys_ref[...] = jnp.zeros_like(ys_ref)


def _ffn(xs, wg, wu, wd, layer, block_expert, n_valid):
    R, D = xs.shape
    _, E, _, De = wg.shape
    last = lambda j, nv: jnp.maximum(jnp.minimum(j, nv[0] - 1), 0)
    return pl.pallas_call(
        _ffn_kernel,
        grid_spec=pltpu.PrefetchScalarGridSpec(
            num_scalar_prefetch=2, grid=(R // MOE_BLOCK,),
            in_specs=[pl.BlockSpec((MOE_BLOCK, D), lambda j, be, nv: (last(j, nv), 0)),
                      pl.BlockSpec((None, None, D, De), lambda j, be, nv: (layer, be[last(j, nv)], 0, 0)),
                      pl.BlockSpec((None, None, D, De), lambda j, be, nv: (layer, be[last(j, nv)], 0, 0)),
                      pl.BlockSpec((None, None, De, D), lambda j, be, nv: (layer, be[last(j, nv)], 0, 0))],
            out_specs=pl.BlockSpec((MOE_BLOCK, D), lambda j, be, nv: (j, 0)),
            scratch_shapes=[pltpu.VMEM((D, De), BF16), pltpu.VMEM((D, De), BF16), pltpu.VMEM((De, D), BF16)]),
        out_shape=jax.ShapeDtypeStruct((R, D), BF16),
        compiler_params=pltpu.CompilerParams(dimension_semantics=("arbitrary",),
                                             vmem_limit_bytes=VMEM_LIMIT_BYTES),
    )(block_expert, n_valid, xs, wg, wu, wd)


def _combine_kernel(dst_ref, tot_ref, x_ref, metac_ref, ys_hbm, ln_g_ref, ln_b_ref, o_ref, buf, sem):
    i = pl.program_id(0)
    n = pl.num_programs(0)
    depth = buf.shape[0]
    b = i % depth
    tm = x_ref.shape[0]

    def fetch(tile, slot):
        _for_tile_chunks(dst_ref, tot_ref, tile,
                         lambda c, d: _chunk_copy(ys_hbm, buf, sem, slot, c, d, False).start())

    @pl.when(i == 0)
    def _():
        for k in range(depth - 1):
            @pl.when(k < n)
            def _():
                fetch(k, k)

    @pl.when(i + (depth - 1) < n)
    def _():
        fetch(i + (depth - 1), (i + (depth - 1)) % depth)

    _for_tile_chunks(dst_ref, tot_ref, i, lambda c, d: _chunk_copy(ys_hbm, buf, sem, b, c, d, False).wait())
    live = lax.broadcasted_iota(jnp.int32, (MOE_SLOTS, 1), 0) < tot_ref[i]
    ys = jnp.where(live, buf[b], jnp.zeros((), BF16))
    mc = metac_ref[...]
    s1 = mc[:, 0:1].astype(jnp.int32)
    s2 = mc[:, 1:2].astype(jnp.int32)
    lane = lax.broadcasted_iota(jnp.int32, (tm, MOE_SLOTS), 1)
    cmb = (jnp.where(lane == s1, mc[:, 2:3], 0.0) + jnp.where(lane == s2, mc[:, 3:4], 0.0)).astype(BF16)
    o_ref[...] = _layer_norm(ALPHA * x_ref[...] + _dot(cmb, ys), ln_g_ref[...], ln_b_ref[...])


def _combine(x2d, meta_col, ys, dst_chunk, total, ln_g, ln_b):
    N, D = x2d.shape
    nt = N // MOE_TILE
    return pl.pallas_call(
        _combine_kernel,
        grid_spec=pltpu.PrefetchScalarGridSpec(
            num_scalar_prefetch=2, grid=(nt,),
            in_specs=[pl.BlockSpec((MOE_TILE, D), lambda i, dst, tot: (i, 0)),
                      pl.BlockSpec((MOE_TILE, 8), lambda i, dst, tot: (i, 0)),
                      pl.BlockSpec(memory_space=pl.ANY),
                      pl.BlockSpec(ln_g.shape, lambda i, dst, tot: (0, 0)),
                      pl.BlockSpec(ln_b.shape, lambda i, dst, tot: (0, 0))],
            out_specs=pl.BlockSpec((MOE_TILE, D), lambda i, dst, tot: (i, 0)),
            scratch_shapes=[pltpu.VMEM((COMBINE_DEPTH, MOE_SLOTS, D), BF16),
                            pltpu.SemaphoreType.DMA((COMBINE_DEPTH,))]),
        out_shape=jax.ShapeDtypeStruct((N, D), F32),
        compiler_params=pltpu.CompilerParams(dimension_semantics=("arbitrary",),
                                             vmem_limit_bytes=VMEM_LIMIT_BYTES),
    )(dst_chunk, total, x2d, meta_col, ys, ln_g, ln_b)


def _moe_sparse(x2d, rw, rb, wg, wu, wd, layer, ln_g, ln_b):
    N, D = x2d.shape
    E = rw.shape[1]
    nt = N // MOE_TILE
    meta, pc = _route(x2d, rw.T, rb.reshape(E, 1))
    pc = pc[:, :, 0].astype(jnp.int32)
    n_rows = nt * MOE_SLOTS + E * MOE_BLOCK
    assert n_rows % MOE_BLOCK == 0
    dst_chunk, total, block_expert, n_valid, fill, n_fill = _moe_schedule(pc, n_rows // MOE_BLOCK)
    xs = _dispatch(x2d, meta, dst_chunk, total, fill, n_fill, n_rows)
    ys = _ffn(xs, wg, wu, wd, layer, block_expert, n_valid)
    meta_col = jnp.swapaxes(meta, 1, 2).reshape(N, 8)
    return _combine(x2d, meta_col, ys, dst_chunk, total, ln_g, ln_b)


def _s5_weights_kernel(lam_col_ref, lam_row_ref, bt_ref, ct_ref, d_ref,
                       toe_ref, pst_ref, qst_ref, a_pow_ref):
    L, Cg = S5_CHUNK, S5_GROUP
    LC = L * Cg
    P = lam_row_ref.shape[-1]

    def discretise(a_re, a_im, log_dt):
        dt = jnp.exp(log_dt)
        lr, li = dt * a_re, dt * a_im
        er = jnp.exp(lr)
        abr, abi = er * jnp.cos(li), er * jnp.sin(li)
        nr, ni = abr - 1.0, abi
        den = a_re * a_re + a_im * a_im
        cr = (nr * a_re + ni * a_im) / den
        ci = (ni * a_re - nr * a_im) / den
        return lr, li, cr, ci

    def a_power(lr, li, tau):
        er = jnp.exp(tau * lr)
        return er * jnp.cos(tau * li), er * jnp.sin(tau * li)

    lc = lam_col_ref[...]
    lr_c, li_c, _, _ = discretise(lc[:, 0:1], lc[:, 1:2], lc[:, 2:3])
    lw = lam_row_ref[...]
    lr_r, li_r, cr_r, ci_r = discretise(lw[0:1, :], lw[1:2, :], lw[2:3, :])

    btr, bti = bt_ref[0], bt_ref[1]
    bbr = cr_r * btr - ci_r * bti
    bbi = cr_r * bti + ci_r * btr

    ctr, cti = ct_ref[0], ct_ref[1]
    tau_l = (lax.broadcasted_iota(jnp.int32, (P, LC), 1) // Cg).astype(F32)

    pr, pi = a_power(lr_c, li_c, tau_l)
    car = ctr * pr - cti * pi
    cai = ctr * pi + cti * pr
    t0 = (jnp.dot(bbr[0:Cg, :], car, preferred_element_type=F32, precision=HIGHEST)
          - jnp.dot(bbi[0:Cg, :], cai, preferred_element_type=F32, precision=HIGHEST))
    r16 = lax.broadcasted_iota(jnp.int32, (Cg, LC), 0)
    l16 = lax.broadcasted_iota(jnp.int32, (Cg, LC), 1)
    t0 = t0 + jnp.where(r16 == l16, d_ref[...], 0.0)
    toe_ref[0:Cg, :] = t0.astype(toe_ref.dtype)
    for s in range(1, L):
        toe_ref[s * Cg:(s + 1) * Cg, :] = jnp.where(
            l16 >= s * Cg, pltpu.roll(t0, s * Cg, axis=1), 0.0).astype(toe_ref.dtype)

    pr, pi = a_power(lr_c, li_c, tau_l + 1.0)
    qst_ref[0:P, :] = (ctr * pr - cti * pi).astype(qst_ref.dtype)
    qst_ref[P:2 * P, :] = (-(ctr * pi + cti * pr)).astype(qst_ref.dtype)

    tau_r = (L - 1 - lax.broadcasted_iota(jnp.int32, (LC, P), 0) // Cg).astype(F32)
    pr, pi = a_power(lr_r, li_r, tau_r)
    pst_ref[:, 0:P] = (bbr * pr - bbi * pi).astype(pst_ref.dtype)
    pst_ref[:, P:2 * P] = (bbr * pi + bbi * pr).astype(pst_ref.dtype)

    pr, pi = a_power(lr_r, li_r, float(L))
    a_pow_ref[0:1, :] = pr
    a_pow_ref[1:2, :] = pi


def _s5_weights(a_re, a_im, log_dt, b_re, b_im, c_re, c_im, d):
    G, P = a_re.shape
    Cg, L = S5_GROUP, S5_CHUNK
    LC = L * Cg
    lam = jnp.stack([a_re, a_im, jnp.broadcast_to(log_dt[:, None], (G, P)), jnp.zeros((G, P), F32)], axis=1)
    lam_row = lam
    lam_col = jnp.swapaxes(lam, 1, 2)
    bt = jnp.stack([jnp.swapaxes(b_re, 1, 2), jnp.swapaxes(b_im, 1, 2)], axis=1)
    bt = jnp.tile(bt, (1, 1, L, 1))
    ct = jnp.stack([jnp.swapaxes(c_re, 1, 2), jnp.swapaxes(c_im, 1, 2)], axis=1)
    ct = jnp.tile(ct, (1, 1, 1, L))
    d_col = d.reshape(G, Cg, 1)
    blk = lambda shp: pl.BlockSpec((None,) + shp, lambda g: (g,) + (0,) * len(shp))
    return pl.pallas_call(
        _s5_weights_kernel,
        grid=(G,),
        in_specs=[blk((P, 4)), blk((4, P)), blk((2, LC, P)), blk((2, P, LC)), blk((Cg, 1))],
        out_specs=(blk((LC, LC)), blk((LC, 2 * P)), blk((2 * P, LC)), blk((2, P))),
        out_shape=(jax.ShapeDtypeStruct((G, LC, LC), BF16),
                   jax.ShapeDtypeStruct((G, LC, 2 * P), BF16),
                   jax.ShapeDtypeStruct((G, 2 * P, LC), BF16),
                   jax.ShapeDtypeStruct((G, 2, P), F32)),
        compiler_params=pltpu.CompilerParams(dimension_semantics=("arbitrary",)),
    )(lam_col, lam_row, bt, ct, d_col)


def _granule_transpose(arrs):
    n = len(arrs)
    granule = lax.broadcasted_iota(jnp.int32, arrs[0].shape, 1) // S5_GROUP
    arrs = list(arrs)
    d = n // 2
    while d >= 1:
        upper = (granule & d) != 0
        for i in range(n):
            if i & d:
                continue
            lo, hi = arrs[i], arrs[i + d]
            arrs[i] = jnp.where(upper, pltpu.roll(hi, d * S5_GROUP, axis=1), lo)
            arrs[i + d] = jnp.where(upper, hi, pltpu.roll(lo, (n - d) * S5_GROUP, axis=1))
        d //= 2
    return arrs


def _s5_in_kernel(x_ref, w_ref, o_ref, u_scr):
    B, tt, D = x_ref.shape
    L, Cg = S5_CHUNK, S5_GROUP
    gpb = 128 // Cg
    cpb = tt // L
    u = _dot(x_ref[...].reshape(B * tt, D).astype(BF16), w_ref[...])
    for j in range(D // 128):
        u_scr[j] = u[:, 128 * j:128 * (j + 1)]
    for j in range(D // 128):
        for h in range(L // gpb):
            pieces = [jnp.concatenate([u_scr[j, pl.ds(c * L + h * gpb + q, B, stride=tt), :]
                                       for c in range(cpb)], axis=0) for q in range(gpb)]
            for gg, dest in enumerate(_granule_transpose(pieces)):
                o_ref[gpb * j + gg, :, 128 * h:128 * (h + 1)] = dest.astype(o_ref.dtype)


def _s5_in(x, w, tt):
    B, T, D = x.shape
    G = D // S5_GROUP
    LC = S5_CHUNK * S5_GROUP
    rows = (tt // S5_CHUNK) * B
    return pl.pallas_call(
        _s5_in_kernel,
        grid=(T // tt,),
        in_specs=[pl.BlockSpec((B, tt, D), lambda i: (0, i, 0)), pl.BlockSpec(w.shape, lambda i: (0, 0))],
        out_specs=pl.BlockSpec((G, rows, LC), lambda i: (0, i, 0)),
        out_shape=jax.ShapeDtypeStruct((G, (T // S5_CHUNK) * B, LC), BF16),
        scratch_shapes=[pltpu.VMEM((D // 128, B * tt, 128), F32)],
        compiler_params=pltpu.CompilerParams(dimension_semantics=("arbitrary",),
                                             vmem_limit_bytes=VMEM_LIMIT_BYTES),
    )(x, w)


def _s5_scan_kernel(x_ref, s0_ref, toe_ref, pst_ref, qst_ref, apow_ref, y_ref, sfin_ref,
                    zr_scr, zi_scr, sr_scr, si_scr, *, n_chunks, batch):
    P = apow_ref.shape[-1]
    for gi in range(x_ref.shape[0]):
        xg = x_ref[gi]
        z = _dot(xg, pst_ref[gi])
        zr_scr[...] = z[:, 0:P]
        zi_scr[...] = z[:, P:2 * P]
        ar = apow_ref[gi, 0:1, :]
        ai = apow_ref[gi, 1:2, :]

        def step(n, carry, ar=ar, ai=ai):
            sr, si = carry
            rows = pl.ds(pl.multiple_of(n * batch, batch), batch)
            sr_scr[rows, :] = sr
            si_scr[rows, :] = si
            return (ar * sr - ai * si + zr_scr[rows, :], ar * si + ai * sr + zi_scr[rows, :])

        sr, si = lax.fori_loop(0, n_chunks, step, (s0_ref[gi, 0], s0_ref[gi, 1]), unroll=4)
        sfin_ref[gi, 0] = sr
        sfin_ref[gi, 1] = si
        y = (_dot(xg, toe_ref[gi]) + _dot(sr_scr[...].astype(BF16), qst_ref[gi, 0:P, :])
             + _dot(si_scr[...].astype(BF16), qst_ref[gi, P:2 * P, :]))
        y_ref[gi] = y.astype(y_ref.dtype)


def _s5_scan(xg, s0, toe, pst, qst, apow, n_chunks, batch):
    G, R, LC = xg.shape
    P = apow.shape[-1]
    gb = max(1, min(8, S5_SCAN_ROWS_PER_STEP // R))
    blk = lambda shp: pl.BlockSpec((gb,) + shp, lambda g: (g,) + (0,) * len(shp))
    return pl.pallas_call(
        functools.partial(_s5_scan_kernel, n_chunks=n_chunks, batch=batch),
        grid=(G // gb,),
        in_specs=[blk((R, LC)), blk((2, batch, P)), blk((LC, LC)), blk((LC, 2 * P)), blk((2 * P, LC)),
                  blk((2, P))],
        out_specs=(blk((R, LC)), blk((2, batch, P))),
        out_shape=(jax.ShapeDtypeStruct((G, R, LC), BF16), jax.ShapeDtypeStruct((G, 2, batch, P), F32)),
        scratch_shapes=[pltpu.VMEM((R, P), F32)] * 4,
        compiler_params=pltpu.CompilerParams(dimension_semantics=("arbitrary",),
                                             vmem_limit_bytes=VMEM_LIMIT_BYTES),
    )(xg, s0, toe, pst, qst, apow)


def _s5_out_kernel(x_ref, yg_ref, w_ref, ln_g_ref, ln_b_ref, o_ref, y_scr):
    B, tt, D = x_ref.shape
    L, Cg = S5_CHUNK, S5_GROUP
    gpb = 128 // Cg
    cpb = tt // L
    for j in range(D // 128):
        for h in range(L // gpb):
            srcs = [yg_ref[gpb * j + gg, :, 128 * h:128 * (h + 1)].astype(F32) for gg in range(gpb)]
            for q, dest in enumerate(_granule_transpose(srcs)):
                for c in range(cpb):
                    y_scr[j, pl.ds(c * L + h * gpb + q, B, stride=tt), :] = dest[c * B:(c + 1) * B, :]
    y = jnp.concatenate([y_scr[j] for j in range(D // 128)], axis=1)
    act = _gelu_tanh(y).astype(BF16)
    zz = _dot(act, w_ref[...])
    mix = zz[:, :D] * _sigmoid(zz[:, D:])
    out = _layer_norm(ALPHA * x_ref[...].reshape(B * tt, D) + mix, ln_g_ref[...], ln_b_ref[...])
    o_ref[...] = out.reshape(B, tt, D)


def _s5_out(x, yg, w_out, ln_g, ln_b, tt):
    B, T, D = x.shape
    G, _, LC = yg.shape
    rows = (tt // S5_CHUNK) * B
    full = lambda a: pl.BlockSpec(a.shape, lambda i: (0,) * a.ndim)
    return pl.pallas_call(
        _s5_out_kernel,
        grid=(T // tt,),
        in_specs=[pl.BlockSpec((B, tt, D), lambda i: (0, i, 0)), pl.BlockSpec((G, rows, LC), lambda i: (0, i, 0)),
                  full(w_out), full(ln_g), full(ln_b)],
        out_specs=pl.BlockSpec((B, tt, D), lambda i: (0, i, 0)),
        out_shape=jax.ShapeDtypeStruct((B, T, D), F32),
        scratch_shapes=[pltpu.VMEM((D // 128, B * tt, 128), F32)],
        compiler_params=pltpu.CompilerParams(dimension_semantics=("arbitrary",),
                                             vmem_limit_bytes=VMEM_LIMIT_BYTES),
    )(x, yg, w_out, ln_g, ln_b)


def _s5_mixer(x, s0_re, s0_im, w_in, w_out, s5w, ln_g, ln_b):
    B, T, D = x.shape
    toe, pst, qst, apow = s5w
    xg = _s5_in(x, w_in, S5_STEP_TOKENS)
    s0 = jnp.stack([s0_re, s0_im], axis=0).transpose(2, 0, 1, 3)
    yg, sfin = _s5_scan(xg, s0, toe, pst, qst, apow, T // S5_CHUNK, B)
    out = _s5_out(x, yg, w_out, ln_g, ln_b, S5_STEP_TOKENS)
    sfin = sfin.transpose(1, 2, 0, 3)
    return out, sfin[0], sfin[1]


def _block_diag(w):
    H, d, _ = w.shape
    eye = jnp.eye(H, dtype=w.dtype)
    return (eye[:, None, :, None] * w[:, :, None, :]).reshape(H * d, H * d)


def kernel(x_prompt, x_sample, state_lru_h, state_lru_conv, state_hgrn2, state_s5_re, state_s5_im, ln_mix_g, ln_mix_b, ln_ffn_g, ln_ffn_b, w_in_ab, w_out_ab, lru_conv_w, lru_conv_b, lru_w_r, lru_b_r, lru_w_i, lru_b_i, lru_lam, hgrn_lb_logits, hgrn_norm_g, w_in_s5, w_out_s5, s5_a_re, s5_a_im, s5_b_re, s5_b_im, s5_c_re, s5_c_im, s5_d, s5_log_dt, router_w, router_b, moe_w_gate, moe_w_up, moe_w_down):
    Bp, Tp, D = x_prompt.shape
    Bs, Ts, _ = x_sample.shape
    W = state_lru_h.shape[-1]
    dk = W // HG_HEADS
    row = lambda v: v.reshape(1, -1)

    def moe(y, l, sparse):
        fn = _moe_sparse if sparse else functools.partial(_moe_dense, tm=512)
        out = fn(y.reshape(-1, D), router_w, row(router_b), moe_w_gate, moe_w_up, moe_w_down, l,
                 row(ln_ffn_g[l]), row(ln_ffn_b[l]))
        return out.reshape(y.shape)

    j = 0
    wts = (w_in_ab[j].astype(BF16), w_out_ab[j].astype(BF16), lru_conv_w[j], row(lru_conv_b[j]),
           _block_diag(lru_w_r[j]).astype(BF16), row(lru_b_r[j]),
           _block_diag(lru_w_i[j]).astype(BF16), row(lru_b_i[j]),
           row(lru_lam[j]), hgrn_lb_logits, row(hgrn_norm_g[j]), row(ln_mix_g[0]), row(ln_mix_b[0]))
    yp, p_h, p_conv, p_hg = _ab_mixer(
        x_prompt, jnp.zeros((Bp, 1, W), F32), jnp.zeros((Bp, 3, W), F32),
        jnp.zeros((Bp, HG_HEADS, dk, dk), F32), wts, AB_TILE, AB_SEQS)
    ys, s_h, s_conv, s_hg = _ab_mixer(
        x_sample, state_lru_h[j][:, None, :], state_lru_conv[j], state_hgrn2[j], wts, Ts, 2)
    yp = moe(yp, 0, True)
    ys = moe(ys, 0, False)

    s5w = _s5_weights(s5_a_re[j], s5_a_im[j], s5_log_dt[j], s5_b_re[j], s5_b_im[j],
                      s5_c_re[j], s5_c_im[j], s5_d[j])
    G, P = s5_a_re[j].shape
    w_in5 = w_in_s5[j].astype(BF16)
    w_out5 = w_out_s5[j].astype(BF16)
    zp = jnp.zeros((Bp, G, P), F32)
    yp, p_re, p_im = _s5_mixer(yp, zp, zp, w_in5, w_out5, s5w, row(ln_mix_g[1]), row(ln_mix_b[1]))
    ys, s_re, s_im = _s5_mixer(ys, state_s5_re[j], state_s5_im[j], w_in5, w_out5, s5w,
                               row(ln_mix_g[1]), row(ln_mix_b[1]))
    yp = moe(yp, 1, True)
    ys = moe(ys, 1, False)

    return (yp, ys,
            p_h.reshape(1, Bp, W), p_conv[None], p_hg[None], p_re[None], p_im[None],
            s_h.reshape(1, Bs, W), s_conv[None], s_hg[None], s_re[None], s_im[None])
```
